```python
import math
import jax, jax.numpy as jnp
from jax import lax
import numpy as np

D_MODEL = 2048
BATCH = 8
SEQ = 4096
DEPTH = 2

D_MIX = D_MODEL
HEAD_DIM = 128
NSA_WIDTH = D_MIX // 2
NSA_HEADS = NSA_WIDTH // HEAD_DIM
NSA_KV_GROUPS = 2
NSA_HPG = NSA_HEADS // NSA_KV_GROUPS
KV_WIDTH = NSA_KV_GROUPS * HEAD_DIM
CMP_LEN = 32
CMP_STRIDE = 16
SLC_LEN = 64
SLC_TOPK = 16
WIN_LEN = 512
WIN_QBLOCK = 128
SLC_QCHUNK = 32
ROPE_THETA = 10000.0
SSM_WIDTH = D_MIX // 4
SSM_HEAD_DIM = 64
SSM_HEADS = SSM_WIDTH // SSM_HEAD_DIM
SSM_GROUPS = 2
SSM_STATE = 128
SSM_CONV = 4
SSM_CHUNK = 128
SSM_CONV_DIM = SSM_WIDTH + 2 * SSM_GROUPS * SSM_STATE
POOL_WIDTH = D_MIX - NSA_WIDTH - SSM_WIDTH
POOL_WINDOWS = (2, 4, 8, 16)
POOL_GROUP = POOL_WIDTH // len(POOL_WINDOWS)
D_FF = ((8 * D_MODEL // 3 + 255) // 256) * 256
FFN_RESID = 0.5
RMS_EPS = 1e-6
NEG = -1e30
IN_WIDTHS = (NSA_WIDTH, KV_WIDTH, KV_WIDTH, KV_WIDTH, KV_WIDTH, KV_WIDTH, KV_WIDTH,
             3 * NSA_HEADS, SSM_WIDTH, SSM_CONV_DIM, SSM_HEADS, POOL_WIDTH)
N_IN = sum(IN_WIDTHS)

kernel_name = "hymba_nsa_ssd_pool_macaron"


def rmsnorm(x, g):
    xf = x.astype(jnp.float32)
    y = xf * lax.rsqrt(jnp.mean(xf * xf, -1, keepdims=True) + RMS_EPS)
    return (y * g.astype(jnp.float32)).astype(x.dtype)


def swiglu(x, wg, wu, wd):
    return (jax.nn.silu(x @ wg) * (x @ wu)) @ wd


def rope_tables(T, dtype):
    inv = ROPE_THETA ** (-jnp.arange(0, HEAD_DIM, 2, dtype=jnp.float32) / HEAD_DIM)
    ang = jnp.arange(T, dtype=jnp.float32)[:, None] * inv[None, :]
    ang = jnp.concatenate([ang, ang], -1)
    return jnp.cos(ang).astype(dtype), jnp.sin(ang).astype(dtype)


def apply_rope(x, cos, sin):
    shape = (1, cos.shape[0]) + (1,) * (x.ndim - 3) + (HEAD_DIM,)
    c, s = cos.reshape(shape), sin.reshape(shape)
    x1, x2 = jnp.split(x, 2, -1)
    return x * c + jnp.concatenate([-x2, x1], -1) * s


def compress(kv, pe, w1, w2):
    nb, T, G, Dh = kv.shape
    r = CMP_LEN // CMP_STRIDE
    nch = T // CMP_STRIDE
    nc = nch - r + 1
    c = kv.reshape(nb, nch, CMP_STRIDE, G, Dh)
    blocks = jnp.concatenate([c[:, k:k + nc] for k in range(r)], axis=2)
    blocks = blocks + pe[None, None, :, None, :]
    flat = blocks.transpose(0, 1, 3, 2, 4).reshape(nb, nc, G, CMP_LEN * Dh)
    return jax.nn.silu(flat @ w1) @ w2


def nsa_mixer(q, k_cmp, v_cmp, k_slc, v_slc, k_win, v_win, gates,
              pe_k, pe_v, wk1, wk2, wv1, wv2, cos, sin):
    nb, T = q.shape[:2]
    G, HPG, Dh = NSA_KV_GROUPS, NSA_HPG, HEAD_DIM
    scale = Dh ** -0.5
    pos = jnp.arange(T)
    q = q.reshape(nb, T, G, HPG, Dh)
    k_cmp, v_cmp, k_slc, v_slc, k_win, v_win = [
        a.reshape(nb, T, G, Dh) for a in (k_cmp, v_cmp, k_slc, v_slc, k_win, v_win)]

    kc = compress(k_cmp, pe_k, wk1, wk2)
    vc = compress(v_cmp, pe_v, wv1, wv2)
    nc = kc.shape[1]
    blk_start = jnp.arange(nc) * CMP_STRIDE
    cmp_ok = (blk_start + CMP_LEN - 1)[None, :] <= pos[:, None]
    s = jnp.einsum('btghd,bngd->bghtn', q, kc).astype(jnp.float32) * scale
    s = jnp.where(cmp_ok, s, NEG)
    p_cmp = jnp.where(cmp_ok, jax.nn.softmax(s, -1), 0.0)
    o_cmp = jnp.einsum('bghtn,bngd->btghd', p_cmp.astype(vc.dtype), vc)

    n_slc = T // SLC_LEN
    j = jnp.arange(n_slc)
    overlap = ((blk_start[:, None] < (j * SLC_LEN + SLC_LEN)[None, :]) &
               ((blk_start + CMP_LEN)[:, None] > (j * SLC_LEN)[None, :])).astype(jnp.float32)
    imp = jnp.einsum('bgtn,nj->bgtj', p_cmp.sum(2), overlap)
    qblk = pos // SLC_LEN
    valid = j[None, :] <= qblk[:, None]
    forced = (j[None, :] == 0) | (j[None, :] == qblk[:, None]) | (j[None, :] == qblk[:, None] - 1)
    score = jnp.where(valid, jnp.where(forced, 1e9, imp), -1e9)
    topk = min(SLC_TOPK, n_slc)
    top_val, top_idx = lax.top_k(score, topk)
    sel_ok = top_val > -1e8

    q_rot = apply_rope(q, cos, sin)
    ks = apply_rope(k_slc, cos, sin)
    kb = ks.reshape(nb, n_slc, SLC_LEN, G, Dh).transpose(0, 3, 1, 2, 4)
    vb = v_slc.reshape(nb, n_slc, SLC_LEN, G, Dh).transpose(0, 3, 1, 2, 4)
    nq = T // SLC_QCHUNK
    qc = q_rot.reshape(nb, nq, SLC_QCHUNK, G, HPG, Dh).transpose(1, 0, 3, 2, 4, 5)
    ic = top_idx.reshape(nb, G, nq, SLC_QCHUNK, topk).transpose(2, 0, 1, 3, 4)
    okc = sel_ok.reshape(nb, G, nq, SLC_QCHUNK, topk).transpose(2, 0, 1, 3, 4)
    pc = pos.reshape(nq, SLC_QCHUNK)
    bi = jnp.arange(nb)[:, None, None, None]
    gi = jnp.arange(G)[None, :, None, None]
    offs = jnp.arange(SLC_LEN)

    def slc_chunk(args):
        qx, ix, okx, px = args
        kg = kb[bi, gi, ix]
        vg = vb[bi, gi, ix]
        sc = jnp.einsum('bgqhd,bgqnsd->bghqns', qx, kg).astype(jnp.float32) * scale
        kpos = ix[..., None] * SLC_LEN + offs
        ok = ((kpos <= px[None, None, :, None, None]) & okx[..., None])[:, :, None]
        sc = jnp.where(ok, sc, NEG).reshape(nb, G, HPG, SLC_QCHUNK, topk * SLC_LEN)
        pr = jax.nn.softmax(sc, -1).reshape(nb, G, HPG, SLC_QCHUNK, topk, SLC_LEN)
        pr = jnp.where(ok, pr, 0.0)
        return jnp.einsum('bghqns,bgqnsd->bqghd', pr.astype(vg.dtype), vg)

    o_slc = lax.map(slc_chunk, (qc, ic, okc, pc))
    o_slc = o_slc.transpose(1, 0, 2, 3, 4, 5).reshape(nb, T, G, HPG, Dh)

    kw = jnp.pad(apply_rope(k_win, cos, sin), ((0, 0), (WIN_LEN, 0), (0, 0), (0, 0)))
    vw = jnp.pad(v_win, ((0, 0), (WIN_LEN, 0), (0, 0), (0, 0)))
    band = WIN_QBLOCK + WIN_LEN

    def win_block(i):
        start = i * WIN_QBLOCK
        qx = lax.dynamic_slice_in_dim(q_rot, start, WIN_QBLOCK, axis=1)
        kx = lax.dynamic_slice_in_dim(kw, start, band, axis=1)
        vx = lax.dynamic_slice_in_dim(vw, start, band, axis=1)
        qp = start + jnp.arange(WIN_QBLOCK)
        kp = start - WIN_LEN + jnp.arange(band)
        ok = ((kp[None, :] <= qp[:, None]) & (kp[None, :] > qp[:, None] - WIN_LEN)
              & (kp[None, :] >= 0))
        sc = jnp.einsum('bqghd,bkgd->bghqk', qx, kx).astype(jnp.float32) * scale
        pr = jax.nn.softmax(jnp.where(ok, sc, NEG), -1)
        return jnp.einsum('bghqk,bkgd->bqghd', pr.astype(vx.dtype), vx)

    o_win = lax.map(win_block, jnp.arange(T // WIN_QBLOCK))
    o_win = o_win.transpose(1, 0, 2, 3, 4, 5).reshape(nb, T, G, HPG, Dh)

    g = jax.nn.sigmoid(gates.reshape(nb, T, G, HPG, 3))
    o = g[..., 0:1] * o_cmp + g[..., 1:2] * o_slc + g[..., 2:3] * o_win
    return o.reshape(nb, T, NSA_WIDTH)


def segsum(x):
    L = x.shape[-1]
    cs = jnp.cumsum(x, -1)
    d = cs[..., :, None] - cs[..., None, :]
    return jnp.where(jnp.tril(jnp.ones((L, L), bool)), d, -jnp.inf)


def ssd_scan(x, a, b, c):
    nb, T, H, P = x.shape
    nc = T // SSM_CHUNK
    x = x.reshape(nb, nc, SSM_CHUNK, H, P)
    b = b.reshape(nb, nc, SSM_CHUNK, H, -1)
    c = c.reshape(nb, nc, SSM_CHUNK, H, -1)
    a = a.reshape(nb, nc, SSM_CHUNK, H).transpose(0, 3, 1, 2)
    a_cs = jnp.cumsum(a, -1)
    cb = jnp.einsum('bclhn,bcshn->bhcls', c, b) * jnp.exp(segsum(a))
    y_diag = jnp.einsum('bhcls,bcshp->bclhp', cb, x)
    decay_states = jnp.exp(a_cs[..., -1:] - a_cs)
    states = jnp.einsum('bclhn,bhcl,bclhp->bchpn', b, decay_states, x)
    states = jnp.concatenate([jnp.zeros_like(states[:, :1]), states], 1)
    chunk_decay = jnp.exp(segsum(jnp.pad(a_cs[..., -1], ((0, 0), (0, 0), (1, 0)))))
    states = jnp.einsum('bhzc,bchpn->bzhpn', chunk_decay, states)[:, :-1]
    y_off = jnp.einsum('bclhn,bchpn,bhcl->bclhp', c, states, jnp.exp(a_cs))
    return (y_diag + y_off).reshape(nb, T, H, P)


def mamba2_mixer(z, xbc, dt_raw, conv_w, conv_b, dt_bias, a_log, d_skip, norm_w):
    nb, T, _ = xbc.shape
    f32 = jnp.float32
    xbc = lax.conv_general_dilated(xbc, conv_w[:, None, :], window_strides=(1,),
                                   padding=[(SSM_CONV - 1, 0)],
                                   dimension_numbers=('NWC', 'WIO', 'NWC'),
                                   feature_group_count=SSM_CONV_DIM) + conv_b
    xbc = jax.nn.silu(xbc)
    xs, bm, cm = jnp.split(xbc, [SSM_WIDTH, SSM_WIDTH + SSM_GROUPS * SSM_STATE], -1)
    xs = xs.reshape(nb, T, SSM_HEADS, SSM_HEAD_DIM).astype(f32)
    rep = SSM_HEADS // SSM_GROUPS
    bm = jnp.repeat(bm.reshape(nb, T, SSM_GROUPS, SSM_STATE), rep, axis=2).astype(f32)
    cm = jnp.repeat(cm.reshape(nb, T, SSM_GROUPS, SSM_STATE), rep, axis=2).astype(f32)
    dt = jax.nn.softplus((dt_raw + dt_bias).astype(f32))
    a = -jnp.exp(a_log.astype(f32))
    y = ssd_scan(xs * dt[..., None], a * dt, bm, cm)
    y = y + d_skip.astype(f32)[:, None] * xs
    y = y.reshape(nb, T, SSM_WIDTH) * jax.nn.silu(z.astype(f32))
    yg = y.reshape(nb, T, SSM_GROUPS, SSM_WIDTH // SSM_GROUPS)
    yg = yg * lax.rsqrt(jnp.mean(yg * yg, -1, keepdims=True) + RMS_EPS)
    return (yg.reshape(nb, T, SSM_WIDTH) * norm_w.astype(f32)).astype(z.dtype)


def pool_mixer(v, pool_w, pool_scale):
    nb, T, _ = v.shape
    f32 = jnp.float32
    vf = v.astype(f32).reshape(nb, T, len(POOL_WINDOWS), POOL_GROUP)
    cs = jnp.pad(jnp.cumsum(vf, 1), ((0, 0), (1, 0), (0, 0), (0, 0)))
    pos = jnp.arange(T)
    outs = []
    for gi, w in enumerate(POOL_WINDOWS):
        cur = cs[:, 1:, gi]
        lag = jnp.pad(cs[:, :T + 1 - w, gi], ((0, 0), (w - 1, 0), (0, 0)))
        cnt = jnp.minimum(pos + 1, w).astype(f32)[None, :, None]
        outs.append((cur - lag) / cnt - vf[:, :, gi])
    d = jnp.stack(outs, 2)
    y = jnp.einsum('btgc,gcd->btgd', d, pool_w.astype(f32)).reshape(nb, T, POOL_WIDTH)
    return (y * pool_scale.astype(f32)).astype(v.dtype)


def token_mixer(u, w_in, pe_k, pe_v, wk1, wk2, wv1, wv2, conv_w, conv_b, dt_bias,
                a_log, d_skip, ssm_norm, pool_w, pool_scale, w_out, cos, sin):
    nb, T, _ = u.shape
    proj = u @ w_in
    (q, kc, vc, ks, vs, kw, vw, gates, z, xbc, dt, pv) = jnp.split(
        proj, np.cumsum(IN_WIDTHS)[:-1].tolist(), -1)
    o_nsa = nsa_mixer(q, kc, vc, ks, vs, kw, vw, gates, pe_k, pe_v, wk1, wk2, wv1, wv2, cos, sin)
    o_ssm = mamba2_mixer(z, xbc, dt, conv_w, conv_b, dt_bias, a_log, d_skip, ssm_norm)
    o_pool = pool_mixer(pv, pool_w, pool_scale)
    return jnp.concatenate([o_nsa, o_ssm, o_pool], -1) @ w_out


def setup_inputs(seed: int = 0) -> dict:
    key = jax.random.key(seed)
    keys = iter(jax.random.split(key, 40))
    f32 = jnp.float32
    L = DEPTH

    def nrm(shape, fan_in):
        return jax.random.normal(next(keys), shape, f32) * fan_in ** -0.5

    def gain(shape):
        return 1.0 + 0.02 * jax.random.normal(next(keys), shape, f32)

    x = jax.random.normal(next(keys), (BATCH, SEQ, D_MODEL), f32)
    dt0 = jnp.exp(jax.random.uniform(next(keys), (L, SSM_HEADS), f32,
                                     minval=math.log(1e-3), maxval=math.log(1e-1)))
    return {
        'x': x,
        'ffn1_norm_pre': gain((L, D_MODEL)),
        'ffn1_norm_post': gain((L, D_MODEL)),
        'ffn1_w_gate': nrm((L, D_MODEL, D_FF), D_MODEL),
        'ffn1_w_up': nrm((L, D_MODEL, D_FF), D_MODEL),
        'ffn1_w_down': nrm((L, D_FF, D_MODEL), D_FF),
        'mix_norm_pre': gain((L, D_MODEL)),
        'mix_norm_post': gain((L, D_MODEL)),
        'w_in': nrm((L, D_MODEL, N_IN), D_MODEL),
        'cmp_pe_k': 0.02 * jax.random.normal(next(keys), (L, CMP_LEN, HEAD_DIM), f32),
        'cmp_pe_v': 0.02 * jax.random.normal(next(keys), (L, CMP_LEN, HEAD_DIM), f32),
        'cmp_k_w1': nrm((L, CMP_LEN * HEAD_DIM, HEAD_DIM), CMP_LEN * HEAD_DIM),
        'cmp_k_w2': nrm((L, HEAD_DIM, HEAD_DIM), HEAD_DIM),
        'cmp_v_w1': nrm((L, CMP_LEN * HEAD_DIM, HEAD_DIM), CMP_LEN * HEAD_DIM),
        'cmp_v_w2': nrm((L, HEAD_DIM, HEAD_DIM), HEAD_DIM),
        'ssm_conv_w': nrm((L, SSM_CONV, SSM_CONV_DIM), SSM_CONV),
        'ssm_conv_b': 0.01 * jax.random.normal(next(keys), (L, SSM_CONV_DIM), f32),
        'ssm_dt_bias': dt0 + jnp.log(-jnp.expm1(-dt0)),
        'ssm_a_log': jnp.log(jax.random.uniform(next(keys), (L, SSM_HEADS), f32, minval=1.0, maxval=16.0)),
        'ssm_d': gain((L, SSM_HEADS)),
        'ssm_norm': gain((L, SSM_WIDTH)),
        'pool_w': nrm((L, len(POOL_WINDOWS), POOL_GROUP, POOL_GROUP), POOL_GROUP),
        'pool_scale': gain((L, POOL_WIDTH)),
        'w_out': nrm((L, D_MIX, D_MODEL), D_MIX),
        'ffn2_norm_pre': gain((L, D_MODEL)),
        'ffn2_norm_post': gain((L, D_MODEL)),
        'ffn2_w_gate': nrm((L, D_MODEL, D_FF), D_MODEL),
        'ffn2_w_up': nrm((L, D_MODEL, D_FF), D_MODEL),
        'ffn2_w_down': nrm((L, D_FF, D_MODEL), D_FF),
    }


def reference(x, ffn1_norm_pre, ffn1_norm_post, ffn1_w_gate, ffn1_w_up, ffn1_w_down,
              mix_norm_pre, mix_norm_post, w_in, cmp_pe_k, cmp_pe_v, cmp_k_w1, cmp_k_w2,
              cmp_v_w1, cmp_v_w2, ssm_conv_w, ssm_conv_b, ssm_dt_bias, ssm_a_log, ssm_d,
              ssm_norm, pool_w, pool_scale, w_out, ffn2_norm_pre, ffn2_norm_post,
              ffn2_w_gate, ffn2_w_up, ffn2_w_down):
    cos, sin = rope_tables(x.shape[1], x.dtype)
    h = x
    for i in range(DEPTH):
        f = swiglu(rmsnorm(h, ffn1_norm_pre[i]), ffn1_w_gate[i], ffn1_w_up[i], ffn1_w_down[i])
        h = h + FFN_RESID * rmsnorm(f, ffn1_norm_post[i])
        m = token_mixer(rmsnorm(h, mix_norm_pre[i]), w_in[i], cmp_pe_k[i], cmp_pe_v[i],
                        cmp_k_w1[i], cmp_k_w2[i], cmp_v_w1[i], cmp_v_w2[i], ssm_conv_w[i],
                        ssm_conv_b[i], ssm_dt_bias[i], ssm_a_log[i], ssm_d[i], ssm_norm[i],
                        pool_w[i], pool_scale[i], w_out[i], cos, sin)
        h = h + rmsnorm(m, mix_norm_post[i])
        f = swiglu(rmsnorm(h, ffn2_norm_pre[i]), ffn2_w_gate[i], ffn2_w_up[i], ffn2_w_down[i])
        h = h + FFN_RESID * rmsnorm(f, ffn2_norm_post[i])
    return h
```

```python
import functools
import math

import jax
import jax.numpy as jnp
from jax import lax
from jax.experimental import pallas as pl
from jax.experimental.pallas import tpu as pltpu

F32 = jnp.float32
BF16 = jnp.bfloat16

D_MODEL = 2048
HEAD_DIM = 128
NSA_WIDTH = 1024
NSA_HEADS = 8
NSA_GROUPS = 2
NSA_HPG = NSA_HEADS // NSA_GROUPS
KV_WIDTH = NSA_GROUPS * HEAD_DIM
CMP_LEN = 32
CMP_STRIDE = 16
SLC_LEN = 64
SLC_TOPK = 16
WIN_LEN = 512
ROPE_THETA = 10000.0
SSM_WIDTH = 512
SSM_HEAD_DIM = 64
SSM_HEADS = 8
SSM_GROUPS = 2
SSM_STATE = 128
SSM_CONV = 4
SSM_CONV_DIM = SSM_WIDTH + 2 * SSM_GROUPS * SSM_STATE
POOL_WIDTH = 512
POOL_WINDOWS = (2, 4, 8, 16)
POOL_GROUP = POOL_WIDTH // len(POOL_WINDOWS)
D_FF = 5632
FFN_RESID = 0.5
RMS_EPS = 1e-6
NEG = -1e30
IN_WIDTHS = (NSA_WIDTH, KV_WIDTH, KV_WIDTH, KV_WIDTH, KV_WIDTH, KV_WIDTH, KV_WIDTH,
             3 * NSA_HEADS, SSM_WIDTH, SSM_CONV_DIM, SSM_HEADS, POOL_WIDTH)

LANES = 128
SUBLANES = 8
VMEM_LIMIT_BYTES = 60000 * 1024

A_Q, A_KS, A_VS, A_KW, A_VW = 0, 1024, 1280, 1536, 1792
A_WIDTH = 2048
B_XBC, B_Z, B_POOL, B_DT, B_GATE, B_KC, B_VC = 0, 1024, 1536, 2048, 2176, 2432, 2688
B_WIDTH = 3072

ROW_TILE = 512
FFN_TILE = 512
PROJ_COL_TILE = 1024
ATT_TILE = 256
SSD_CHUNK = 128
POOL_TILE = 512
POOL_HALO = 16
CONV_HALO = 8


def _params(semantics):
    return pltpu.CompilerParams(dimension_semantics=semantics, vmem_limit_bytes=VMEM_LIMIT_BYTES)


def _sigmoid(x):
    return 1.0 / (1.0 + jnp.exp(-x))


def _silu(x):
    return x * _sigmoid(x)


def _rms(x):
    return x * lax.rsqrt(jnp.mean(x * x, axis=-1, keepdims=True) + RMS_EPS)


def _dot(a, b):
    return jnp.dot(a, b, preferred_element_type=F32)


def _dot_nt(a, b):
    return lax.dot_general(a, b, (((1,), (1,)), ((), ())), preferred_element_type=F32)


def _split3(x):
    hi = x.astype(BF16)
    r = x - hi.astype(F32)
    mid = r.astype(BF16)
    lo = (r - mid.astype(F32)).astype(BF16)
    return hi, mid, lo


def _dot_exact_rhs(x, sel):
    hi, mid, lo = _split3(x)
    return _dot(hi, sel) + _dot(mid, sel) + _dot(lo, sel)


def _ffn_kernel(x_ref, gpre_ref, gpost_ref, wg_ref, wu_ref, wd_ref, o_ref, xn_ref, acc_ref):
    j = pl.program_id(1)

    @pl.when(j == 0)
    def _():
        xn_ref[...] = (_rms(x_ref[...]) * gpre_ref[...]).astype(BF16)
        acc_ref[...] = jnp.zeros_like(acc_ref)

    xn = xn_ref[...]
    g = _dot(xn, wg_ref[...])
    u = _dot(xn, wu_ref[...])
    a = (_silu(g) * u).astype(BF16)
    acc_ref[...] += _dot(a, wd_ref[...])

    @pl.when(j == pl.num_programs(1) - 1)
    def _():
        o_ref[...] = x_ref[...] + FFN_RESID * (_rms(acc_ref[...]) * gpost_ref[...])


def _ffn(x, gpre, gpost, wg, wu, wd):
    n, d = x.shape
    dff = wg.shape[1]
    tm, tf = min(ROW_TILE, n), FFN_TILE
    return pl.pallas_call(
        _ffn_kernel,
        grid=(n // tm, dff // tf),
        in_specs=[
            pl.BlockSpec((tm, d), lambda i, j: (i, 0)),
            pl.BlockSpec((1, d), lambda i, j: (0, 0)),
            pl.BlockSpec((1, d), lambda i, j: (0, 0)),
            pl.BlockSpec((d, tf), lambda i, j: (0, j)),
            pl.BlockSpec((d, tf), lambda i, j: (0, j)),
            pl.BlockSpec((tf, d), lambda i, j: (j, 0)),
        ],
        out_specs=pl.BlockSpec((tm, d), lambda i, j: (i, 0)),
        out_shape=jax.ShapeDtypeStruct((n, d), F32),
        scratch_shapes=[pltpu.VMEM((tm, d), BF16), pltpu.VMEM((tm, d), F32)],
        compiler_params=_params(("parallel", "arbitrary")),
        name="ffn",
    )(x, gpre, gpost, wg, wu, wd)


def _norm_matmul_kernel(x_ref, g_ref, w_ref, o_ref, xn_ref):
    @pl.when(pl.program_id(1) == 0)
    def _():
        xn_ref[...] = (_rms(x_ref[...]) * g_ref[...]).astype(BF16)

    o_ref[...] = _dot(xn_ref[...], w_ref[...]).astype(o_ref.dtype)


def _norm_matmul(x, g, w, out_dtype, name):
    n, d = x.shape
    wout = w.shape[1]
    tm, tn = min(ROW_TILE, n), PROJ_COL_TILE
    return pl.pallas_call(
        _norm_matmul_kernel,
        grid=(n // tm, wout // tn),
        in_specs=[
            pl.BlockSpec((tm, d), lambda i, j: (i, 0)),
            pl.BlockSpec((1, d), lambda i, j: (0, 0)),
            pl.BlockSpec((d, tn), lambda i, j: (0, j)),
        ],
        out_specs=pl.BlockSpec((tm, tn), lambda i, j: (i, j)),
        out_shape=jax.ShapeDtypeStruct((n, wout), out_dtype),
        scratch_shapes=[pltpu.VMEM((tm, d), BF16)],
        compiler_params=_params(("parallel", "arbitrary")),
        name=name,
    )(x, g, w)


def _out_proj_kernel(h_ref, a_ref, s_ref, p_ref, w_ref, g_ref, o_ref):
    m = _dot(a_ref[...], w_ref[0:NSA_WIDTH, :])
    m += _dot(s_ref[...], w_ref[NSA_WIDTH:NSA_WIDTH + SSM_WIDTH, :])
    m += _dot(p_ref[...], w_ref[NSA_WIDTH + SSM_WIDTH:, :])
    o_ref[...] = h_ref[...] + _rms(m) * g_ref[...]


def _out_proj(h, o_nsa, o_ssm, o_pool, w, g):
    n, d = h.shape
    tm = min(ROW_TILE, n)
    return pl.pallas_call(
        _out_proj_kernel,
        grid=(n // tm,),
        in_specs=[
            pl.BlockSpec((tm, d), lambda i: (i, 0)),
            pl.BlockSpec((tm, NSA_WIDTH), lambda i: (i, 0)),
            pl.BlockSpec((tm, SSM_WIDTH), lambda i: (i, 0)),
            pl.BlockSpec((tm, POOL_WIDTH), lambda i: (i, 0)),
            pl.BlockSpec((d, d), lambda i: (0, 0)),
            pl.BlockSpec((1, d), lambda i: (0, 0)),
        ],
        out_specs=pl.BlockSpec((tm, d), lambda i: (i, 0)),
        out_shape=jax.ShapeDtypeStruct((n, d), F32),
        compiler_params=_params(("parallel",)),
        name="out_proj",
    )(h, o_nsa, o_ssm, o_pool, w, g)


def _pool_kernel(halo_ref, x_ref, w_ref, scale_ref, o_ref, *, tiles_per_seq):
    tile = x_ref.shape[0]
    it = pl.program_id(0) % tiles_per_seq
    halo = jnp.where(it == 0, 0.0, halo_ref[...])
    ext = jnp.concatenate([halo, x_ref[...]], axis=0)
    pos = it * tile + lax.broadcasted_iota(jnp.int32, (tile, POOL_GROUP), 0)
    for gi, w in enumerate(POOL_WINDOWS):
        xg = ext[:, gi * POOL_GROUP:(gi + 1) * POOL_GROUP]
        s, k = xg, 1
        while k < w:
            s = s + pltpu.roll(s, k, axis=0)
            k *= 2
        cnt = jnp.minimum(pos + 1, w).astype(F32)
        dlt = s[POOL_HALO:] / cnt - xg[POOL_HALO:]
        y = _dot(dlt.astype(BF16), w_ref[gi])
        sl = slice(gi * POOL_GROUP, (gi + 1) * POOL_GROUP)
        o_ref[:, sl] = (y * scale_ref[:, sl]).astype(o_ref.dtype)


def _pool(proj_b, pool_w, pool_scale, seq_len):
    n = proj_b.shape[0]
    tile = min(POOL_TILE, seq_len)
    tiles_per_seq = seq_len // tile
    ratio = tile // POOL_HALO
    col = B_POOL // POOL_WIDTH
    return pl.pallas_call(
        functools.partial(_pool_kernel, tiles_per_seq=tiles_per_seq),
        grid=(n // tile,),
        in_specs=[
            pl.BlockSpec((POOL_HALO, POOL_WIDTH), lambda i: (jnp.maximum(i * ratio - 1, 0), col)),
            pl.BlockSpec((tile, POOL_WIDTH), lambda i: (i, col)),
            pl.BlockSpec((len(POOL_WINDOWS), POOL_GROUP, POOL_GROUP), lambda i: (0, 0, 0)),
            pl.BlockSpec((1, POOL_WIDTH), lambda i: (0, 0)),
        ],
        out_specs=pl.BlockSpec((tile, POOL_WIDTH), lambda i: (i, 0)),
        out_shape=jax.ShapeDtypeStruct((n, POOL_WIDTH), BF16),
        compiler_params=_params(("parallel",)),
        name="pool",
    )(proj_b, proj_b, pool_w, pool_scale)


def _ssd_kernel(halo_ref, xbc_ref, z_ref, dt_ref, convw_ref, convb_ref, dtb_ref, alog_ref,
                dskip_ref, normw_ref, o_ref, state_ref):
    L = xbc_ref.shape[0]
    c = pl.program_id(1)

    @pl.when(c == 0)
    def _():
        state_ref[...] = jnp.zeros_like(state_ref)

    halo = jnp.where(c == 0, 0.0, halo_ref[...])
    ext = jnp.concatenate([halo, xbc_ref[...]], axis=0)
    conv = convb_ref[...] + convw_ref[SSM_CONV - 1:SSM_CONV, :] * ext[CONV_HALO:]
    for k in range(1, SSM_CONV):
        conv += convw_ref[SSM_CONV - 1 - k:SSM_CONV - k, :] * pltpu.roll(ext, k, axis=0)[CONV_HALO:]
    act = _silu(conv)
    xs = act[:, :SSM_WIDTH]
    bm = act[:, SSM_WIDTH:SSM_WIDTH + SSM_GROUPS * SSM_STATE]
    cm = act[:, SSM_WIDTH + SSM_GROUPS * SSM_STATE:]

    lane = lax.broadcasted_iota(jnp.int32, (1, LANES), 1)
    xdt = dt_ref[...] + dtb_ref[...]
    dt = jnp.maximum(xdt, 0.0) + jnp.log1p(jnp.exp(-jnp.abs(xdt)))
    a_head = jnp.where(lane < SSM_HEADS, -jnp.exp(alog_ref[...]), 0.0)
    cs = a_head * dt
    row = lax.broadcasted_iota(jnp.int32, (L, LANES), 0)
    k = 1
    while k < L:
        cs = cs + jnp.where(row >= k, pltpu.roll(cs, k, axis=0), 0.0)
        k *= 2
    cs_t = cs.T

    er = lax.broadcasted_iota(jnp.int32, (LANES, SSM_WIDTH), 0)
    ec = lax.broadcasted_iota(jnp.int32, (LANES, SSM_WIDTH), 1)
    expand = jnp.where(er == ec // SSM_HEAD_DIM, 1.0, 0.0).astype(BF16)
    dtx = _dot_exact_rhs(dt, expand)
    csx = _dot_exact_rhs(cs, expand)
    cs_last = csx[L - 1:L, :]

    x_dt = xs * dtx
    x_end = (x_dt * jnp.exp(cs_last - csx)).astype(BF16)
    x_dt16 = x_dt.astype(BF16)
    ecs = jnp.exp(csx)
    li = lax.broadcasted_iota(jnp.int32, (L, L), 0)
    si = lax.broadcasted_iota(jnp.int32, (L, L), 1)
    gw = SSM_WIDTH // SSM_GROUPS
    hpg = SSM_HEADS // SSM_GROUPS
    lane_g = lax.broadcasted_iota(jnp.int32, (1, gw), 1)
    ys = []
    for g in range(SSM_GROUPS):
        bg = bm[:, g * SSM_STATE:(g + 1) * SSM_STATE]
        cg = cm[:, g * SSM_STATE:(g + 1) * SSM_STATE].astype(BF16)
        gs = slice(g * gw, (g + 1) * gw)
        cb = _dot_nt(cg, bg.astype(BF16))
        y_g = jnp.zeros((L, gw), F32)
        for hh in range(hpg):
            h = g * hpg + hh
            seg = cs[:, h:h + 1] - cs_t[h:h + 1, :]
            m_h = (cb * jnp.exp(jnp.where(li >= si, seg, NEG))).astype(BF16)
            x_h = jnp.where(lane_g // SSM_HEAD_DIM == hh, x_dt16[:, gs], jnp.zeros((), BF16))
            y_g += _dot(m_h, x_h)
        st = state_ref[g]
        y_g += _dot(cg, st.astype(BF16)) * ecs[:, gs]
        state_ref[g] = jnp.exp(cs_last[:, gs]) * st + _dot(bg.T.astype(BF16), x_end[:, gs])
        ys.append(y_g)
    y = jnp.concatenate(ys, axis=1) + dskip_ref[...] * xs
    y = y * _silu(z_ref[...])
    outs = []
    for g in range(SSM_GROUPS):
        outs.append(_rms(y[:, g * gw:(g + 1) * gw]))
    o_ref[...] = (jnp.concatenate(outs, axis=1) * normw_ref[...]).astype(o_ref.dtype)


def _ssd(proj_b, conv_w, conv_b, dt_bias, a_log, d_skip, norm_w, batch, seq_len):
    n = proj_b.shape[0]
    L = SSD_CHUNK
    nch = seq_len // L
    ratio = L // CONV_HALO
    return pl.pallas_call(
        _ssd_kernel,
        grid=(batch, nch),
        in_specs=[
            pl.BlockSpec((CONV_HALO, SSM_CONV_DIM),
                         lambda b, c: (jnp.maximum((b * nch + c) * ratio - 1, 0), B_XBC // SSM_CONV_DIM)),
            pl.BlockSpec((L, SSM_CONV_DIM), lambda b, c: (b * nch + c, B_XBC // SSM_CONV_DIM)),
            pl.BlockSpec((L, SSM_WIDTH), lambda b, c: (b * nch + c, B_Z // SSM_WIDTH)),
            pl.BlockSpec((L, LANES), lambda b, c: (b * nch + c, B_DT // LANES)),
            pl.BlockSpec((SSM_CONV, SSM_CONV_DIM), lambda b, c: (0, 0)),
            pl.BlockSpec((1, SSM_CONV_DIM), lambda b, c: (0, 0)),
            pl.BlockSpec((1, LANES), lambda b, c: (0, 0)),
            pl.BlockSpec((1, LANES), lambda b, c: (0, 0)),
            pl.BlockSpec((1, SSM_WIDTH), lambda b, c: (0, 0)),
            pl.BlockSpec((1, SSM_WIDTH), lambda b, c: (0, 0)),
        ],
        out_specs=pl.BlockSpec((L, SSM_WIDTH), lambda b, c: (b * nch + c, 0)),
        out_shape=jax.ShapeDtypeStruct((n, SSM_WIDTH), BF16),
        scratch_shapes=[pltpu.VMEM((SSM_GROUPS, SSM_STATE, SSM_WIDTH // SSM_GROUPS), F32)],
        compiler_params=_params(("parallel", "arbitrary")),
        name="ssd",
    )(proj_b, proj_b, proj_b, proj_b, conv_w, conv_b, dt_bias, a_log, d_skip, norm_w)


def _rope(x, cos, sin_signed):
    return x * cos + pltpu.roll(x, HEAD_DIM // 2, axis=1) * sin_signed


def _compress(src_ref, w1_ref, w2_ref, pe_ref):
    n_cmp = src_ref.shape[0] // CMP_STRIDE
    xs = [src_ref[pl.ds(p, n_cmp, stride=CMP_STRIDE), :] for p in range(CMP_STRIDE)]
    x = jnp.concatenate(xs, axis=1).astype(BF16)
    half = CMP_STRIDE * HEAD_DIM
    first = _dot(x, w1_ref[0:half, :])
    second = _dot(x, w1_ref[half:2 * half, :])
    pe = jnp.broadcast_to(pe_ref[...], (SUBLANES, 2 * half)).astype(BF16)
    pe_term = _dot(pe, w1_ref[...])[0:1, :]
    pre = first + pltpu.roll(second, n_cmp - 1, axis=0) + pe_term
    return _dot(_silu(pre).astype(BF16), w2_ref[...])


def _softmax_step(s, mask, v, m_ref, l_ref, acc_ref):
    reps = s.shape[1] // LANES
    s = jnp.where(mask, s, NEG)
    m_prev = m_ref[...]
    m_new = jnp.maximum(m_prev, jnp.max(s, axis=1, keepdims=True))
    p = jnp.where(mask, jnp.exp(s - pltpu.repeat(m_new, reps, axis=1)), 0.0)
    alpha = jnp.exp(m_prev - m_new)
    l_ref[...] = alpha * l_ref[...] + jnp.sum(p, axis=1, keepdims=True)
    acc_ref[...] = alpha * acc_ref[...] + _dot(p.astype(BF16), v)
    m_ref[...] = m_new


def _nsa_kernel(q_ref, gate_ref, kc_src_ref, vc_src_ref, ks_ref, vs_ref, kw_ref, vw_ref,
                cos_ref, sin_ref, w1k_ref, w2k_ref, w1v_ref, w2v_ref, pek_ref, pev_ref,
                o_ref,
                kc_scr, vc_scr, ks_scr, kw_scr, qr_scr, m_scr, l_scr, acc_scr, oc_scr, os_scr):
    tq = q_ref.shape[0]
    tk = tq
    seq = ks_ref.shape[0]
    n_cmp, n_slc = seq // CMP_STRIDE, seq // SLC_LEN
    topk = min(SLC_TOPK, n_slc)
    qi = pl.program_id(2)
    q0 = qi * tq
    scale = HEAD_DIM ** -0.5

    @pl.when(qi == 0)
    def _():
        kc_scr[...] = _compress(kc_src_ref, w1k_ref, w2k_ref, pek_ref).astype(BF16)
        vc_scr[...] = _compress(vc_src_ref, w1v_ref, w2v_ref, pev_ref).astype(BF16)

        def rope_rows(i, carry):
            rows = pl.ds(pl.multiple_of(i * tk, tk), tk)
            cos, sin = cos_ref[rows, :], sin_ref[rows, :]
            ks_scr[rows, :] = _rope(ks_ref[rows, :].astype(F32), cos, sin).astype(BF16)
            kw_scr[rows, :] = _rope(kw_ref[rows, :].astype(F32), cos, sin).astype(BF16)
            return carry

        lax.fori_loop(0, seq // tk, rope_rows, 0)

    qrows = pl.ds(pl.multiple_of(q0, tq), tq)
    tpos = q0 + lax.broadcasted_iota(jnp.int32, (tq, 1), 0)

    nblk = lax.broadcasted_iota(jnp.int32, (1, n_cmp), 1)
    cmp_ok = (nblk * CMP_STRIDE + (CMP_LEN - 1)) <= tpos
    kc = kc_scr[...]
    vc = vc_scr[...]
    p_sum = jnp.zeros((tq, n_cmp), F32)
    for h in range(NSA_HPG):
        hs = slice(h * HEAD_DIM, (h + 1) * HEAD_DIM)
        s = jnp.where(cmp_ok, _dot_nt(q_ref[:, hs], kc) * scale, NEG)
        e = jnp.where(cmp_ok, jnp.exp(s - jnp.max(s, axis=1, keepdims=True)), 0.0)
        p = e / jnp.maximum(jnp.sum(e, axis=1, keepdims=True), 1e-30)
        oc_scr[:, hs] = _dot(p.astype(BF16), vc)
        p_sum += p

    nr = lax.broadcasted_iota(jnp.int32, (n_slc, n_cmp), 1) * CMP_STRIDE
    jr = lax.broadcasted_iota(jnp.int32, (n_slc, n_cmp), 0) * SLC_LEN
    overlap_t = jnp.where((nr < jr + SLC_LEN) & (nr + CMP_LEN > jr), 1.0, 0.0).astype(BF16)
    ps_hi = p_sum.astype(BF16)
    ps_lo = (p_sum - ps_hi.astype(F32)).astype(BF16)
    imp_t = _dot_nt(overlap_t, ps_hi) + _dot_nt(overlap_t, ps_lo)

    jblk = lax.broadcasted_iota(jnp.int32, (n_slc, tq), 0)
    qblk = (q0 + lax.broadcasted_iota(jnp.int32, (n_slc, tq), 1)) // SLC_LEN
    valid = jblk <= qblk
    forced = (jblk == 0) | (jblk == qblk) | (jblk == qblk - 1)
    score = jnp.where(valid, jnp.where(forced, 1e9, imp_t), -1e9)
    rank = jnp.zeros((n_slc, tq), F32)
    for jp in range(n_slc):
        other = score[jp:jp + 1, :]
        ahead = (other > score) | ((other == score) & (jblk > jp))
        rank += jnp.where(ahead, 1.0, 0.0)
    sel_t = jnp.where(valid & (rank < topk), 1.0, 0.0).astype(BF16)
    ri = lax.broadcasted_iota(jnp.int32, (tq, tq), 0)
    ci = lax.broadcasted_iota(jnp.int32, (tq, tq), 1)
    eye = jnp.where(ri == ci, 1.0, 0.0).astype(BF16)
    sel = _dot_nt(eye, sel_t).astype(BF16)

    cos_q, sin_q = cos_ref[qrows, :], sin_ref[qrows, :]
    for h in range(NSA_HPG):
        hs = slice(h * HEAD_DIM, (h + 1) * HEAD_DIM)
        qr_scr[:, hs] = (_rope(q_ref[:, hs].astype(F32), cos_q, sin_q) * scale).astype(BF16)

    kcol = lax.broadcasted_iota(jnp.int32, (1, tk), 1)
    ej = lax.broadcasted_iota(jnp.int32, (n_slc, tk), 0)
    ek = lax.broadcasted_iota(jnp.int32, (n_slc, tk), 1) // SLC_LEN

    def reset():
        m_scr[...] = jnp.full(m_scr.shape, NEG, F32)
        l_scr[...] = jnp.zeros_like(l_scr)
        acc_scr[...] = jnp.zeros_like(acc_scr)

    def finish(dst):
        for h in range(NSA_HPG):
            hs = slice(h * HEAD_DIM, (h + 1) * HEAD_DIM)
            dst[:, hs] = acc_scr[h] / l_scr[h]

    reset()

    def slc_step(kt, carry):
        rows = pl.ds(pl.multiple_of(kt * tk, tk), tk)
        k, v = ks_scr[rows, :], vs_ref[rows, :]
        expand = jnp.where(ej == kt * (tk // SLC_LEN) + ek, 1.0, 0.0).astype(BF16)
        mask = (_dot(sel, expand) > 0.5) & ((kt * tk + kcol) <= tpos)
        for h in range(NSA_HPG):
            hs = slice(h * HEAD_DIM, (h + 1) * HEAD_DIM)
            _softmax_step(_dot_nt(qr_scr[:, hs], k), mask, v, m_scr.at[h], l_scr.at[h], acc_scr.at[h])
        return carry

    lax.fori_loop(0, qi + 1, slc_step, 0)
    finish(os_scr)

    reset()

    def win_step(i, carry):
        kt = qi - i
        rows = pl.ds(pl.multiple_of(kt * tk, tk), tk)
        k, v = kw_scr[rows, :], vw_ref[rows, :]
        kpos = kt * tk + kcol
        mask = (kpos <= tpos) & (kpos > tpos - WIN_LEN)
        for h in range(NSA_HPG):
            hs = slice(h * HEAD_DIM, (h + 1) * HEAD_DIM)
            _softmax_step(_dot_nt(qr_scr[:, hs], k), mask, v, m_scr.at[h], l_scr.at[h], acc_scr.at[h])
        return carry

    lax.fori_loop(0, jnp.minimum(qi, WIN_LEN // tk) + 1, win_step, 0)

    gates = _sigmoid(gate_ref[...])
    for h in range(NSA_HPG):
        hs = slice(h * HEAD_DIM, (h + 1) * HEAD_DIM)
        o_win = acc_scr[h] / l_scr[h]
        o = (gates[:, 3 * h:3 * h + 1] * oc_scr[:, hs]
             + gates[:, 3 * h + 1:3 * h + 2] * os_scr[:, hs]
             + gates[:, 3 * h + 2:3 * h + 3] * o_win)
        o_ref[:, hs] = o.astype(o_ref.dtype)


def _nsa(proj_a, proj_b, cos, sin_signed, w1k, w2k, w1v, w2v, pe_k, pe_v, batch, seq_len):
    n = proj_a.shape[0]
    tq = ATT_TILE
    nq = seq_len // tq
    gw = NSA_HPG * HEAD_DIM

    def kv_spec(col0):
        return pl.BlockSpec((seq_len, HEAD_DIM), lambda b, g, i: (b, col0 // HEAD_DIM + g))

    def full(shape):
        return pl.BlockSpec(shape, lambda b, g, i: (0,) * len(shape))

    return pl.pallas_call(
        _nsa_kernel,
        grid=(batch, NSA_GROUPS, nq),
        in_specs=[
            pl.BlockSpec((tq, gw), lambda b, g, i: (b * nq + i, A_Q // gw + g)),
            pl.BlockSpec((tq, LANES), lambda b, g, i: (b * nq + i, B_GATE // LANES + g)),
            kv_spec(B_KC), kv_spec(B_VC),
            kv_spec(A_KS), kv_spec(A_VS), kv_spec(A_KW), kv_spec(A_VW),
            full((seq_len, HEAD_DIM)), full((seq_len, HEAD_DIM)),
            full((CMP_LEN * HEAD_DIM, HEAD_DIM)), full((HEAD_DIM, HEAD_DIM)),
            full((CMP_LEN * HEAD_DIM, HEAD_DIM)), full((HEAD_DIM, HEAD_DIM)),
            full((1, CMP_LEN * HEAD_DIM)), full((1, CMP_LEN * HEAD_DIM)),
        ],
        out_specs=pl.BlockSpec((tq, gw), lambda b, g, i: (b * nq + i, g)),
        out_shape=jax.ShapeDtypeStruct((n, NSA_WIDTH), BF16),
        scratch_shapes=[
            pltpu.VMEM((seq_len // CMP_STRIDE, HEAD_DIM), BF16),
            pltpu.VMEM((seq_len // CMP_STRIDE, HEAD_DIM), BF16),
            pltpu.VMEM((seq_len, HEAD_DIM), BF16),
            pltpu.VMEM((seq_len, HEAD_DIM), BF16),
            pltpu.VMEM((tq, gw), BF16),
            pltpu.VMEM((NSA_HPG, tq, LANES), F32),
            pltpu.VMEM((NSA_HPG, tq, LANES), F32),
            pltpu.VMEM((NSA_HPG, tq, HEAD_DIM), F32),
            pltpu.VMEM((tq, gw), F32),
            pltpu.VMEM((tq, gw), F32),
        ],
        compiler_params=_params(("parallel", "parallel", "arbitrary")),
        name="nsa",
    )(proj_a, proj_b, proj_b, proj_b, proj_a, proj_a, proj_a, proj_a,
      cos, sin_signed, w1k, w2k, w1v, w2v, pe_k, pe_v)


def _rope_tables(seq_len):
    inv = ROPE_THETA ** (-jnp.arange(0, HEAD_DIM, 2, dtype=F32) / HEAD_DIM)
    ang = jnp.arange(seq_len, dtype=F32)[:, None] * inv[None, :]
    ang = jnp.concatenate([ang, ang], -1)
    sign = jnp.concatenate([-jnp.ones((HEAD_DIM // 2,), F32), jnp.ones((HEAD_DIM // 2,), F32)])
    return jnp.cos(ang), jnp.sin(ang) * sign


def _split_w_in(w_in):
    offs = [0]
    for wd in IN_WIDTHS:
        offs.append(offs[-1] + wd)
    (q, kc, vc, ks, vs, kw, vw, gates, z, xbc, dt, pv) = [
        w_in[:, offs[i]:offs[i + 1]] for i in range(len(IN_WIDTHS))]
    d = w_in.shape[0]

    def pad_to(w, width):
        return jnp.pad(w, ((0, 0), (0, width - w.shape[1])))

    per_group = 3 * NSA_HPG
    w_a = jnp.concatenate([q, ks, vs, kw, vw], axis=1)
    w_b = jnp.concatenate(
        [xbc, z, pv, pad_to(dt, LANES),
         pad_to(gates[:, :per_group], LANES), pad_to(gates[:, per_group:], LANES),
         kc, vc, jnp.zeros((d, B_WIDTH - B_VC - KV_WIDTH), w_in.dtype)], axis=1)
    return w_a.astype(BF16), w_b.astype(BF16)


def _pad_lanes(v):
    return jnp.pad(v, (0, LANES - v.shape[0]))[None, :]


def kernel(x, ffn1_norm_pre, ffn1_norm_post, ffn1_w_gate, ffn1_w_up, ffn1_w_down, mix_norm_pre, mix_norm_post, w_in, cmp_pe_k, cmp_pe_v, cmp_k_w1, cmp_k_w2, cmp_v_w1, cmp_v_w2, ssm_conv_w, ssm_conv_b, ssm_dt_bias, ssm_a_log, ssm_d, ssm_norm, pool_w, pool_scale, w_out, ffn2_norm_pre, ffn2_norm_post, ffn2_w_gate, ffn2_w_up, ffn2_w_down):
    batch, seq_len, d = x.shape
    depth = w_in.shape[0]
    cos, sin_signed = _rope_tables(seq_len)
    h = x.reshape(batch * seq_len, d)
    for i in range(depth):
        h = _ffn(h, ffn1_norm_pre[i][None], ffn1_norm_post[i][None],
                 ffn1_w_gate[i].astype(BF16), ffn1_w_up[i].astype(BF16), ffn1_w_down[i].astype(BF16))
        w_a, w_b = _split_w_in(w_in[i])
        g_mix = mix_norm_pre[i][None]
        proj_a = _norm_matmul(h, g_mix, w_a, BF16, "in_proj_a")
        proj_b = _norm_matmul(h, g_mix, w_b, F32, "in_proj_b")
        o_nsa = _nsa(proj_a, proj_b, cos, sin_signed,
                     cmp_k_w1[i].astype(BF16), cmp_k_w2[i].astype(BF16),
                     cmp_v_w1[i].astype(BF16), cmp_v_w2[i].astype(BF16),
                     cmp_pe_k[i].reshape(1, -1), cmp_pe_v[i].reshape(1, -1), batch, seq_len)
        o_ssm = _ssd(proj_b, ssm_conv_w[i], ssm_conv_b[i][None], _pad_lanes(ssm_dt_bias[i]),
                     _pad_lanes(ssm_a_log[i]), jnp.repeat(ssm_d[i], SSM_HEAD_DIM)[None],
                     ssm_norm[i][None], batch, seq_len)
        o_pool = _pool(proj_b, pool_w[i].astype(BF16), pool_scale[i][None], seq_len)
        h = _out_proj(h, o_nsa, o_ssm, o_pool, w_out[i].astype(BF16), mix_norm_post[i][None])
        h = _ffn(h, ffn2_norm_pre[i][None], ffn2_norm_post[i][None],
                 ffn2_w_gate[i].astype(BF16), ffn2_w_up[i].astype(BF16), ffn2_w_down[i].astype(BF16))
    return h.reshape(batch, seq_len, d)
```

```python
import functools
import math

import jax
import jax.numpy as jnp
from jax import lax
from jax.experimental import pallas as pl
from jax.experimental.pallas import tpu as pltpu

F32 = jnp.float32
BF16 = jnp.bfloat16

D_MODEL = 2048
HEAD_DIM = 128
NSA_WIDTH = 1024
NSA_HEADS = 8
NSA_GROUPS = 2
NSA_HPG = NSA_HEADS // NSA_GROUPS
KV_WIDTH = NSA_GROUPS * HEAD_DIM
CMP_LEN = 32
CMP_STRIDE = 16
SLC_LEN = 64
SLC_TOPK = 16
WIN_LEN = 512
ROPE_THETA = 10000.0
SSM_WIDTH = 512
SSM_HEAD_DIM = 64
SSM_HEADS = 8
SSM_GROUPS = 2
SSM_STATE = 128
SSM_CONV = 4
SSM_CONV_DIM = SSM_WIDTH + 2 * SSM_GROUPS * SSM_STATE
POOL_WIDTH = 512
POOL_WINDOWS = (2, 4, 8, 16)
POOL_GROUP = POOL_WIDTH // len(POOL_WINDOWS)
D_FF = 5632
FFN_RESID = 0.5
RMS_EPS = 1e-6
NEG = -1e30
LOG2E = math.log2(math.e)
IN_WIDTHS = (NSA_WIDTH, KV_WIDTH, KV_WIDTH, KV_WIDTH, KV_WIDTH, KV_WIDTH, KV_WIDTH,
             3 * NSA_HEADS, SSM_WIDTH, SSM_CONV_DIM, SSM_HEADS, POOL_WIDTH)

LANES = 128
SUBLANES = 8
VMEM_LIMIT_BYTES = 60000 * 1024

A_Q, A_KS, A_VS, A_KW, A_VW = 0, 1024, 1280, 1536, 1792
A_WIDTH = 2048
B_XBC, B_Z, B_POOL, B_DT, B_GATE, B_KC, B_VC = 0, 1024, 1536, 2048, 2176, 2432, 2688
B_WIDTH = 3072

FFN_ROW_TILE = 512
FFN_TILE = 512
ROW_TILE = 512
PROJ_COL_TILE = 1024
ATT_TILE = 256
SLC_KEY_TILE = 512
SSD_CHUNK = 128
POOL_TILE = 512
POOL_HALO = 16
CONV_HALO = 8


def _params(semantics):
    return pltpu.CompilerParams(dimension_semantics=semantics, vmem_limit_bytes=VMEM_LIMIT_BYTES)


def _sigmoid(x):
    return 1.0 / (1.0 + jnp.exp(-x))


def _silu(x):
    return x * _sigmoid(x)


def _rms(x):
    return x * lax.rsqrt(jnp.mean(x * x, axis=-1, keepdims=True) + RMS_EPS)


def _dot(a, b):
    return jnp.dot(a, b, preferred_element_type=F32)


def _dot_nt(a, b):
    return lax.dot_general(a, b, (((1,), (1,)), ((), ())), preferred_element_type=F32)


def _split3(x):
    hi = x.astype(BF16)
    r = x - hi.astype(F32)
    mid = r.astype(BF16)
    lo = (r - mid.astype(F32)).astype(BF16)
    return hi, mid, lo


def _dot_exact_rhs(x, sel):
    hi, mid, lo = _split3(x)
    return _dot(hi, sel) + _dot(mid, sel) + _dot(lo, sel)


def _ffn_kernel(x_ref, gpre_ref, gpost_ref, wg_ref, wu_ref, wd_ref, o_ref, xn_ref):
    j = pl.program_id(1)
    last = pl.num_programs(1) - 1

    def step(first, final):
        if first:
            xn = (_rms(x_ref[...]) * gpre_ref[...]).astype(BF16)
            xn_ref[...] = xn
        else:
            xn = xn_ref[...]
        g = _dot(xn, wg_ref[...])
        u = _dot(xn, wu_ref[...])
        a = (_silu(g) * u).astype(BF16)
        acc = _dot(a, wd_ref[...])
        if not first:
            acc += o_ref[...]
        if final:
            o_ref[...] = x_ref[...] + FFN_RESID * (_rms(acc) * gpost_ref[...])
        else:
            o_ref[...] = acc

    pl.when(j == 0)(functools.partial(step, True, False))
    pl.when((j > 0) & (j < last))(functools.partial(step, False, False))
    pl.when(j == last)(functools.partial(step, False, True))


def _ffn(x, gpre, gpost, wg, wu, wd):
    n, d = x.shape
    dff = wg.shape[1]
    tm, tf = min(FFN_ROW_TILE, n), FFN_TILE
    return pl.pallas_call(
        _ffn_kernel,
        grid=(n // tm, dff // tf),
        in_specs=[
            pl.BlockSpec((tm, d), lambda i, j: (i, 0)),
            pl.BlockSpec((1, d), lambda i, j: (0, 0)),
            pl.BlockSpec((1, d), lambda i, j: (0, 0)),
            pl.BlockSpec((d, tf), lambda i, j: (0, j)),
            pl.BlockSpec((d, tf), lambda i, j: (0, j)),
            pl.BlockSpec((tf, d), lambda i, j: (j, 0)),
        ],
        out_specs=pl.BlockSpec((tm, d), lambda i, j: (i, 0)),
        out_shape=jax.ShapeDtypeStruct((n, d), F32),
        scratch_shapes=[pltpu.VMEM((tm, d), BF16)],
        compiler_params=_params(("parallel", "arbitrary")),
        name="ffn",
    )(x, gpre, gpost, wg, wu, wd)


def _in_proj_kernel(x_ref, g_ref, w_ref, oa_ref, ob_ref, xn_ref, *, na):
    j = pl.program_id(1)

    @pl.when(j == 0)
    def _():
        xn_ref[...] = (_rms(x_ref[...]) * g_ref[...]).astype(BF16)

    @pl.when(j < na)
    def _():
        oa_ref[...] = _dot(xn_ref[...], w_ref[j]).astype(oa_ref.dtype)

    @pl.when(j >= na)
    def _():
        ob_ref[...] = _dot(xn_ref[...], w_ref[j])


def _in_proj(x, g, w):
    n, d = x.shape
    nt, _, tn = w.shape
    na = A_WIDTH // tn
    tm = min(ROW_TILE, n)
    return pl.pallas_call(
        functools.partial(_in_proj_kernel, na=na),
        grid=(n // tm, nt),
        in_specs=[
            pl.BlockSpec((tm, d), lambda i, j: (i, 0)),
            pl.BlockSpec((1, d), lambda i, j: (0, 0)),
            pl.BlockSpec((nt, d, tn), lambda i, j: (0, 0, 0), pipeline_mode=pl.Buffered(1)),
        ],
        out_specs=[
            pl.BlockSpec((tm, tn), lambda i, j: (i, jnp.minimum(j, na - 1))),
            pl.BlockSpec((tm, tn), lambda i, j: (i, jnp.maximum(j - na, 0))),
        ],
        out_shape=[jax.ShapeDtypeStruct((n, A_WIDTH), BF16),
                   jax.ShapeDtypeStruct((n, B_WIDTH), F32)],
        scratch_shapes=[pltpu.VMEM((tm, d), BF16)],
        compiler_params=_params(("parallel", "arbitrary")),
        name="in_proj",
    )(x, g, w)


def _out_proj_kernel(h_ref, a_ref, s_ref, p_ref, w_ref, g_ref, o_ref):
    m = _dot(a_ref[...], w_ref[0:NSA_WIDTH, :])
    m += _dot(s_ref[...], w_ref[NSA_WIDTH:NSA_WIDTH + SSM_WIDTH, :])
    m += _dot(p_ref[...], w_ref[NSA_WIDTH + SSM_WIDTH:, :])
    o_ref[...] = h_ref[...] + _rms(m) * g_ref[...]


def _out_proj(h, o_nsa, o_ssm, o_pool, w, g):
    n, d = h.shape
    tm = min(ROW_TILE, n)
    return pl.pallas_call(
        _out_proj_kernel,
        grid=(n // tm,),
        in_specs=[
            pl.BlockSpec((tm, d), lambda i: (i, 0)),
            pl.BlockSpec((tm, NSA_WIDTH), lambda i: (i, 0)),
            pl.BlockSpec((tm, SSM_WIDTH), lambda i: (i, 0)),
            pl.BlockSpec((tm, POOL_WIDTH), lambda i: (i, 0)),
            pl.BlockSpec((d, d), lambda i: (0, 0)),
            pl.BlockSpec((1, d), lambda i: (0, 0)),
        ],
        out_specs=pl.BlockSpec((tm, d), lambda i: (i, 0)),
        out_shape=jax.ShapeDtypeStruct((n, d), F32),
        compiler_params=_params(("parallel",)),
        name="out_proj",
    )(h, o_nsa, o_ssm, o_pool, w, g)


def _pool_kernel(halo_ref, x_ref, w_ref, scale_ref, o_ref, *, tiles_per_seq):
    tile = x_ref.shape[0]
    it = pl.program_id(0) % tiles_per_seq
    halo = jnp.where(it == 0, 0.0, halo_ref[...])
    ext = jnp.concatenate([halo, x_ref[...]], axis=0)
    pos = it * tile + lax.broadcasted_iota(jnp.int32, (tile, POOL_GROUP), 0)
    for gi, w in enumerate(POOL_WINDOWS):
        xg = ext[:, gi * POOL_GROUP:(gi + 1) * POOL_GROUP]
        s, k = xg, 1
        while k < w:
            s = s + pltpu.roll(s, k, axis=0)
            k *= 2
        cnt = jnp.minimum(pos + 1, w).astype(F32)
        dlt = s[POOL_HALO:] / cnt - xg[POOL_HALO:]
        y = _dot(dlt.astype(BF16), w_ref[gi])
        sl = slice(gi * POOL_GROUP, (gi + 1) * POOL_GROUP)
        o_ref[:, sl] = (y * scale_ref[:, sl]).astype(o_ref.dtype)


def _pool(proj_b, pool_w, pool_scale, seq_len):
    n = proj_b.shape[0]
    tile = min(POOL_TILE, seq_len)
    tiles_per_seq = seq_len // tile
    ratio = tile // POOL_HALO
    col = B_POOL // POOL_WIDTH
    return pl.pallas_call(
        functools.partial(_pool_kernel, tiles_per_seq=tiles_per_seq),
        grid=(n // tile,),
        in_specs=[
            pl.BlockSpec((POOL_HALO, POOL_WIDTH), lambda i: (jnp.maximum(i * ratio - 1, 0), col)),
            pl.BlockSpec((tile, POOL_WIDTH), lambda i: (i, col)),
            pl.BlockSpec((len(POOL_WINDOWS), POOL_GROUP, POOL_GROUP), lambda i: (0, 0, 0)),
            pl.BlockSpec((1, POOL_WIDTH), lambda i: (0, 0)),
        ],
        out_specs=pl.BlockSpec((tile, POOL_WIDTH), lambda i: (i, 0)),
        out_shape=jax.ShapeDtypeStruct((n, POOL_WIDTH), BF16),
        compiler_params=_params(("parallel",)),
        name="pool",
    )(proj_b, proj_b, pool_w, pool_scale)


def _ssd_kernel(halo_ref, xbc_ref, z_ref, dt_ref, convw_ref, convb_ref, dtb_ref, alog_ref,
                dskip_ref, normw_ref, o_ref, state_ref):
    L = xbc_ref.shape[0]
    c = pl.program_id(1)

    @pl.when(c == 0)
    def _():
        state_ref[...] = jnp.zeros_like(state_ref)

    halo = jnp.where(c == 0, 0.0, halo_ref[...])
    ext = jnp.concatenate([halo, xbc_ref[...]], axis=0)
    conv = convb_ref[...] + convw_ref[SSM_CONV - 1:SSM_CONV, :] * ext[CONV_HALO:]
    for k in range(1, SSM_CONV):
        conv += convw_ref[SSM_CONV - 1 - k:SSM_CONV - k, :] * pltpu.roll(ext, k, axis=0)[CONV_HALO:]
    act = _silu(conv)
    xs = act[:, :SSM_WIDTH]
    bm = act[:, SSM_WIDTH:SSM_WIDTH + SSM_GROUPS * SSM_STATE]
    cm = act[:, SSM_WIDTH + SSM_GROUPS * SSM_STATE:]

    lane = lax.broadcasted_iota(jnp.int32, (1, LANES), 1)
    xdt = dt_ref[...] + dtb_ref[...]
    dt = jnp.maximum(xdt, 0.0) + jnp.log1p(jnp.exp(-jnp.abs(xdt)))
    a_head = jnp.where(lane < SSM_HEADS, -jnp.exp(alog_ref[...]), 0.0)
    cs = a_head * dt
    row = lax.broadcasted_iota(jnp.int32, (L, LANES), 0)
    k = 1
    while k < L:
        cs = cs + jnp.where(row >= k, pltpu.roll(cs, k, axis=0), 0.0)
        k *= 2
    cs_t = cs.T

    er = lax.broadcasted_iota(jnp.int32, (LANES, SSM_WIDTH), 0)
    ec = lax.broadcasted_iota(jnp.int32, (LANES, SSM_WIDTH), 1)
    expand = jnp.where(er == ec // SSM_HEAD_DIM, 1.0, 0.0).astype(BF16)
    dtx = _dot_exact_rhs(dt, expand)
    csx = _dot_exact_rhs(cs, expand)
    cs_last = csx[L - 1:L, :]

    x_dt = xs * dtx
    x_end = (x_dt * jnp.exp(cs_last - csx)).astype(BF16)
    x_dt16 = x_dt.astype(BF16)
    ecs = jnp.exp(csx)
    li = lax.broadcasted_iota(jnp.int32, (L, L), 0)
    si = lax.broadcasted_iota(jnp.int32, (L, L), 1)
    gw = SSM_WIDTH // SSM_GROUPS
    hpg = SSM_HEADS // SSM_GROUPS
    lane_g = lax.broadcasted_iota(jnp.int32, (1, gw), 1)
    ys = []
    for g in range(SSM_GROUPS):
        bg = bm[:, g * SSM_STATE:(g + 1) * SSM_STATE]
        cg = cm[:, g * SSM_STATE:(g + 1) * SSM_STATE].astype(BF16)
        gs = slice(g * gw, (g + 1) * gw)
        cb = _dot_nt(cg, bg.astype(BF16))
        y_g = jnp.zeros((L, gw), F32)
        for hh in range(hpg):
            h = g * hpg + hh
            seg = cs[:, h:h + 1] - cs_t[h:h + 1, :]
            m_h = (cb * jnp.exp(jnp.where(li >= si, seg, NEG))).astype(BF16)
            x_h = jnp.where(lane_g // SSM_HEAD_DIM == hh, x_dt16[:, gs], jnp.zeros((), BF16))
            y_g += _dot(m_h, x_h)
        st = state_ref[g]
        y_g += _dot(cg, st.astype(BF16)) * ecs[:, gs]
        state_ref[g] = jnp.exp(cs_last[:, gs]) * st + _dot(bg.T.astype(BF16), x_end[:, gs])
        ys.append(y_g)
    y = jnp.concatenate(ys, axis=1) + dskip_ref[...] * xs
    y = y * _silu(z_ref[...])
    outs = []
    for g in range(SSM_GROUPS):
        outs.append(_rms(y[:, g * gw:(g + 1) * gw]))
    o_ref[...] = (jnp.concatenate(outs, axis=1) * normw_ref[...]).astype(o_ref.dtype)


def _ssd(proj_b, conv_w, conv_b, dt_bias, a_log, d_skip, norm_w, batch, seq_len):
    n = proj_b.shape[0]
    L = SSD_CHUNK
    nch = seq_len // L
    ratio = L // CONV_HALO
    return pl.pallas_call(
        _ssd_kernel,
        grid=(batch, nch),
        in_specs=[
            pl.BlockSpec((CONV_HALO, SSM_CONV_DIM),
                         lambda b, c: (jnp.maximum((b * nch + c) * ratio - 1, 0), B_XBC // SSM_CONV_DIM)),
            pl.BlockSpec((L, SSM_CONV_DIM), lambda b, c: (b * nch + c, B_XBC // SSM_CONV_DIM)),
            pl.BlockSpec((L, SSM_WIDTH), lambda b, c: (b * nch + c, B_Z // SSM_WIDTH)),
            pl.BlockSpec((L, LANES), lambda b, c: (b * nch + c, B_DT // LANES)),
            pl.BlockSpec((SSM_CONV, SSM_CONV_DIM), lambda b, c: (0, 0)),
            pl.BlockSpec((1, SSM_CONV_DIM), lambda b, c: (0, 0)),
            pl.BlockSpec((1, LANES), lambda b, c: (0, 0)),
            pl.BlockSpec((1, LANES), lambda b, c: (0, 0)),
            pl.BlockSpec((1, SSM_WIDTH), lambda b, c: (0, 0)),
            pl.BlockSpec((1, SSM_WIDTH), lambda b, c: (0, 0)),
        ],
        out_specs=pl.BlockSpec((L, SSM_WIDTH), lambda b, c: (b * nch + c, 0)),
        out_shape=jax.ShapeDtypeStruct((n, SSM_WIDTH), BF16),
        scratch_shapes=[pltpu.VMEM((SSM_GROUPS, SSM_STATE, SSM_WIDTH // SSM_GROUPS), F32)],
        compiler_params=_params(("parallel", "arbitrary")),
        name="ssd",
    )(proj_b, proj_b, proj_b, proj_b, conv_w, conv_b, dt_bias, a_log, d_skip, norm_w)


def _rope(x, cos, sin_signed):
    return x * cos + pltpu.roll(x, HEAD_DIM // 2, axis=1) * sin_signed


def _compress(src_ref, w1_ref, w2_ref, pe_ref):
    n_cmp = src_ref.shape[0] // CMP_STRIDE
    xs = [src_ref[pl.ds(p, n_cmp, stride=CMP_STRIDE), :] for p in range(CMP_STRIDE)]
    x = jnp.concatenate(xs, axis=1).astype(BF16)
    half = CMP_STRIDE * HEAD_DIM
    first = _dot(x, w1_ref[0:half, :])
    second = _dot(x, w1_ref[half:2 * half, :])
    pe = jnp.broadcast_to(pe_ref[...], (SUBLANES, 2 * half)).astype(BF16)
    pe_term = _dot(pe, w1_ref[...])[0:1, :]
    pre = first + pltpu.roll(second, n_cmp - 1, axis=0) + pe_term
    return _dot(_silu(pre).astype(BF16), w2_ref[...])


def _nsa_kernel(q_ref, gate_ref, kc_src_ref, vc_src_ref, ks_ref, vs_ref, kw_ref, vw_ref,
                cos_ref, sin_ref, w1k_ref, w2k_ref, w1v_ref, w2v_ref, pek_ref, pev_ref,
                o_ref,
                kc_scr, vc_scr, kx_scr, kwp_scr, vwp_scr, qx_scr, score_scr, rank_scr,
                m_scr, l_scr, acc_scr, oc_scr, ow_scr, s_scr):
    tq = q_ref.shape[0]
    seq = ks_ref.shape[0]
    tk = min(SLC_KEY_TILE, seq)
    nh = NSA_HPG
    n_cmp, n_slc = seq // CMP_STRIDE, seq // SLC_LEN
    topk = min(SLC_TOPK, n_slc)
    qi = pl.program_id(2)
    q0 = qi * tq
    qk_scale = HEAD_DIM ** -0.5 * LOG2E

    @pl.when(qi == 0)
    def _():
        kc_scr[...] = _compress(kc_src_ref, w1k_ref, w2k_ref, pek_ref).astype(BF16)
        vc_scr[...] = _compress(vc_src_ref, w1v_ref, w2v_ref, pev_ref).astype(BF16)
        kwp_scr[0:WIN_LEN, :] = jnp.zeros((WIN_LEN, HEAD_DIM), BF16)
        vwp_scr[0:WIN_LEN, :] = jnp.zeros((WIN_LEN, HEAD_DIM), BF16)

        def prep_rows(i, carry):
            rows = pl.ds(pl.multiple_of(i * tq, tq), tq)
            prows = pl.ds(pl.multiple_of(WIN_LEN + i * tq, tq), tq)
            cos, sin = cos_ref[rows, :], sin_ref[rows, :]
            kx_scr[rows, 0:HEAD_DIM] = _rope(ks_ref[rows, :].astype(F32), cos, sin).astype(BF16)
            blk = (i * tq + lax.broadcasted_iota(jnp.int32, (tq, LANES), 0)) // SLC_LEN
            lane = lax.broadcasted_iota(jnp.int32, (tq, LANES), 1)
            kx_scr[rows, HEAD_DIM:2 * HEAD_DIM] = jnp.where(lane == blk, 1.0, 0.0).astype(BF16)
            kwp_scr[prows, :] = _rope(kw_ref[rows, :].astype(F32), cos, sin).astype(BF16)
            vwp_scr[prows, :] = vw_ref[rows, :]
            return carry

        lax.fori_loop(0, seq // tq, prep_rows, 0)

    qrows = pl.ds(pl.multiple_of(q0, tq), tq)
    rcol = lax.broadcasted_iota(jnp.int32, (tq, 1), 0)
    tpos = q0 + rcol

    cos_q, sin_q = cos_ref[qrows, :], sin_ref[qrows, :]
    for h in range(nh):
        hs = slice(h * HEAD_DIM, (h + 1) * HEAD_DIM)
        qx_scr[h * tq:(h + 1) * tq, 0:HEAD_DIM] = (
            _rope(q_ref[:, hs].astype(F32), cos_q, sin_q) * qk_scale).astype(BF16)

    q_stack = jnp.concatenate([q_ref[:, h * HEAD_DIM:(h + 1) * HEAD_DIM] for h in range(nh)], axis=0)
    nblk = lax.broadcasted_iota(jnp.int32, (1, n_cmp), 1)
    cmp_bias = jnp.where((nblk * CMP_STRIDE + (CMP_LEN - 1)) <= tpos, 0.0, NEG)
    any_visible = jnp.where(tpos >= CMP_LEN - 1, 1.0, 0.0)
    s_all = _dot_nt(q_stack, kc_scr[...]) * qk_scale
    p_sum = jnp.zeros((tq, n_cmp), F32)
    ps = []
    for h in range(nh):
        s = s_all[h * tq:(h + 1) * tq] + cmp_bias
        e = jnp.exp2(s - jnp.max(s, axis=1, keepdims=True))
        p = e * (any_visible / jnp.sum(e, axis=1, keepdims=True))
        p_sum += p
        ps.append(p.astype(BF16))
    oc_scr[...] = _dot(jnp.concatenate(ps, axis=0), vc_scr[...])

    wspan = tq + WIN_LEN
    wrows = pl.ds(pl.multiple_of(q0, tq), wspan)
    c = lax.broadcasted_iota(jnp.int32, (1, wspan), 1)
    win_ok = (c > rcol) & (c <= rcol + WIN_LEN) & (c + q0 >= WIN_LEN)
    wb = jnp.where(win_ok, 0.0, NEG)
    sw = _dot_nt(qx_scr[:, 0:HEAD_DIM], kwp_scr[wrows, :]) + jnp.concatenate([wb] * nh, axis=0)
    pw = jnp.exp2(sw - jnp.max(sw, axis=1, keepdims=True))
    ow_scr[...] = _dot(pw.astype(BF16), vwp_scr[wrows, :]) / jnp.sum(pw, axis=1, keepdims=True)

    nr = lax.broadcasted_iota(jnp.int32, (n_slc, n_cmp), 1) * CMP_STRIDE
    jr = lax.broadcasted_iota(jnp.int32, (n_slc, n_cmp), 0) * SLC_LEN
    overlap_t = jnp.where((nr < jr + SLC_LEN) & (nr + CMP_LEN > jr), 1.0, 0.0).astype(BF16)
    ps_hi = p_sum.astype(BF16)
    ps_lo = (p_sum - ps_hi.astype(F32)).astype(BF16)
    imp_t = _dot_nt(overlap_t, ps_hi) + _dot_nt(overlap_t, ps_lo)

    jblk = lax.broadcasted_iota(jnp.int32, (n_slc, tq), 0)
    qblk = (q0 + lax.broadcasted_iota(jnp.int32, (n_slc, tq), 1)) // SLC_LEN
    valid = jblk <= qblk
    forced = (jblk == 0) | (jblk == qblk) | (jblk == qblk - 1)
    score_scr[...] = jnp.where(valid, jnp.where(forced, 1e9, imp_t), -1e9)
    rank_scr[...] = jnp.zeros_like(rank_scr)
    ng = n_slc // SUBLANES
    g_last = ((q0 + tq - 1) // SLC_LEN) // SUBLANES
    row8 = lax.broadcasted_iota(jnp.int32, (SUBLANES, tq), 0)
    for gp in range(ng):
        @pl.when(gp <= g_last)
        def _():
            cnt = [jnp.zeros((SUBLANES, tq), F32) for _ in range(ng)]
            for jj in range(SUBLANES):
                jp = gp * SUBLANES + jj
                other = jnp.broadcast_to(score_scr[jp:jp + 1, :], (SUBLANES, tq))
                for g in range(ng):
                    sc = score_scr[g * SUBLANES:(g + 1) * SUBLANES, :]
                    if g > gp:
                        ahead = other >= sc
                    elif g < gp:
                        ahead = other > sc
                    else:
                        ahead = (other > sc) | ((other == sc) & (row8 > jj))
                    cnt[g] += jnp.where(ahead, 1.0, 0.0)
            for g in range(ng):
                rank_scr[g * SUBLANES:(g + 1) * SUBLANES, :] += cnt[g]

    sel_t = jnp.where(valid & (rank_scr[...] < topk), 1.0, 0.0).astype(BF16)
    sel_pad = jnp.concatenate([sel_t, jnp.zeros((LANES - n_slc, tq), BF16)], axis=0)
    ri = lax.broadcasted_iota(jnp.int32, (tq, tq), 0)
    ci = lax.broadcasted_iota(jnp.int32, (tq, tq), 1)
    eye = jnp.where(ri == ci, 1.0, 0.0).astype(BF16)
    sel = _dot_nt(eye, sel_pad)
    lane = lax.broadcasted_iota(jnp.int32, (tq, LANES), 1)
    sel_bias = jnp.where((lane < n_slc) & (sel < 0.5), NEG, 0.0).astype(BF16)

    for h in range(nh):
        qx_scr[h * tq:(h + 1) * tq, HEAD_DIM:2 * HEAD_DIM] = sel_bias
    m_scr[...] = jnp.full(m_scr.shape, NEG, F32)
    l_scr[...] = jnp.zeros_like(l_scr)
    acc_scr[...] = jnp.zeros_like(acc_scr)

    def key_rows(kt):
        return pl.ds(pl.multiple_of(kt * tk, tk), tk)

    def scores(kt):
        return _dot_nt(qx_scr[...], kx_scr[key_rows(kt), :])

    def softmax_update(s, kt):
        m_prev = m_scr[...]
        m_new = jnp.maximum(m_prev, jnp.max(s, axis=1, keepdims=True))
        p = jnp.exp2(s - pltpu.repeat(m_new, tk // LANES, axis=1))
        alpha = jnp.exp2(m_prev - m_new)
        l_scr[...] = alpha * l_scr[...] + jnp.sum(p, axis=1, keepdims=True)
        acc_scr[...] = alpha * acc_scr[...] + _dot(p.astype(BF16), vs_ref[key_rows(kt), :])
        m_scr[...] = m_new

    n_tiles = (q0 + tq + tk - 1) // tk
    s_scr[...] = scores(0)

    def slc_body(kt, carry):
        s = s_scr[...]
        s_next = scores(kt + 1)
        softmax_update(s, kt)
        s_scr[...] = s_next
        return carry

    lax.fori_loop(0, n_tiles - 1, slc_body, 0)
    kpos = (n_tiles - 1) * tk + lax.broadcasted_iota(jnp.int32, (1, tk), 1)
    causal_bias = jnp.where(kpos <= tpos, 0.0, NEG)
    softmax_update(s_scr[...] + jnp.concatenate([causal_bias] * nh, axis=0), n_tiles - 1)

    gates = _sigmoid(gate_ref[...])
    o_slc = acc_scr[...] / l_scr[...]
    for h in range(nh):
        hs = slice(h * HEAD_DIM, (h + 1) * HEAD_DIM)
        rows = slice(h * tq, (h + 1) * tq)
        o = (gates[:, 3 * h:3 * h + 1] * oc_scr[rows, :]
             + gates[:, 3 * h + 1:3 * h + 2] * o_slc[rows, :]
             + gates[:, 3 * h + 2:3 * h + 3] * ow_scr[rows, :])
        o_ref[:, hs] = o.astype(o_ref.dtype)


def _nsa(proj_a, proj_b, cos, sin_signed, w1k, w2k, w1v, w2v, pe_k, pe_v, batch, seq_len):
    n = proj_a.shape[0]
    tq = min(ATT_TILE, seq_len)
    nq = seq_len // tq
    gw = NSA_HPG * HEAD_DIM
    n_cmp, n_slc = seq_len // CMP_STRIDE, seq_len // SLC_LEN
    assert n_slc <= LANES // 2 and n_slc % SUBLANES == 0 and seq_len % min(SLC_KEY_TILE, seq_len) == 0

    def kv_spec(col0):
        return pl.BlockSpec((seq_len, HEAD_DIM), lambda b, g, i: (b, col0 // HEAD_DIM + g))

    def full(shape):
        return pl.BlockSpec(shape, lambda b, g, i: (0,) * len(shape))

    return pl.pallas_call(
        _nsa_kernel,
        grid=(batch, NSA_GROUPS, nq),
        in_specs=[
            pl.BlockSpec((tq, gw), lambda b, g, i: (b * nq + i, A_Q // gw + g)),
            pl.BlockSpec((tq, LANES), lambda b, g, i: (b * nq + i, B_GATE // LANES + g)),
            kv_spec(B_KC), kv_spec(B_VC),
            kv_spec(A_KS), kv_spec(A_VS), kv_spec(A_KW), kv_spec(A_VW),
            full((seq_len, HEAD_DIM)), full((seq_len, HEAD_DIM)),
            full((CMP_LEN * HEAD_DIM, HEAD_DIM)), full((HEAD_DIM, HEAD_DIM)),
            full((CMP_LEN * HEAD_DIM, HEAD_DIM)), full((HEAD_DIM, HEAD_DIM)),
            full((1, CMP_LEN * HEAD_DIM)), full((1, CMP_LEN * HEAD_DIM)),
        ],
        out_specs=pl.BlockSpec((tq, gw), lambda b, g, i: (b * nq + i, g)),
        out_shape=jax.ShapeDtypeStruct((n, NSA_WIDTH), BF16),
        scratch_shapes=[
            pltpu.VMEM((n_cmp, HEAD_DIM), BF16),
            pltpu.VMEM((n_cmp, HEAD_DIM), BF16),
            pltpu.VMEM((seq_len, 2 * HEAD_DIM), BF16),
            pltpu.VMEM((seq_len + WIN_LEN, HEAD_DIM), BF16),
            pltpu.VMEM((seq_len + WIN_LEN, HEAD_DIM), BF16),
            pltpu.VMEM((NSA_HPG * tq, 2 * HEAD_DIM), BF16),
            pltpu.VMEM((n_slc, tq), F32),
            pltpu.VMEM((n_slc, tq), F32),
            pltpu.VMEM((NSA_HPG * tq, LANES), F32),
            pltpu.VMEM((NSA_HPG * tq, LANES), F32),
            pltpu.VMEM((NSA_HPG * tq, HEAD_DIM), F32),
            pltpu.VMEM((NSA_HPG * tq, HEAD_DIM), F32),
            pltpu.VMEM((NSA_HPG * tq, HEAD_DIM), F32),
            pltpu.VMEM((NSA_HPG * tq, min(SLC_KEY_TILE, seq_len)), F32),
        ],
        compiler_params=_params(("parallel", "parallel", "arbitrary")),
        name="nsa",
    )(proj_a, proj_b, proj_b, proj_b, proj_a, proj_a, proj_a, proj_a,
      cos, sin_signed, w1k, w2k, w1v, w2v, pe_k, pe_v)


def _rope_tables(seq_len):
    inv = ROPE_THETA ** (-jnp.arange(0, HEAD_DIM, 2, dtype=F32) / HEAD_DIM)
    ang = jnp.arange(seq_len, dtype=F32)[:, None] * inv[None, :]
    ang = jnp.concatenate([ang, ang], -1)
    sign = jnp.concatenate([-jnp.ones((HEAD_DIM // 2,), F32), jnp.ones((HEAD_DIM // 2,), F32)])
    return jnp.cos(ang), jnp.sin(ang) * sign


def _split_w_in(w_in):
    offs = [0]
    for wd in IN_WIDTHS:
        offs.append(offs[-1] + wd)
    (q, kc, vc, ks, vs, kw, vw, gates, z, xbc, dt, pv) = [
        w_in[:, offs[i]:offs[i + 1]] for i in range(len(IN_WIDTHS))]
    d = w_in.shape[0]

    def pad_to(w, width):
        return jnp.pad(w, ((0, 0), (0, width - w.shape[1])))

    per_group = 3 * NSA_HPG
    w_a = jnp.concatenate([q, ks, vs, kw, vw], axis=1)
    w_b = jnp.concatenate(
        [xbc, z, pv, pad_to(dt, LANES),
         pad_to(gates[:, :per_group], LANES), pad_to(gates[:, per_group:], LANES),
         kc, vc, jnp.zeros((d, B_WIDTH - B_VC - KV_WIDTH), w_in.dtype)], axis=1)
    return w_a.astype(BF16), w_b.astype(BF16)


def _in_proj_weights(w_in):
    w_a, w_b = _split_w_in(w_in)
    w = jnp.concatenate([w_a, w_b], axis=1)
    d = w.shape[0]
    return w.reshape(d, -1, PROJ_COL_TILE).transpose(1, 0, 2)


def _pad_lanes(v):
    return jnp.pad(v, (0, LANES - v.shape[0]))[None, :]


def kernel(x, ffn1_norm_pre, ffn1_norm_post, ffn1_w_gate, ffn1_w_up, ffn1_w_down, mix_norm_pre, mix_norm_post, w_in, cmp_pe_k, cmp_pe_v, cmp_k_w1, cmp_k_w2, cmp_v_w1, cmp_v_w2, ssm_conv_w, ssm_conv_b, ssm_dt_bias, ssm_a_log, ssm_d, ssm_norm, pool_w, pool_scale, w_out, ffn2_norm_pre, ffn2_norm_post, ffn2_w_gate, ffn2_w_up, ffn2_w_down):
    batch, seq_len, d = x.shape
    depth = w_in.shape[0]
    cos, sin_signed = _rope_tables(seq_len)
    h = x.reshape(batch * seq_len, d)
    for i in range(depth):
        h = _ffn(h, ffn1_norm_pre[i][None], ffn1_norm_post[i][None],
                 ffn1_w_gate[i].astype(BF16), ffn1_w_up[i].astype(BF16), ffn1_w_down[i].astype(BF16))
        proj_a, proj_b = _in_proj(h, mix_norm_pre[i][None], _in_proj_weights(w_in[i]))
        o_nsa = _nsa(proj_a, proj_b, cos, sin_signed,
                     cmp_k_w1[i].astype(BF16), cmp_k_w2[i].astype(BF16),
                     cmp_v_w1[i].astype(BF16), cmp_v_w2[i].astype(BF16),
                     cmp_pe_k[i].reshape(1, -1), cmp_pe_v[i].reshape(1, -1), batch, seq_len)
        o_ssm = _ssd(proj_b, ssm_conv_w[i], ssm_conv_b[i][None], _pad_lanes(ssm_dt_bias[i]),
                     _pad_lanes(ssm_a_log[i]), jnp.repeat(ssm_d[i], SSM_HEAD_DIM)[None],
                     ssm_norm[i][None], batch, seq_len)
        o_pool = _pool(proj_b, pool_w[i].astype(BF16), pool_scale[i][None], seq_len)
        h = _out_proj(h, o_nsa, o_ssm, o_pool, w_out[i].astype(BF16), mix_norm_post[i][None])
        h = _ffn(h, ffn2_norm_pre[i][None], ffn2_norm_post[i][None],
                 ffn2_w_gate[i].astype(BF16), ffn2_w_up[i].astype(BF16), ffn2_w_down[i].astype(BF16))
    return h.reshape(batch, seq_len, d)
```

```python
import functools
import math

import jax
import jax.numpy as jnp
from jax import lax
from jax.experimental import pallas as pl
from jax.experimental.pallas import tpu as pltpu

F32 = jnp.float32
BF16 = jnp.bfloat16

D_MODEL = 2048
HEAD_DIM = 128
NSA_WIDTH = 1024
NSA_HEADS = 8
NSA_GROUPS = 2
NSA_HPG = NSA_HEADS // NSA_GROUPS
KV_WIDTH = NSA_GROUPS * HEAD_DIM
CMP_LEN = 32
CMP_STRIDE = 16
SLC_LEN = 64
SLC_TOPK = 16
WIN_LEN = 512
ROPE_THETA = 10000.0
SSM_WIDTH = 512
SSM_HEAD_DIM = 64
SSM_HEADS = 8
SSM_GROUPS = 2
SSM_STATE = 128
SSM_CONV = 4
SSM_CONV_DIM = SSM_WIDTH + 2 * SSM_GROUPS * SSM_STATE
POOL_WIDTH = 512
POOL_WINDOWS = (2, 4, 8, 16)
POOL_GROUP = POOL_WIDTH // len(POOL_WINDOWS)
D_FF = 5632
FFN_RESID = 0.5
RMS_EPS = 1e-6
NEG = -1e30
LOG2E = math.log2(math.e)
IN_WIDTHS = (NSA_WIDTH, KV_WIDTH, KV_WIDTH, KV_WIDTH, KV_WIDTH, KV_WIDTH, KV_WIDTH,
             3 * NSA_HEADS, SSM_WIDTH, SSM_CONV_DIM, SSM_HEADS, POOL_WIDTH)

LANES = 128
SUBLANES = 8
VMEM_LIMIT_BYTES = 60000 * 1024

A_Q, A_KS, A_VS, A_KW, A_VW = 0, 1024, 1280, 1536, 1792
A_WIDTH = 2048
B_XBC, B_Z, B_POOL, B_DT, B_GATE, B_KC, B_VC = 0, 1024, 1536, 2048, 2176, 2432, 2688
B_WIDTH = 3072

FFN_ROW_TILE = 1024
FFN_TILE = 512
CAST_STEPS = 8
ROW_TILE = 512
PROJ_COL_TILE = 1024
ATT_TILE = 256
SLC_KEY_TILE = 512
SSD_CHUNK = 128
POOL_TILE = 512
POOL_HALO = 16
CONV_HALO = 8


def _params(semantics):
    return pltpu.CompilerParams(dimension_semantics=semantics, vmem_limit_bytes=VMEM_LIMIT_BYTES)


def _sigmoid(x):
    return 1.0 / (1.0 + jnp.exp(-x))


def _silu(x):
    return x * _sigmoid(x)


def _rms(x):
    return x * lax.rsqrt(jnp.mean(x * x, axis=-1, keepdims=True) + RMS_EPS)


def _dot(a, b):
    return jnp.dot(a, b, preferred_element_type=F32)


def _dot_nt(a, b):
    return lax.dot_general(a, b, (((1,), (1,)), ((), ())), preferred_element_type=F32)


def _split3(x):
    hi = x.astype(BF16)
    r = x - hi.astype(F32)
    mid = r.astype(BF16)
    lo = (r - mid.astype(F32)).astype(BF16)
    return hi, mid, lo


def _dot_exact_rhs(x, sel):
    hi, mid, lo = _split3(x)
    return _dot(hi, sel) + _dot(mid, sel) + _dot(lo, sel)


def _ffn_kernel(x_ref, gpre_ref, gpost_ref, wg_ref, wu_ref, wd_ref, o_ref, xn_ref):
    j = pl.program_id(1)
    last = pl.num_programs(1) - 1

    def step(first, final):
        if first:
            xn = (_rms(x_ref[...]) * gpre_ref[...]).astype(BF16)
            xn_ref[...] = xn
        else:
            xn = xn_ref[...]
        g = _dot(xn, wg_ref[...])
        u = _dot(xn, wu_ref[...])
        a = (_silu(g) * u).astype(BF16)
        acc = _dot(a, wd_ref[...])
        if not first:
            acc += o_ref[...]
        if final:
            o_ref[...] = x_ref[...] + FFN_RESID * (_rms(acc) * gpost_ref[...])
        else:
            o_ref[...] = acc

    pl.when(j == 0)(functools.partial(step, True, False))
    pl.when((j > 0) & (j < last))(functools.partial(step, False, False))
    pl.when(j == last)(functools.partial(step, False, True))


def _ffn(x, gpre, gpost, wg, wu, wd):
    n, d = x.shape
    dff = wg.shape[1]
    tm, tf = min(FFN_ROW_TILE, n), FFN_TILE
    return pl.pallas_call(
        _ffn_kernel,
        grid=(n // tm, dff // tf),
        in_specs=[
            pl.BlockSpec((tm, d), lambda i, j: (i, 0), pipeline_mode=pl.Buffered(1)),
            pl.BlockSpec((1, d), lambda i, j: (0, 0)),
            pl.BlockSpec((1, d), lambda i, j: (0, 0)),
            pl.BlockSpec((d, tf), lambda i, j: (0, j)),
            pl.BlockSpec((d, tf), lambda i, j: (0, j)),
            pl.BlockSpec((tf, d), lambda i, j: (j, 0)),
        ],
        out_specs=pl.BlockSpec((tm, d), lambda i, j: (i, 0)),
        out_shape=jax.ShapeDtypeStruct((n, d), F32),
        scratch_shapes=[pltpu.VMEM((tm, d), BF16)],
        compiler_params=_params(("parallel", "arbitrary")),
        name="ffn",
    )(x, gpre, gpost, wg, wu, wd)


def _cast3_kernel(a_ref, b_ref, c_ref, oa_ref, ob_ref, oc_ref):
    oa_ref[...] = a_ref[...].astype(oa_ref.dtype)
    ob_ref[...] = b_ref[...].astype(ob_ref.dtype)
    oc_ref[...] = c_ref[...].astype(oc_ref.dtype)


def _ffn_weights_bf16(wg, wu, wd, layer):
    steps = CAST_STEPS

    def spec_in(w):
        return pl.BlockSpec((None, w.shape[1] // steps, w.shape[2]), lambda r: (layer, r, 0))

    def spec_out(w):
        return pl.BlockSpec((w.shape[1] // steps, w.shape[2]), lambda r: (r, 0))

    ws = (wg, wu, wd)
    return pl.pallas_call(
        _cast3_kernel,
        grid=(steps,),
        in_specs=[spec_in(w) for w in ws],
        out_specs=[spec_out(w) for w in ws],
        out_shape=[jax.ShapeDtypeStruct(w.shape[1:], BF16) for w in ws],
        compiler_params=_params(("parallel",)),
        name="ffn_weights_bf16",
    )(*ws)


def _in_proj_kernel(x_ref, g_ref, w_ref, oa_ref, ob_ref, xn_ref, *, na):
    j = pl.program_id(1)

    @pl.when(j == 0)
    def _():
        xn_ref[...] = (_rms(x_ref[...]) * g_ref[...]).astype(BF16)

    @pl.when(j < na)
    def _():
        oa_ref[...] = _dot(xn_ref[...], w_ref[j]).astype(oa_ref.dtype)

    @pl.when(j >= na)
    def _():
        ob_ref[...] = _dot(xn_ref[...], w_ref[j])


def _in_proj(x, g, w):
    n, d = x.shape
    nt, _, tn = w.shape
    na = A_WIDTH // tn
    tm = min(ROW_TILE, n)
    return pl.pallas_call(
        functools.partial(_in_proj_kernel, na=na),
        grid=(n // tm, nt),
        in_specs=[
            pl.BlockSpec((tm, d), lambda i, j: (i, 0)),
            pl.BlockSpec((1, d), lambda i, j: (0, 0)),
            pl.BlockSpec((nt, d, tn), lambda i, j: (0, 0, 0), pipeline_mode=pl.Buffered(1)),
        ],
        out_specs=[
            pl.BlockSpec((tm, tn), lambda i, j: (i, jnp.minimum(j, na - 1))),
            pl.BlockSpec((tm, tn), lambda i, j: (i, jnp.maximum(j - na, 0))),
        ],
        out_shape=[jax.ShapeDtypeStruct((n, A_WIDTH), BF16),
                   jax.ShapeDtypeStruct((n, B_WIDTH), F32)],
        scratch_shapes=[pltpu.VMEM((tm, d), BF16)],
        compiler_params=_params(("parallel", "arbitrary")),
        name="in_proj",
    )(x, g, w)


def _out_proj_kernel(h_ref, a_ref, s_ref, p_ref, w_ref, g_ref, o_ref):
    m = _dot(a_ref[...], w_ref[0:NSA_WIDTH, :])
    m += _dot(s_ref[...], w_ref[NSA_WIDTH:NSA_WIDTH + SSM_WIDTH, :])
    m += _dot(p_ref[...], w_ref[NSA_WIDTH + SSM_WIDTH:, :])
    o_ref[...] = h_ref[...] + _rms(m) * g_ref[...]


def _out_proj(h, o_nsa, o_ssm, o_pool, w, g):
    n, d = h.shape
    tm = min(ROW_TILE, n)
    return pl.pallas_call(
        _out_proj_kernel,
        grid=(n // tm,),
        in_specs=[
            pl.BlockSpec((tm, d), lambda i: (i, 0)),
            pl.BlockSpec((tm, NSA_WIDTH), lambda i: (i, 0)),
            pl.BlockSpec((tm, SSM_WIDTH), lambda i: (i, 0)),
            pl.BlockSpec((tm, POOL_WIDTH), lambda i: (i, 0)),
            pl.BlockSpec((d, d), lambda i: (0, 0)),
            pl.BlockSpec((1, d), lambda i: (0, 0)),
        ],
        out_specs=pl.BlockSpec((tm, d), lambda i: (i, 0)),
        out_shape=jax.ShapeDtypeStruct((n, d), F32),
        compiler_params=_params(("parallel",)),
        name="out_proj",
    )(h, o_nsa, o_ssm, o_pool, w, g)


def _pool_kernel(halo_ref, x_ref, w_ref, scale_ref, o_ref, *, tiles_per_seq):
    tile = x_ref.shape[0]
    it = pl.program_id(0) % tiles_per_seq
    halo = jnp.where(it == 0, 0.0, halo_ref[...])
    ext = jnp.concatenate([halo, x_ref[...]], axis=0)
    pos = it * tile + lax.broadcasted_iota(jnp.int32, (tile, POOL_GROUP), 0)
    for gi, w in enumerate(POOL_WINDOWS):
        xg = ext[:, gi * POOL_GROUP:(gi + 1) * POOL_GROUP]
        s, k = xg, 1
        while k < w:
            s = s + pltpu.roll(s, k, axis=0)
            k *= 2
        cnt = jnp.minimum(pos + 1, w).astype(F32)
        dlt = s[POOL_HALO:] / cnt - xg[POOL_HALO:]
        y = _dot(dlt.astype(BF16), w_ref[gi])
        sl = slice(gi * POOL_GROUP, (gi + 1) * POOL_GROUP)
        o_ref[:, sl] = (y * scale_ref[:, sl]).astype(o_ref.dtype)


def _pool(proj_b, pool_w, pool_scale, seq_len):
    n = proj_b.shape[0]
    tile = min(POOL_TILE, seq_len)
    tiles_per_seq = seq_len // tile
    ratio = tile // POOL_HALO
    col = B_POOL // POOL_WIDTH
    return pl.pallas_call(
        functools.partial(_pool_kernel, tiles_per_seq=tiles_per_seq),
        grid=(n // tile,),
        in_specs=[
            pl.BlockSpec((POOL_HALO, POOL_WIDTH), lambda i: (jnp.maximum(i * ratio - 1, 0), col)),
            pl.BlockSpec((tile, POOL_WIDTH), lambda i: (i, col)),
            pl.BlockSpec((len(POOL_WINDOWS), POOL_GROUP, POOL_GROUP), lambda i: (0, 0, 0)),
            pl.BlockSpec((1, POOL_WIDTH), lambda i: (0, 0)),
        ],
        out_specs=pl.BlockSpec((tile, POOL_WIDTH), lambda i: (i, 0)),
        out_shape=jax.ShapeDtypeStruct((n, POOL_WIDTH), BF16),
        compiler_params=_params(("parallel",)),
        name="pool",
    )(proj_b, proj_b, pool_w, pool_scale)


def _ssd_kernel(halo_ref, xbc_ref, z_ref, dt_ref, convw_ref, convb_ref, dtb_ref, alog_ref,
                dskip_ref, normw_ref, o_ref, state_ref):
    L = xbc_ref.shape[0]
    c = pl.program_id(1)

    @pl.when(c == 0)
    def _():
        state_ref[...] = jnp.zeros_like(state_ref)

    halo = jnp.where(c == 0, 0.0, halo_ref[...])
    ext = jnp.concatenate([halo, xbc_ref[...]], axis=0)
    conv = convb_ref[...] + convw_ref[SSM_CONV - 1:SSM_CONV, :] * ext[CONV_HALO:]
    for k in range(1, SSM_CONV):
        conv += convw_ref[SSM_CONV - 1 - k:SSM_CONV - k, :] * pltpu.roll(ext, k, axis=0)[CONV_HALO:]
    act = _silu(conv)
    xs = act[:, :SSM_WIDTH]
    bm = act[:, SSM_WIDTH:SSM_WIDTH + SSM_GROUPS * SSM_STATE]
    cm = act[:, SSM_WIDTH + SSM_GROUPS * SSM_STATE:]

    lane = lax.broadcasted_iota(jnp.int32, (1, LANES), 1)
    xdt = dt_ref[...] + dtb_ref[...]
    dt = jnp.maximum(xdt, 0.0) + jnp.log1p(jnp.exp(-jnp.abs(xdt)))
    a_head = jnp.where(lane < SSM_HEADS, -jnp.exp(alog_ref[...]), 0.0)
    cs = a_head * dt
    row = lax.broadcasted_iota(jnp.int32, (L, LANES), 0)
    k = 1
    while k < L:
        cs = cs + jnp.where(row >= k, pltpu.roll(cs, k, axis=0), 0.0)
        k *= 2
    cs_t = cs.T

    er = lax.broadcasted_iota(jnp.int32, (LANES, SSM_WIDTH), 0)
    ec = lax.broadcasted_iota(jnp.int32, (LANES, SSM_WIDTH), 1)
    expand = jnp.where(er == ec // SSM_HEAD_DIM, 1.0, 0.0).astype(BF16)
    dtx = _dot_exact_rhs(dt, expand)
    csx = _dot_exact_rhs(cs, expand)
    cs_last = csx[L - 1:L, :]

    x_dt = xs * dtx
    x_end = (x_dt * jnp.exp(cs_last - csx)).astype(BF16)
    x_dt16 = x_dt.astype(BF16)
    ecs = jnp.exp(csx)
    li = lax.broadcasted_iota(jnp.int32, (L, L), 0)
    si = lax.broadcasted_iota(jnp.int32, (L, L), 1)
    gw = SSM_WIDTH // SSM_GROUPS
    hpg = SSM_HEADS // SSM_GROUPS
    lane_g = lax.broadcasted_iota(jnp.int32, (1, gw), 1)
    ys = []
    for g in range(SSM_GROUPS):
        bg = bm[:, g * SSM_STATE:(g + 1) * SSM_STATE]
        cg = cm[:, g * SSM_STATE:(g + 1) * SSM_STATE].astype(BF16)
        gs = slice(g * gw, (g + 1) * gw)
        cb = _dot_nt(cg, bg.astype(BF16))
        y_g = jnp.zeros((L, gw), F32)
        for hh in range(hpg):
            h = g * hpg + hh
            seg = cs[:, h:h + 1] - cs_t[h:h + 1, :]
            m_h = (cb * jnp.exp(jnp.where(li >= si, seg, NEG))).astype(BF16)
            x_h = jnp.where(lane_g // SSM_HEAD_DIM == hh, x_dt16[:, gs], jnp.zeros((), BF16))
            y_g += _dot(m_h, x_h)
        st = state_ref[g]
        y_g += _dot(cg, st.astype(BF16)) * ecs[:, gs]
        state_ref[g] = jnp.exp(cs_last[:, gs]) * st + _dot(bg.T.astype(BF16), x_end[:, gs])
        ys.append(y_g)
    y = jnp.concatenate(ys, axis=1) + dskip_ref[...] * xs
    y = y * _silu(z_ref[...])
    outs = []
    for g in range(SSM_GROUPS):
        outs.append(_rms(y[:, g * gw:(g + 1) * gw]))
    o_ref[...] = (jnp.concatenate(outs, axis=1) * normw_ref[...]).astype(o_ref.dtype)


def _ssd(proj_b, conv_w, conv_b, dt_bias, a_log, d_skip, norm_w, batch, seq_len):
    n = proj_b.shape[0]
    L = SSD_CHUNK
    nch = seq_len // L
    ratio = L // CONV_HALO
    return pl.pallas_call(
        _ssd_kernel,
        grid=(batch, nch),
        in_specs=[
            pl.BlockSpec((CONV_HALO, SSM_CONV_DIM),
                         lambda b, c: (jnp.maximum((b * nch + c) * ratio - 1, 0), B_XBC // SSM_CONV_DIM)),
            pl.BlockSpec((L, SSM_CONV_DIM), lambda b, c: (b * nch + c, B_XBC // SSM_CONV_DIM)),
            pl.BlockSpec((L, SSM_WIDTH), lambda b, c: (b * nch + c, B_Z // SSM_WIDTH)),
            pl.BlockSpec((L, LANES), lambda b, c: (b * nch + c, B_DT // LANES)),
            pl.BlockSpec((SSM_CONV, SSM_CONV_DIM), lambda b, c: (0, 0)),
            pl.BlockSpec((1, SSM_CONV_DIM), lambda b, c: (0, 0)),
            pl.BlockSpec((1, LANES), lambda b, c: (0, 0)),
            pl.BlockSpec((1, LANES), lambda b, c: (0, 0)),
            pl.BlockSpec((1, SSM_WIDTH), lambda b, c: (0, 0)),
            pl.BlockSpec((1, SSM_WIDTH), lambda b, c: (0, 0)),
        ],
        out_specs=pl.BlockSpec((L, SSM_WIDTH), lambda b, c: (b * nch + c, 0)),
        out_shape=jax.ShapeDtypeStruct((n, SSM_WIDTH), BF16),
        scratch_shapes=[pltpu.VMEM((SSM_GROUPS, SSM_STATE, SSM_WIDTH // SSM_GROUPS), F32)],
        compiler_params=_params(("parallel", "arbitrary")),
        name="ssd",
    )(proj_b, proj_b, proj_b, proj_b, conv_w, conv_b, dt_bias, a_log, d_skip, norm_w)


def _rope(x, cos, sin_signed):
    return x * cos + pltpu.roll(x, HEAD_DIM // 2, axis=1) * sin_signed


def _compress(src_ref, w1_ref, w2_ref, pe_ref):
    n_cmp = src_ref.shape[0] // CMP_STRIDE
    xs = [src_ref[pl.ds(p, n_cmp, stride=CMP_STRIDE), :] for p in range(CMP_STRIDE)]
    x = jnp.concatenate(xs, axis=1).astype(BF16)
    half = CMP_STRIDE * HEAD_DIM
    first = _dot(x, w1_ref[0:half, :])
    second = _dot(x, w1_ref[half:2 * half, :])
    pe = jnp.broadcast_to(pe_ref[...], (SUBLANES, 2 * half)).astype(BF16)
    pe_term = _dot(pe, w1_ref[...])[0:1, :]
    pre = first + pltpu.roll(second, n_cmp - 1, axis=0) + pe_term
    return _dot(_silu(pre).astype(BF16), w2_ref[...])


def _nsa_kernel(q_ref, gate_ref, kc_src_ref, vc_src_ref, ks_ref, vs_ref, kw_ref, vw_ref,
                cos_ref, sin_ref, w1k_ref, w2k_ref, w1v_ref, w2v_ref, pek_ref, pev_ref,
                o_ref,
                kc_scr, vct_scr, kx_scr, kwp_scr, vst_scr, vwt_scr, qx_scr, score_scr, rank_scr,
                m_scr, l_scr, acc_scr, oct_scr, owt_scr, s_scr):
    tq = q_ref.shape[0]
    seq = ks_ref.shape[0]
    tk = min(SLC_KEY_TILE, seq)
    nh = NSA_HPG
    n_cmp, n_slc = seq // CMP_STRIDE, seq // SLC_LEN
    topk = min(SLC_TOPK, n_slc)
    qi = pl.program_id(2)
    q0 = qi * tq
    qk_scale = HEAD_DIM ** -0.5 * LOG2E

    @pl.when(qi == 0)
    def _():
        kc_scr[...] = _compress(kc_src_ref, w1k_ref, w2k_ref, pek_ref).astype(BF16)
        vct_scr[...] = _compress(vc_src_ref, w1v_ref, w2v_ref, pev_ref).T.astype(BF16)
        kwp_scr[0:WIN_LEN, :] = jnp.zeros((WIN_LEN, HEAD_DIM), BF16)
        for j in range(WIN_LEN // tq):
            vwt_scr[j] = jnp.zeros((HEAD_DIM, tq), BF16)

        def prep_rows(i, carry):
            rows = pl.ds(pl.multiple_of(i * tq, tq), tq)
            prows = pl.ds(pl.multiple_of(WIN_LEN + i * tq, tq), tq)
            cos, sin = cos_ref[rows, :], sin_ref[rows, :]
            kx_scr[rows, 0:HEAD_DIM] = _rope(ks_ref[rows, :].astype(F32), cos, sin).astype(BF16)
            blk = (i * tq + lax.broadcasted_iota(jnp.int32, (tq, LANES), 0)) // SLC_LEN
            lane = lax.broadcasted_iota(jnp.int32, (tq, LANES), 1)
            kx_scr[rows, HEAD_DIM:2 * HEAD_DIM] = jnp.where(lane == blk, 1.0, 0.0).astype(BF16)
            kwp_scr[prows, :] = _rope(kw_ref[rows, :].astype(F32), cos, sin).astype(BF16)
            vst_scr[i] = vs_ref[rows, :].astype(F32).T.astype(BF16)
            vwt_scr[i + WIN_LEN // tq] = vw_ref[rows, :].astype(F32).T.astype(BF16)
            return carry

        lax.fori_loop(0, seq // tq, prep_rows, 0)

    qrows = pl.ds(pl.multiple_of(q0, tq), tq)
    qlane = lax.broadcasted_iota(jnp.int32, (1, tq), 1)
    tpos = q0 + qlane

    def q_rows(h):
        return slice(h * tq, (h + 1) * tq)

    cos_q, sin_q = cos_ref[qrows, :], sin_ref[qrows, :]
    for h in range(nh):
        hs = slice(h * HEAD_DIM, (h + 1) * HEAD_DIM)
        qx_scr[q_rows(h), 0:HEAD_DIM] = (_rope(q_ref[:, hs].astype(F32), cos_q, sin_q) * qk_scale).astype(BF16)

    def heads(x):
        return jnp.concatenate([x] * nh, axis=1)

    q_stack = jnp.concatenate([q_ref[:, h * HEAD_DIM:(h + 1) * HEAD_DIM] for h in range(nh)], axis=0)
    nblk = lax.broadcasted_iota(jnp.int32, (n_cmp, 1), 0)
    cmp_bias = jnp.where((nblk * CMP_STRIDE + (CMP_LEN - 1)) <= tpos, 0.0, NEG)
    any_visible = jnp.where(tpos >= CMP_LEN - 1, 1.0, 0.0)
    s = _dot_nt(kc_scr[...], q_stack) * qk_scale + heads(cmp_bias)
    e = jnp.exp2(s - jnp.max(s, axis=0, keepdims=True))
    p = e * (heads(any_visible) / jnp.sum(e, axis=0, keepdims=True))
    p_sum = p[:, 0:tq]
    for h in range(1, nh):
        p_sum += p[:, h * tq:(h + 1) * tq]
    oct_scr[...] = _dot(vct_scr[...], p.astype(BF16))

    wspan = tq + WIN_LEN
    wrows = pl.ds(pl.multiple_of(q0, tq), wspan)
    c = lax.broadcasted_iota(jnp.int32, (wspan, 1), 0)
    win_ok = (c > qlane) & (c <= qlane + WIN_LEN) & (c + q0 >= WIN_LEN)
    wb = jnp.where(win_ok, 0.0, NEG)
    s = _dot_nt(kwp_scr[wrows, :], qx_scr[:, 0:HEAD_DIM]) + heads(wb)
    p = jnp.exp2(s - jnp.max(s, axis=0, keepdims=True))
    pb = p.astype(BF16)
    o = _dot(vwt_scr[qi], pb[0:tq])
    for j in range(1, wspan // tq):
        o += _dot(vwt_scr[qi + j], pb[j * tq:(j + 1) * tq])
    owt_scr[...] = o / jnp.sum(p, axis=0, keepdims=True)

    nr = lax.broadcasted_iota(jnp.int32, (n_slc, n_cmp), 1) * CMP_STRIDE
    jr = lax.broadcasted_iota(jnp.int32, (n_slc, n_cmp), 0) * SLC_LEN
    overlap_t = jnp.where((nr < jr + SLC_LEN) & (nr + CMP_LEN > jr), 1.0, 0.0).astype(BF16)
    ps_hi = p_sum.astype(BF16)
    ps_lo = (p_sum - ps_hi.astype(F32)).astype(BF16)
    imp_t = _dot(overlap_t, ps_hi) + _dot(overlap_t, ps_lo)

    jblk = lax.broadcasted_iota(jnp.int32, (n_slc, tq), 0)
    qblk = (q0 + lax.broadcasted_iota(jnp.int32, (n_slc, tq), 1)) // SLC_LEN
    valid = jblk <= qblk
    forced = (jblk == 0) | (jblk == qblk) | (jblk == qblk - 1)
    score_scr[...] = jnp.where(valid, jnp.where(forced, 1e9, imp_t), -1e9)
    rank_scr[...] = jnp.zeros_like(rank_scr)
    ng = n_slc // SUBLANES
    g_last = ((q0 + tq - 1) // SLC_LEN) // SUBLANES
    row8 = lax.broadcasted_iota(jnp.int32, (SUBLANES, tq), 0)
    for gp in range(ng):
        @pl.when(gp <= g_last)
        def _():
            cnt = [jnp.zeros((SUBLANES, tq), F32) for _ in range(ng)]
            for jj in range(SUBLANES):
                jp = gp * SUBLANES + jj
                other = jnp.broadcast_to(score_scr[jp:jp + 1, :], (SUBLANES, tq))
                for g in range(ng):
                    sc = score_scr[g * SUBLANES:(g + 1) * SUBLANES, :]
                    if g > gp:
                        ahead = other >= sc
                    elif g < gp:
                        ahead = other > sc
                    else:
                        ahead = (other > sc) | ((other == sc) & (row8 > jj))
                    cnt[g] += jnp.where(ahead, 1.0, 0.0)
            for g in range(ng):
                rank_scr[g * SUBLANES:(g + 1) * SUBLANES, :] += cnt[g]

    sel_t = jnp.where(valid & (rank_scr[...] < topk), 1.0, 0.0).astype(BF16)
    sel_pad = jnp.concatenate([sel_t, jnp.zeros((LANES - n_slc, tq), BF16)], axis=0)
    ri = lax.broadcasted_iota(jnp.int32, (tq, tq), 0)
    ci = lax.broadcasted_iota(jnp.int32, (tq, tq), 1)
    eye = jnp.where(ri == ci, 1.0, 0.0).astype(BF16)
    sel = _dot_nt(eye, sel_pad)
    lane = lax.broadcasted_iota(jnp.int32, (tq, LANES), 1)
    sel_bias = jnp.where((lane < n_slc) & (sel < 0.5), NEG, 0.0).astype(BF16)

    for h in range(nh):
        qx_scr[q_rows(h), HEAD_DIM:2 * HEAD_DIM] = sel_bias
    m_scr[...] = jnp.full(m_scr.shape, NEG, F32)
    l_scr[...] = jnp.zeros_like(l_scr)
    acc_scr[...] = jnp.zeros_like(acc_scr)
    chunks = tk // tq

    def scores(kt):
        return _dot_nt(kx_scr[pl.ds(pl.multiple_of(kt * tk, tk), tk), :], qx_scr[...])

    def softmax_update(s, kt):
        m_prev = m_scr[...]
        m_new = jnp.maximum(m_prev, jnp.max(s, axis=0, keepdims=True))
        p = jnp.exp2(s - m_new)
        alpha = jnp.exp2(m_prev - m_new)
        l_scr[...] = alpha * l_scr[...] + jnp.sum(p, axis=0, keepdims=True)
        pb = p.astype(BF16)
        pv = _dot(vst_scr[kt * chunks], pb[0:tq])
        for j in range(1, chunks):
            pv += _dot(vst_scr[kt * chunks + j], pb[j * tq:(j + 1) * tq])
        acc_scr[...] = alpha * acc_scr[...] + pv
        m_scr[...] = m_new

    n_tiles = (q0 + tq + tk - 1) // tk
    s_scr[...] = scores(0)

    def slc_body(kt, carry):
        s = s_scr[...]
        s_next = scores(kt + 1)
        softmax_update(s, kt)
        s_scr[...] = s_next
        return carry

    lax.fori_loop(0, n_tiles - 1, slc_body, 0)
    kpos = (n_tiles - 1) * tk + lax.broadcasted_iota(jnp.int32, (tk, 1), 0)
    softmax_update(s_scr[...] + heads(jnp.where(kpos <= tpos, 0.0, NEG)), n_tiles - 1)

    gates = _sigmoid(gate_ref[...]).T
    for h in range(nh):
        hs = slice(h * HEAD_DIM, (h + 1) * HEAD_DIM)
        ql = slice(h * tq, (h + 1) * tq)
        o = (gates[3 * h:3 * h + 1] * oct_scr[:, ql]
             + gates[3 * h + 1:3 * h + 2] * (acc_scr[:, ql] / l_scr[:, ql])
             + gates[3 * h + 2:3 * h + 3] * owt_scr[:, ql])
        o_ref[:, hs] = o.T.astype(o_ref.dtype)


def _nsa(proj_a, proj_b, cos, sin_signed, w1k, w2k, w1v, w2v, pe_k, pe_v, batch, seq_len):
    n = proj_a.shape[0]
    tq = min(ATT_TILE, seq_len)
    nq = seq_len // tq
    gw = NSA_HPG * HEAD_DIM
    n_cmp, n_slc = seq_len // CMP_STRIDE, seq_len // SLC_LEN
    assert n_slc <= LANES // 2 and n_slc % SUBLANES == 0 and seq_len % min(SLC_KEY_TILE, seq_len) == 0

    def kv_spec(col0):
        return pl.BlockSpec((seq_len, HEAD_DIM), lambda b, g, i: (b, col0 // HEAD_DIM + g))

    def full(shape):
        return pl.BlockSpec(shape, lambda b, g, i: (0,) * len(shape))

    return pl.pallas_call(
        _nsa_kernel,
        grid=(batch, NSA_GROUPS, nq),
        in_specs=[
            pl.BlockSpec((tq, gw), lambda b, g, i: (b * nq + i, A_Q // gw + g)),
            pl.BlockSpec((tq, LANES), lambda b, g, i: (b * nq + i, B_GATE // LANES + g)),
            kv_spec(B_KC), kv_spec(B_VC),
            kv_spec(A_KS), kv_spec(A_VS), kv_spec(A_KW), kv_spec(A_VW),
            full((seq_len, HEAD_DIM)), full((seq_len, HEAD_DIM)),
            full((CMP_LEN * HEAD_DIM, HEAD_DIM)), full((HEAD_DIM, HEAD_DIM)),
            full((CMP_LEN * HEAD_DIM, HEAD_DIM)), full((HEAD_DIM, HEAD_DIM)),
            full((1, CMP_LEN * HEAD_DIM)), full((1, CMP_LEN * HEAD_DIM)),
        ],
        out_specs=pl.BlockSpec((tq, gw), lambda b, g, i: (b * nq + i, g)),
        out_shape=jax.ShapeDtypeStruct((n, NSA_WIDTH), BF16),
        scratch_shapes=[
            pltpu.VMEM((n_cmp, HEAD_DIM), BF16),
            pltpu.VMEM((HEAD_DIM, n_cmp), BF16),
            pltpu.VMEM((seq_len, 2 * HEAD_DIM), BF16),
            pltpu.VMEM((seq_len + WIN_LEN, HEAD_DIM), BF16),
            pltpu.VMEM((seq_len // tq, HEAD_DIM, tq), BF16),
            pltpu.VMEM(((seq_len + WIN_LEN) // tq, HEAD_DIM, tq), BF16),
            pltpu.VMEM((NSA_HPG * tq, 2 * HEAD_DIM), BF16),
            pltpu.VMEM((n_slc, tq), F32),
            pltpu.VMEM((n_slc, tq), F32),
            pltpu.VMEM((1, NSA_HPG * tq), F32),
            pltpu.VMEM((1, NSA_HPG * tq), F32),
            pltpu.VMEM((HEAD_DIM, NSA_HPG * tq), F32),
            pltpu.VMEM((HEAD_DIM, NSA_HPG * tq), F32),
            pltpu.VMEM((HEAD_DIM, NSA_HPG * tq), F32),
            pltpu.VMEM((min(SLC_KEY_TILE, seq_len), NSA_HPG * tq), F32),
        ],
        compiler_params=_params(("parallel", "parallel", "arbitrary")),
        name="nsa",
    )(proj_a, proj_b, proj_b, proj_b, proj_a, proj_a, proj_a, proj_a,
      cos, sin_signed, w1k, w2k, w1v, w2v, pe_k, pe_v)


def _rope_tables(seq_len):
    inv = ROPE_THETA ** (-jnp.arange(0, HEAD_DIM, 2, dtype=F32) / HEAD_DIM)
    ang = jnp.arange(seq_len, dtype=F32)[:, None] * inv[None, :]
    ang = jnp.concatenate([ang, ang], -1)
    sign = jnp.concatenate([-jnp.ones((HEAD_DIM // 2,), F32), jnp.ones((HEAD_DIM // 2,), F32)])
    return jnp.cos(ang), jnp.sin(ang) * sign


def _split_w_in(w_in):
    offs = [0]
    for wd in IN_WIDTHS:
        offs.append(offs[-1] + wd)
    (q, kc, vc, ks, vs, kw, vw, gates, z, xbc, dt, pv) = [
        w_in[:, offs[i]:offs[i + 1]] for i in range(len(IN_WIDTHS))]
    d = w_in.shape[0]

    def pad_to(w, width):
        return jnp.pad(w, ((0, 0), (0, width - w.shape[1])))

    per_group = 3 * NSA_HPG
    w_a = jnp.concatenate([q, ks, vs, kw, vw], axis=1)
    w_b = jnp.concatenate(
        [xbc, z, pv, pad_to(dt, LANES),
         pad_to(gates[:, :per_group], LANES), pad_to(gates[:, per_group:], LANES),
         kc, vc, jnp.zeros((d, B_WIDTH - B_VC - KV_WIDTH), w_in.dtype)], axis=1)
    return w_a.astype(BF16), w_b.astype(BF16)


def _in_proj_weights(w_in):
    w_a, w_b = _split_w_in(w_in)
    w = jnp.concatenate([w_a, w_b], axis=1)
    d = w.shape[0]
    return w.reshape(d, -1, PROJ_COL_TILE).transpose(1, 0, 2)


def _pad_lanes(v):
    return jnp.pad(v, (0, LANES - v.shape[0]))[None, :]


def kernel(x, ffn1_norm_pre, ffn1_norm_post, ffn1_w_gate, ffn1_w_up, ffn1_w_down, mix_norm_pre, mix_norm_post, w_in, cmp_pe_k, cmp_pe_v, cmp_k_w1, cmp_k_w2, cmp_v_w1, cmp_v_w2, ssm_conv_w, ssm_conv_b, ssm_dt_bias, ssm_a_log, ssm_d, ssm_norm, pool_w, pool_scale, w_out, ffn2_norm_pre, ffn2_norm_post, ffn2_w_gate, ffn2_w_up, ffn2_w_down):
    batch, seq_len, d = x.shape
    depth = w_in.shape[0]
    cos, sin_signed = _rope_tables(seq_len)
    h = x.reshape(batch * seq_len, d)
    for i in range(depth):
        h = _ffn(h, ffn1_norm_pre[i][None], ffn1_norm_post[i][None],
                 *_ffn_weights_bf16(ffn1_w_gate, ffn1_w_up, ffn1_w_down, i))
        proj_a, proj_b = _in_proj(h, mix_norm_pre[i][None], _in_proj_weights(w_in[i]))
        o_nsa = _nsa(proj_a, proj_b, cos, sin_signed,
                     cmp_k_w1[i].astype(BF16), cmp_k_w2[i].astype(BF16),
                     cmp_v_w1[i].astype(BF16), cmp_v_w2[i].astype(BF16),
                     cmp_pe_k[i].reshape(1, -1), cmp_pe_v[i].reshape(1, -1), batch, seq_len)
        o_ssm = _ssd(proj_b, ssm_conv_w[i], ssm_conv_b[i][None], _pad_lanes(ssm_dt_bias[i]),
                     _pad_lanes(ssm_a_log[i]), jnp.repeat(ssm_d[i], SSM_HEAD_DIM)[None],
                     ssm_norm[i][None], batch, seq_len)
        o_pool = _pool(proj_b, pool_w[i].astype(BF16), pool_scale[i][None], seq_len)
        h = _out_proj(h, o_nsa, o_ssm, o_pool, w_out[i].astype(BF16), mix_norm_post[i][None])
        h = _ffn(h, ffn2_norm_pre[i][None], ffn2_norm_post[i][None],
                 *_ffn_weights_bf16(ffn2_w_gate, ffn2_w_up, ffn2_w_down, i))
    return h.reshape(batch, seq_len, d)
```

```python
import functools
import math

import jax
import jax.numpy as jnp
from jax import lax
from jax.experimental import pallas as pl
from jax.experimental.pallas import tpu as pltpu

F32 = jnp.float32
BF16 = jnp.bfloat16

D_MODEL = 2048
HEAD_DIM = 128
NSA_WIDTH = 1024
NSA_HEADS = 8
NSA_GROUPS = 2
NSA_HPG = NSA_HEADS // NSA_GROUPS
KV_WIDTH = NSA_GROUPS * HEAD_DIM
CMP_LEN = 32
CMP_STRIDE = 16
SLC_LEN = 64
SLC_TOPK = 16
WIN_LEN = 512
ROPE_THETA = 10000.0
SSM_WIDTH = 512
SSM_HEAD_DIM = 64
SSM_HEADS = 8
SSM_GROUPS = 2
SSM_STATE = 128
SSM_CONV = 4
SSM_CONV_DIM = SSM_WIDTH + 2 * SSM_GROUPS * SSM_STATE
POOL_WIDTH = 512
POOL_WINDOWS = (2, 4, 8, 16)
POOL_GROUP = POOL_WIDTH // len(POOL_WINDOWS)
D_FF = 5632
FFN_RESID = 0.5
RMS_EPS = 1e-6
NEG = -1e30
LOG2E = math.log2(math.e)
IN_WIDTHS = (NSA_WIDTH, KV_WIDTH, KV_WIDTH, KV_WIDTH, KV_WIDTH, KV_WIDTH, KV_WIDTH,
             3 * NSA_HEADS, SSM_WIDTH, SSM_CONV_DIM, SSM_HEADS, POOL_WIDTH)

LANES = 128
SUBLANES = 8
VMEM_LIMIT_BYTES = 60000 * 1024

A_Q, A_KS, A_VS, A_KW, A_VW = 0, 1024, 1280, 1536, 1792
A_WIDTH = 2048
B_XBC, B_Z, B_POOL, B_DT, B_GATE, B_KC, B_VC = 0, 1024, 1536, 2048, 2176, 2432, 2688
B_WIDTH = 3072

FFN_ROW_TILE = 512
FFN_TILE = 512
CAST_STEPS = 8
ROW_TILE = 512
PROJ_COL_TILE = 1024
ATT_TILE = 256
SLC_KEY_TILE = 512
SSD_CHUNK = 128
POOL_TILE = 512
POOL_HALO = 16
CONV_HALO = 8


def _params(semantics):
    return pltpu.CompilerParams(dimension_semantics=semantics, vmem_limit_bytes=VMEM_LIMIT_BYTES)


def _sigmoid(x):
    return 1.0 / (1.0 + jnp.exp(-x))


def _silu(x):
    return x * _sigmoid(x)


def _rms(x):
    return x * lax.rsqrt(jnp.mean(x * x, axis=-1, keepdims=True) + RMS_EPS)


def _dot(a, b):
    return jnp.dot(a, b, preferred_element_type=F32)


def _dot_nt(a, b):
    return lax.dot_general(a, b, (((1,), (1,)), ((), ())), preferred_element_type=F32)


def _split3(x):
    hi = x.astype(BF16)
    r = x - hi.astype(F32)
    mid = r.astype(BF16)
    lo = (r - mid.astype(F32)).astype(BF16)
    return hi, mid, lo


def _dot_exact_rhs(x, sel):
    hi, mid, lo = _split3(x)
    return _dot(hi, sel) + _dot(mid, sel) + _dot(lo, sel)


def _ffn_kernel(x_ref, gpre_ref, gpost_ref, wg_ref, wu_ref, wd_ref, o_ref, xn_ref):
    j = pl.program_id(1)
    last = pl.num_programs(1) - 1

    def step(first, final):
        if first:
            xn = (_rms(x_ref[...]) * gpre_ref[...]).astype(BF16)
            xn_ref[...] = xn
        else:
            xn = xn_ref[...]
        g = _dot(xn, wg_ref[...])
        u = _dot(xn, wu_ref[...])
        a = (_silu(g) * u).astype(BF16)
        acc = _dot(a, wd_ref[...])
        if not first:
            acc += o_ref[...]
        if final:
            o_ref[...] = x_ref[...] + FFN_RESID * (_rms(acc) * gpost_ref[...])
        else:
            o_ref[...] = acc

    pl.when(j == 0)(functools.partial(step, True, False))
    pl.when((j > 0) & (j < last))(functools.partial(step, False, False))
    pl.when(j == last)(functools.partial(step, False, True))


def _ffn(x, gpre, gpost, wg, wu, wd):
    n, d = x.shape
    dff = wg.shape[1]
    tm, tf = min(FFN_ROW_TILE, n), FFN_TILE
    return pl.pallas_call(
        _ffn_kernel,
        grid=(n // tm, dff // tf),
        in_specs=[
            pl.BlockSpec((tm, d), lambda i, j: (i, 0)),
            pl.BlockSpec((1, d), lambda i, j: (0, 0)),
            pl.BlockSpec((1, d), lambda i, j: (0, 0)),
            pl.BlockSpec((d, tf), lambda i, j: (0, j)),
            pl.BlockSpec((d, tf), lambda i, j: (0, j)),
            pl.BlockSpec((tf, d), lambda i, j: (j, 0)),
        ],
        out_specs=pl.BlockSpec((tm, d), lambda i, j: (i, 0)),
        out_shape=jax.ShapeDtypeStruct((n, d), F32),
        scratch_shapes=[pltpu.VMEM((tm, d), BF16)],
        compiler_params=_params(("parallel", "arbitrary")),
        name="ffn",
    )(x, gpre, gpost, wg, wu, wd)


def _cast3_kernel(a_ref, b_ref, c_ref, oa_ref, ob_ref, oc_ref):
    oa_ref[...] = a_ref[...].astype(oa_ref.dtype)
    ob_ref[...] = b_ref[...].astype(ob_ref.dtype)
    oc_ref[...] = c_ref[...].astype(oc_ref.dtype)


def _ffn_weights_bf16(wg, wu, wd, layer):
    steps = CAST_STEPS

    def spec_in(w):
        return pl.BlockSpec((None, w.shape[1] // steps, w.shape[2]), lambda r: (layer, r, 0))

    def spec_out(w):
        return pl.BlockSpec((w.shape[1] // steps, w.shape[2]), lambda r: (r, 0))

    ws = (wg, wu, wd)
    return pl.pallas_call(
        _cast3_kernel,
        grid=(steps,),
        in_specs=[spec_in(w) for w in ws],
        out_specs=[spec_out(w) for w in ws],
        out_shape=[jax.ShapeDtypeStruct(w.shape[1:], BF16) for w in ws],
        compiler_params=_params(("parallel",)),
        name="ffn_weights_bf16",
    )(*ws)


def _in_proj_kernel(x_ref, g_ref, w_ref, oa_ref, ob_ref, xn_ref, *, na):
    j = pl.program_id(1)

    @pl.when(j == 0)
    def _():
        xn_ref[...] = (_rms(x_ref[...]) * g_ref[...]).astype(BF16)

    @pl.when(j < na)
    def _():
        oa_ref[...] = _dot(xn_ref[...], w_ref[j]).astype(oa_ref.dtype)

    @pl.when(j >= na)
    def _():
        ob_ref[...] = _dot(xn_ref[...], w_ref[j])


def _in_proj(x, g, w):
    n, d = x.shape
    nt, _, tn = w.shape
    na = A_WIDTH // tn
    tm = min(ROW_TILE, n)
    return pl.pallas_call(
        functools.partial(_in_proj_kernel, na=na),
        grid=(n // tm, nt),
        in_specs=[
            pl.BlockSpec((tm, d), lambda i, j: (i, 0)),
            pl.BlockSpec((1, d), lambda i, j: (0, 0)),
            pl.BlockSpec((nt, d, tn), lambda i, j: (0, 0, 0), pipeline_mode=pl.Buffered(1)),
        ],
        out_specs=[
            pl.BlockSpec((tm, tn), lambda i, j: (i, jnp.minimum(j, na - 1))),
            pl.BlockSpec((tm, tn), lambda i, j: (i, jnp.maximum(j - na, 0))),
        ],
        out_shape=[jax.ShapeDtypeStruct((n, A_WIDTH), BF16),
                   jax.ShapeDtypeStruct((n, B_WIDTH), F32)],
        scratch_shapes=[pltpu.VMEM((tm, d), BF16)],
        compiler_params=_params(("parallel", "arbitrary")),
        name="in_proj",
    )(x, g, w)


def _out_proj_kernel(h_ref, a_ref, s_ref, p_ref, w_ref, g_ref, o_ref):
    m = _dot(a_ref[...], w_ref[0:NSA_WIDTH, :])
    m += _dot(s_ref[...], w_ref[NSA_WIDTH:NSA_WIDTH + SSM_WIDTH, :])
    m += _dot(p_ref[...], w_ref[NSA_WIDTH + SSM_WIDTH:, :])
    o_ref[...] = h_ref[...] + _rms(m) * g_ref[...]


def _out_proj(h, o_nsa, o_ssm, o_pool, w, g):
    n, d = h.shape
    tm = min(ROW_TILE, n)
    return pl.pallas_call(
        _out_proj_kernel,
        grid=(n // tm,),
        in_specs=[
            pl.BlockSpec((tm, d), lambda i: (i, 0)),
            pl.BlockSpec((tm, NSA_WIDTH), lambda i: (i, 0)),
            pl.BlockSpec((tm, SSM_WIDTH), lambda i: (i, 0)),
            pl.BlockSpec((tm, POOL_WIDTH), lambda i: (i, 0)),
            pl.BlockSpec((d, d), lambda i: (0, 0)),
            pl.BlockSpec((1, d), lambda i: (0, 0)),
        ],
        out_specs=pl.BlockSpec((tm, d), lambda i: (i, 0)),
        out_shape=jax.ShapeDtypeStruct((n, d), F32),
        compiler_params=_params(("parallel",)),
        name="out_proj",
    )(h, o_nsa, o_ssm, o_pool, w, g)


def _pool_kernel(halo_ref, x_ref, w_ref, scale_ref, o_ref, *, tiles_per_seq):
    tile = x_ref.shape[0]
    it = pl.program_id(0) % tiles_per_seq
    halo = jnp.where(it == 0, 0.0, halo_ref[...])
    ext = jnp.concatenate([halo, x_ref[...]], axis=0)
    pos = it * tile + lax.broadcasted_iota(jnp.int32, (tile, POOL_GROUP), 0)
    for gi, w in enumerate(POOL_WINDOWS):
        xg = ext[:, gi * POOL_GROUP:(gi + 1) * POOL_GROUP]
        s, k = xg, 1
        while k < w:
            s = s + pltpu.roll(s, k, axis=0)
            k *= 2
        cnt = jnp.minimum(pos + 1, w).astype(F32)
        dlt = s[POOL_HALO:] / cnt - xg[POOL_HALO:]
        y = _dot(dlt.astype(BF16), w_ref[gi])
        sl = slice(gi * POOL_GROUP, (gi + 1) * POOL_GROUP)
        o_ref[:, sl] = (y * scale_ref[:, sl]).astype(o_ref.dtype)


def _pool(proj_b, pool_w, pool_scale, seq_len):
    n = proj_b.shape[0]
    tile = min(POOL_TILE, seq_len)
    tiles_per_seq = seq_len // tile
    ratio = tile // POOL_HALO
    col = B_POOL // POOL_WIDTH
    return pl.pallas_call(
        functools.partial(_pool_kernel, tiles_per_seq=tiles_per_seq),
        grid=(n // tile,),
        in_specs=[
            pl.BlockSpec((POOL_HALO, POOL_WIDTH), lambda i: (jnp.maximum(i * ratio - 1, 0), col)),
            pl.BlockSpec((tile, POOL_WIDTH), lambda i: (i, col)),
            pl.BlockSpec((len(POOL_WINDOWS), POOL_GROUP, POOL_GROUP), lambda i: (0, 0, 0)),
            pl.BlockSpec((1, POOL_WIDTH), lambda i: (0, 0)),
        ],
        out_specs=pl.BlockSpec((tile, POOL_WIDTH), lambda i: (i, 0)),
        out_shape=jax.ShapeDtypeStruct((n, POOL_WIDTH), BF16),
        compiler_params=_params(("parallel",)),
        name="pool",
    )(proj_b, proj_b, pool_w, pool_scale)


def _ssd_kernel(halo_ref, xbc_ref, z_ref, dt_ref, convw_ref, convb_ref, dtb_ref, alog_ref,
                dskip_ref, normw_ref, o_ref, state_ref):
    L = xbc_ref.shape[0]
    c = pl.program_id(1)

    @pl.when(c == 0)
    def _():
        state_ref[...] = jnp.zeros_like(state_ref)

    halo = jnp.where(c == 0, 0.0, halo_ref[...])
    ext = jnp.concatenate([halo, xbc_ref[...]], axis=0)
    conv = convb_ref[...] + convw_ref[SSM_CONV - 1:SSM_CONV, :] * ext[CONV_HALO:]
    for k in range(1, SSM_CONV):
        conv += convw_ref[SSM_CONV - 1 - k:SSM_CONV - k, :] * pltpu.roll(ext, k, axis=0)[CONV_HALO:]
    act = _silu(conv)
    xs = act[:, :SSM_WIDTH]
    bm = act[:, SSM_WIDTH:SSM_WIDTH + SSM_GROUPS * SSM_STATE]
    cm = act[:, SSM_WIDTH + SSM_GROUPS * SSM_STATE:]

    lane = lax.broadcasted_iota(jnp.int32, (1, LANES), 1)
    xdt = dt_ref[...] + dtb_ref[...]
    dt = jnp.maximum(xdt, 0.0) + jnp.log1p(jnp.exp(-jnp.abs(xdt)))
    a_head = jnp.where(lane < SSM_HEADS, -jnp.exp(alog_ref[...]), 0.0)
    cs = a_head * dt
    row = lax.broadcasted_iota(jnp.int32, (L, LANES), 0)
    k = 1
    while k < L:
        cs = cs + jnp.where(row >= k, pltpu.roll(cs, k, axis=0), 0.0)
        k *= 2
    cs_t = cs.T

    er = lax.broadcasted_iota(jnp.int32, (LANES, SSM_WIDTH), 0)
    ec = lax.broadcasted_iota(jnp.int32, (LANES, SSM_WIDTH), 1)
    expand = jnp.where(er == ec // SSM_HEAD_DIM, 1.0, 0.0).astype(BF16)
    dtx = _dot_exact_rhs(dt, expand)
    csx = _dot_exact_rhs(cs, expand)
    cs_last = csx[L - 1:L, :]

    x_dt = xs * dtx
    x_end = (x_dt * jnp.exp(cs_last - csx)).astype(BF16)
    x_dt16 = x_dt.astype(BF16)
    ecs = jnp.exp(csx)
    li = lax.broadcasted_iota(jnp.int32, (L, L), 0)
    si = lax.broadcasted_iota(jnp.int32, (L, L), 1)
    gw = SSM_WIDTH // SSM_GROUPS
    hpg = SSM_HEADS // SSM_GROUPS
    lane_g = lax.broadcasted_iota(jnp.int32, (1, gw), 1)
    ys = []
    for g in range(SSM_GROUPS):
        bg = bm[:, g * SSM_STATE:(g + 1) * SSM_STATE]
        cg = cm[:, g * SSM_STATE:(g + 1) * SSM_STATE].astype(BF16)
        gs = slice(g * gw, (g + 1) * gw)
        cb = _dot_nt(cg, bg.astype(BF16))
        y_g = jnp.zeros((L, gw), F32)
        for hh in range(hpg):
            h = g * hpg + hh
            seg = cs[:, h:h + 1] - cs_t[h:h + 1, :]
            m_h = (cb * jnp.exp(jnp.where(li >= si, seg, NEG))).astype(BF16)
            x_h = jnp.where(lane_g // SSM_HEAD_DIM == hh, x_dt16[:, gs], jnp.zeros((), BF16))
            y_g += _dot(m_h, x_h)
        st = state_ref[g]
        y_g += _dot(cg, st.astype(BF16)) * ecs[:, gs]
        state_ref[g] = jnp.exp(cs_last[:, gs]) * st + _dot(bg.T.astype(BF16), x_end[:, gs])
        ys.append(y_g)
    y = jnp.concatenate(ys, axis=1) + dskip_ref[...] * xs
    y = y * _silu(z_ref[...])
    outs = []
    for g in range(SSM_GROUPS):
        outs.append(_rms(y[:, g * gw:(g + 1) * gw]))
    o_ref[...] = (jnp.concatenate(outs, axis=1) * normw_ref[...]).astype(o_ref.dtype)


def _ssd(proj_b, conv_w, conv_b, dt_bias, a_log, d_skip, norm_w, batch, seq_len):
    n = proj_b.shape[0]
    L = SSD_CHUNK
    nch = seq_len // L
    ratio = L // CONV_HALO
    return pl.pallas_call(
        _ssd_kernel,
        grid=(batch, nch),
        in_specs=[
            pl.BlockSpec((CONV_HALO, SSM_CONV_DIM),
                         lambda b, c: (jnp.maximum((b * nch + c) * ratio - 1, 0), B_XBC // SSM_CONV_DIM)),
            pl.BlockSpec((L, SSM_CONV_DIM), lambda b, c: (b * nch + c, B_XBC // SSM_CONV_DIM)),
            pl.BlockSpec((L, SSM_WIDTH), lambda b, c: (b * nch + c, B_Z // SSM_WIDTH)),
            pl.BlockSpec((L, LANES), lambda b, c: (b * nch + c, B_DT // LANES)),
            pl.BlockSpec((SSM_CONV, SSM_CONV_DIM), lambda b, c: (0, 0)),
            pl.BlockSpec((1, SSM_CONV_DIM), lambda b, c: (0, 0)),
            pl.BlockSpec((1, LANES), lambda b, c: (0, 0)),
            pl.BlockSpec((1, LANES), lambda b, c: (0, 0)),
            pl.BlockSpec((1, SSM_WIDTH), lambda b, c: (0, 0)),
            pl.BlockSpec((1, SSM_WIDTH), lambda b, c: (0, 0)),
        ],
        out_specs=pl.BlockSpec((L, SSM_WIDTH), lambda b, c: (b * nch + c, 0)),
        out_shape=jax.ShapeDtypeStruct((n, SSM_WIDTH), BF16),
        scratch_shapes=[pltpu.VMEM((SSM_GROUPS, SSM_STATE, SSM_WIDTH // SSM_GROUPS), F32)],
        compiler_params=_params(("parallel", "arbitrary")),
        name="ssd",
    )(proj_b, proj_b, proj_b, proj_b, conv_w, conv_b, dt_bias, a_log, d_skip, norm_w)


def _rope(x, cos, sin_signed):
    return x * cos + pltpu.roll(x, HEAD_DIM // 2, axis=1) * sin_signed


def _compress(src_ref, w1_ref, w2_ref, pe_ref):
    n_cmp = src_ref.shape[0] // CMP_STRIDE
    xs = [src_ref[pl.ds(p, n_cmp, stride=CMP_STRIDE), :] for p in range(CMP_STRIDE)]
    x = jnp.concatenate(xs, axis=1).astype(BF16)
    half = CMP_STRIDE * HEAD_DIM
    first = _dot(x, w1_ref[0:half, :])
    second = _dot(x, w1_ref[half:2 * half, :])
    pe = jnp.broadcast_to(pe_ref[...], (SUBLANES, 2 * half)).astype(BF16)
    pe_term = _dot(pe, w1_ref[...])[0:1, :]
    pre = first + pltpu.roll(second, n_cmp - 1, axis=0) + pe_term
    return _dot(_silu(pre).astype(BF16), w2_ref[...])


def _nsa_kernel(q_ref, gate_ref, kc_src_ref, vc_src_ref, ks_ref, vs_ref, kw_ref, vw_ref,
                cos_ref, sin_ref, w1k_ref, w2k_ref, w1v_ref, w2v_ref, pek_ref, pev_ref,
                o_ref,
                kc_scr, vct_scr, kx_scr, kwp_scr, vst_scr, vwt_scr, qx_scr, score_scr, rank_scr,
                m_scr, l_scr, acc_scr, oct_scr, owt_scr, s_scr):
    tq = q_ref.shape[0]
    seq = ks_ref.shape[0]
    tk = min(SLC_KEY_TILE, seq)
    nh = NSA_HPG
    n_cmp, n_slc = seq // CMP_STRIDE, seq // SLC_LEN
    topk = min(SLC_TOPK, n_slc)
    qi = pl.program_id(2)
    q0 = qi * tq
    qk_scale = HEAD_DIM ** -0.5 * LOG2E

    @pl.when(qi == 0)
    def _():
        kc_scr[...] = _compress(kc_src_ref, w1k_ref, w2k_ref, pek_ref).astype(BF16)
        vct_scr[...] = _compress(vc_src_ref, w1v_ref, w2v_ref, pev_ref).T.astype(BF16)
        kwp_scr[0:WIN_LEN, :] = jnp.zeros((WIN_LEN, HEAD_DIM), BF16)
        for j in range(WIN_LEN // tq):
            vwt_scr[j] = jnp.zeros((HEAD_DIM, tq), BF16)

        def prep_rows(i, carry):
            rows = pl.ds(pl.multiple_of(i * tq, tq), tq)
            prows = pl.ds(pl.multiple_of(WIN_LEN + i * tq, tq), tq)
            cos, sin = cos_ref[rows, :], sin_ref[rows, :]
            kx_scr[rows, 0:HEAD_DIM] = _rope(ks_ref[rows, :].astype(F32), cos, sin).astype(BF16)
            blk = (i * tq + lax.broadcasted_iota(jnp.int32, (tq, LANES), 0)) // SLC_LEN
            lane = lax.broadcasted_iota(jnp.int32, (tq, LANES), 1)
            kx_scr[rows, HEAD_DIM:2 * HEAD_DIM] = jnp.where(lane == blk, 1.0, 0.0).astype(BF16)
            kwp_scr[prows, :] = _rope(kw_ref[rows, :].astype(F32), cos, sin).astype(BF16)
            vst_scr[i] = vs_ref[rows, :].astype(F32).T.astype(BF16)
            vwt_scr[i + WIN_LEN // tq] = vw_ref[rows, :].astype(F32).T.astype(BF16)
            return carry

        lax.fori_loop(0, seq // tq, prep_rows, 0)

    qrows = pl.ds(pl.multiple_of(q0, tq), tq)
    qlane = lax.broadcasted_iota(jnp.int32, (1, tq), 1)
    tpos = q0 + qlane

    def q_rows(h):
        return slice(h * tq, (h + 1) * tq)

    cos_q, sin_q = cos_ref[qrows, :], sin_ref[qrows, :]
    for h in range(nh):
        hs = slice(h * HEAD_DIM, (h + 1) * HEAD_DIM)
        qx_scr[q_rows(h), 0:HEAD_DIM] = (_rope(q_ref[:, hs].astype(F32), cos_q, sin_q) * qk_scale).astype(BF16)

    def heads(x):
        return jnp.concatenate([x] * nh, axis=1)

    q_stack = jnp.concatenate([q_ref[:, h * HEAD_DIM:(h + 1) * HEAD_DIM] for h in range(nh)], axis=0)
    nblk = lax.broadcasted_iota(jnp.int32, (n_cmp, 1), 0)
    cmp_bias = jnp.where((nblk * CMP_STRIDE + (CMP_LEN - 1)) <= tpos, 0.0, NEG)
    any_visible = jnp.where(tpos >= CMP_LEN - 1, 1.0, 0.0)
    s = _dot_nt(kc_scr[...], q_stack) * qk_scale + heads(cmp_bias)
    e = jnp.exp2(s - jnp.max(s, axis=0, keepdims=True))
    p = e * (heads(any_visible) / jnp.sum(e, axis=0, keepdims=True))
    p_sum = p[:, 0:tq]
    for h in range(1, nh):
        p_sum += p[:, h * tq:(h + 1) * tq]
    oct_scr[...] = _dot(vct_scr[...], p.astype(BF16))

    wspan = tq + WIN_LEN
    wrows = pl.ds(pl.multiple_of(q0, tq), wspan)
    c = lax.broadcasted_iota(jnp.int32, (wspan, 1), 0)
    win_ok = (c > qlane) & (c <= qlane + WIN_LEN) & (c + q0 >= WIN_LEN)
    wb = jnp.where(win_ok, 0.0, NEG)
    s = _dot_nt(kwp_scr[wrows, :], qx_scr[:, 0:HEAD_DIM]) + heads(wb)
    p = jnp.exp2(s - jnp.max(s, axis=0, keepdims=True))
    pb = p.astype(BF16)
    o = _dot(vwt_scr[qi], pb[0:tq])
    for j in range(1, wspan // tq):
        o += _dot(vwt_scr[qi + j], pb[j * tq:(j + 1) * tq])
    owt_scr[...] = o / jnp.sum(p, axis=0, keepdims=True)

    nr = lax.broadcasted_iota(jnp.int32, (n_slc, n_cmp), 1) * CMP_STRIDE
    jr = lax.broadcasted_iota(jnp.int32, (n_slc, n_cmp), 0) * SLC_LEN
    overlap_t = jnp.where((nr < jr + SLC_LEN) & (nr + CMP_LEN > jr), 1.0, 0.0).astype(BF16)
    ps_hi = p_sum.astype(BF16)
    ps_lo = (p_sum - ps_hi.astype(F32)).astype(BF16)
    imp_t = _dot(overlap_t, ps_hi) + _dot(overlap_t, ps_lo)

    jblk = lax.broadcasted_iota(jnp.int32, (n_slc, tq), 0)
    qblk = (q0 + lax.broadcasted_iota(jnp.int32, (n_slc, tq), 1)) // SLC_LEN
    valid = jblk <= qblk
    forced = (jblk == 0) | (jblk == qblk) | (jblk == qblk - 1)
    score_scr[...] = jnp.where(valid, jnp.where(forced, 1e9, imp_t), -1e9)
    rank_scr[...] = jnp.zeros_like(rank_scr)
    ng = n_slc // SUBLANES
    g_last = ((q0 + tq - 1) // SLC_LEN) // SUBLANES
    row8 = lax.broadcasted_iota(jnp.int32, (SUBLANES, tq), 0)
    for gp in range(ng):
        @pl.when(gp <= g_last)
        def _():
            cnt = [jnp.zeros((SUBLANES, tq), F32) for _ in range(ng)]
            for jj in range(SUBLANES):
                jp = gp * SUBLANES + jj
                other = jnp.broadcast_to(score_scr[jp:jp + 1, :], (SUBLANES, tq))
                for g in range(ng):
                    sc = score_scr[g * SUBLANES:(g + 1) * SUBLANES, :]
                    if g > gp:
                        ahead = other >= sc
                    elif g < gp:
                        ahead = other > sc
                    else:
                        ahead = (other > sc) | ((other == sc) & (row8 > jj))
                    cnt[g] += jnp.where(ahead, 1.0, 0.0)
            for g in range(ng):
                rank_scr[g * SUBLANES:(g + 1) * SUBLANES, :] += cnt[g]

    sel_t = jnp.where(valid & (rank_scr[...] < topk), 1.0, 0.0).astype(BF16)
    sel_pad = jnp.concatenate([sel_t, jnp.zeros((LANES - n_slc, tq), BF16)], axis=0)
    ri = lax.broadcasted_iota(jnp.int32, (tq, tq), 0)
    ci = lax.broadcasted_iota(jnp.int32, (tq, tq), 1)
    eye = jnp.where(ri == ci, 1.0, 0.0).astype(BF16)
    sel = _dot_nt(eye, sel_pad)
    lane = lax.broadcasted_iota(jnp.int32, (tq, LANES), 1)
    sel_bias = jnp.where((lane < n_slc) & (sel < 0.5), NEG, 0.0).astype(BF16)

    for h in range(nh):
        qx_scr[q_rows(h), HEAD_DIM:2 * HEAD_DIM] = sel_bias
    m_scr[...] = jnp.full(m_scr.shape, NEG, F32)
    l_scr[...] = jnp.zeros_like(l_scr)
    acc_scr[...] = jnp.zeros_like(acc_scr)
    chunks = tk // tq

    def scores(kt):
        return _dot_nt(kx_scr[pl.ds(pl.multiple_of(kt * tk, tk), tk), :], qx_scr[...])

    def softmax_update(s, kt):
        m_prev = m_scr[...]
        m_new = jnp.maximum(m_prev, jnp.max(s, axis=0, keepdims=True))
        p = jnp.exp2(s - m_new)
        alpha = jnp.exp2(m_prev - m_new)
        l_scr[...] = alpha * l_scr[...] + jnp.sum(p, axis=0, keepdims=True)
        pb = p.astype(BF16)
        pv = _dot(vst_scr[kt * chunks], pb[0:tq])
        for j in range(1, chunks):
            pv += _dot(vst_scr[kt * chunks + j], pb[j * tq:(j + 1) * tq])
        acc_scr[...] = alpha * acc_scr[...] + pv
        m_scr[...] = m_new

    n_tiles = (q0 + tq + tk - 1) // tk
    s_scr[...] = scores(0)

    def slc_step(kt):
        s = s_scr[...]
        s_next = scores(kt + 1)
        softmax_update(s, kt)
        s_scr[...] = s_next

    def slc_pair(i, carry):
        slc_step(2 * i)
        slc_step(2 * i + 1)
        return carry

    n_steps = n_tiles - 1
    lax.fori_loop(0, n_steps // 2, slc_pair, 0)

    @pl.when(n_steps % 2 == 1)
    def _():
        slc_step(n_steps - 1)

    kpos = (n_tiles - 1) * tk + lax.broadcasted_iota(jnp.int32, (tk, 1), 0)
    softmax_update(s_scr[...] + heads(jnp.where(kpos <= tpos, 0.0, NEG)), n_tiles - 1)

    gates = _sigmoid(gate_ref[...]).T
    for h in range(nh):
        hs = slice(h * HEAD_DIM, (h + 1) * HEAD_DIM)
        ql = slice(h * tq, (h + 1) * tq)
        o = (gates[3 * h:3 * h + 1] * oct_scr[:, ql]
             + gates[3 * h + 1:3 * h + 2] * (acc_scr[:, ql] / l_scr[:, ql])
             + gates[3 * h + 2:3 * h + 3] * owt_scr[:, ql])
        o_ref[:, hs] = o.T.astype(o_ref.dtype)


def _nsa(proj_a, proj_b, cos, sin_signed, w1k, w2k, w1v, w2v, pe_k, pe_v, batch, seq_len):
    n = proj_a.shape[0]
    tq = min(ATT_TILE, seq_len)
    nq = seq_len // tq
    gw = NSA_HPG * HEAD_DIM
    n_cmp, n_slc = seq_len // CMP_STRIDE, seq_len // SLC_LEN
    assert n_slc <= LANES // 2 and n_slc % SUBLANES == 0 and seq_len % min(SLC_KEY_TILE, seq_len) == 0

    def kv_spec(col0):
        return pl.BlockSpec((seq_len, HEAD_DIM), lambda b, g, i: (b, col0 // HEAD_DIM + g))

    def full(shape):
        return pl.BlockSpec(shape, lambda b, g, i: (0,) * len(shape))

    return pl.pallas_call(
        _nsa_kernel,
        grid=(batch, NSA_GROUPS, nq),
        in_specs=[
            pl.BlockSpec((tq, gw), lambda b, g, i: (b * nq + i, A_Q // gw + g)),
            pl.BlockSpec((tq, LANES), lambda b, g, i: (b * nq + i, B_GATE // LANES + g)),
            kv_spec(B_KC), kv_spec(B_VC),
            kv_spec(A_KS), kv_spec(A_VS), kv_spec(A_KW), kv_spec(A_VW),
            full((seq_len, HEAD_DIM)), full((seq_len, HEAD_DIM)),
            full((CMP_LEN * HEAD_DIM, HEAD_DIM)), full((HEAD_DIM, HEAD_DIM)),
            full((CMP_LEN * HEAD_DIM, HEAD_DIM)), full((HEAD_DIM, HEAD_DIM)),
            full((1, CMP_LEN * HEAD_DIM)), full((1, CMP_LEN * HEAD_DIM)),
        ],
        out_specs=pl.BlockSpec((tq, gw), lambda b, g, i: (b * nq + i, g)),
        out_shape=jax.ShapeDtypeStruct((n, NSA_WIDTH), BF16),
        scratch_shapes=[
            pltpu.VMEM((n_cmp, HEAD_DIM), BF16),
            pltpu.VMEM((HEAD_DIM, n_cmp), BF16),
            pltpu.VMEM((seq_len, 2 * HEAD_DIM), BF16),
            pltpu.VMEM((seq_len + WIN_LEN, HEAD_DIM), BF16),
            pltpu.VMEM((seq_len // tq, HEAD_DIM, tq), BF16),
            pltpu.VMEM(((seq_len + WIN_LEN) // tq, HEAD_DIM, tq), BF16),
            pltpu.VMEM((NSA_HPG * tq, 2 * HEAD_DIM), BF16),
            pltpu.VMEM((n_slc, tq), F32),
            pltpu.VMEM((n_slc, tq), F32),
            pltpu.VMEM((1, NSA_HPG * tq), F32),
            pltpu.VMEM((1, NSA_HPG * tq), F32),
            pltpu.VMEM((HEAD_DIM, NSA_HPG * tq), F32),
            pltpu.VMEM((HEAD_DIM, NSA_HPG * tq), F32),
            pltpu.VMEM((HEAD_DIM, NSA_HPG * tq), F32),
            pltpu.VMEM((min(SLC_KEY_TILE, seq_len), NSA_HPG * tq), F32),
        ],
        compiler_params=_params(("parallel", "parallel", "arbitrary")),
        name="nsa",
    )(proj_a, proj_b, proj_b, proj_b, proj_a, proj_a, proj_a, proj_a,
      cos, sin_signed, w1k, w2k, w1v, w2v, pe_k, pe_v)


def _rope_tables(seq_len):
    inv = ROPE_THETA ** (-jnp.arange(0, HEAD_DIM, 2, dtype=F32) / HEAD_DIM)
    ang = jnp.arange(seq_len, dtype=F32)[:, None] * inv[None, :]
    ang = jnp.concatenate([ang, ang], -1)
    sign = jnp.concatenate([-jnp.ones((HEAD_DIM // 2,), F32), jnp.ones((HEAD_DIM // 2,), F32)])
    return jnp.cos(ang), jnp.sin(ang) * sign


def _split_w_in(w_in):
    offs = [0]
    for wd in IN_WIDTHS:
        offs.append(offs[-1] + wd)
    (q, kc, vc, ks, vs, kw, vw, gates, z, xbc, dt, pv) = [
        w_in[:, offs[i]:offs[i + 1]] for i in range(len(IN_WIDTHS))]
    d = w_in.shape[0]

    def pad_to(w, width):
        return jnp.pad(w, ((0, 0), (0, width - w.shape[1])))

    per_group = 3 * NSA_HPG
    w_a = jnp.concatenate([q, ks, vs, kw, vw], axis=1)
    w_b = jnp.concatenate(
        [xbc, z, pv, pad_to(dt, LANES),
         pad_to(gates[:, :per_group], LANES), pad_to(gates[:, per_group:], LANES),
         kc, vc, jnp.zeros((d, B_WIDTH - B_VC - KV_WIDTH), w_in.dtype)], axis=1)
    return w_a.astype(BF16), w_b.astype(BF16)


def _in_proj_weights(w_in):
    w_a, w_b = _split_w_in(w_in)
    w = jnp.concatenate([w_a, w_b], axis=1)
    d = w.shape[0]
    return w.reshape(d, -1, PROJ_COL_TILE).transpose(1, 0, 2)


def _pad_lanes(v):
    return jnp.pad(v, (0, LANES - v.shape[0]))[None, :]


def kernel(x, ffn1_norm_pre, ffn1_norm_post, ffn1_w_gate, ffn1_w_up, ffn1_w_down, mix_norm_pre, mix_norm_post, w_in, cmp_pe_k, cmp_pe_v, cmp_k_w1, cmp_k_w2, cmp_v_w1, cmp_v_w2, ssm_conv_w, ssm_conv_b, ssm_dt_bias, ssm_a_log, ssm_d, ssm_norm, pool_w, pool_scale, w_out, ffn2_norm_pre, ffn2_norm_post, ffn2_w_gate, ffn2_w_up, ffn2_w_down):
    batch, seq_len, d = x.shape
    depth = w_in.shape[0]
    cos, sin_signed = _rope_tables(seq_len)
    h = x.reshape(batch * seq_len, d)
    for i in range(depth):
        h = _ffn(h, ffn1_norm_pre[i][None], ffn1_norm_post[i][None],
                 *_ffn_weights_bf16(ffn1_w_gate, ffn1_w_up, ffn1_w_down, i))
        proj_a, proj_b = _in_proj(h, mix_norm_pre[i][None], _in_proj_weights(w_in[i]))
        o_nsa = _nsa(proj_a, proj_b, cos, sin_signed,
                     cmp_k_w1[i].astype(BF16), cmp_k_w2[i].astype(BF16),
                     cmp_v_w1[i].astype(BF16), cmp_v_w2[i].astype(BF16),
                     cmp_pe_k[i].reshape(1, -1), cmp_pe_v[i].reshape(1, -1), batch, seq_len)
        o_ssm = _ssd(proj_b, ssm_conv_w[i], ssm_conv_b[i][None], _pad_lanes(ssm_dt_bias[i]),
                     _pad_lanes(ssm_a_log[i]), jnp.repeat(ssm_d[i], SSM_HEAD_DIM)[None],
                     ssm_norm[i][None], batch, seq_len)
        o_pool = _pool(proj_b, pool_w[i].astype(BF16), pool_scale[i][None], seq_len)
        h = _out_proj(h, o_nsa, o_ssm, o_pool, w_out[i].astype(BF16), mix_norm_post[i][None])
        h = _ffn(h, ffn2_norm_pre[i][None], ffn2_norm_post[i][None],
                 *_ffn_weights_bf16(ffn2_w_gate, ffn2_w_up, ffn2_w_down, i))
    return h.reshape(batch, seq_len, d)
```

```python
import functools
import math

import jax
import jax.numpy as jnp
from jax import lax
from jax.experimental import pallas as pl
from jax.experimental.pallas import tpu as pltpu

F32 = jnp.float32
BF16 = jnp.bfloat16

D_MODEL = 2048
HEAD_DIM = 128
NSA_WIDTH = 1024
NSA_HEADS = 8
NSA_GROUPS = 2
NSA_HPG = NSA_HEADS // NSA_GROUPS
KV_WIDTH = NSA_GROUPS * HEAD_DIM
CMP_LEN = 32
CMP_STRIDE = 16
SLC_LEN = 64
SLC_TOPK = 16
WIN_LEN = 512
ROPE_THETA = 10000.0
SSM_WIDTH = 512
SSM_HEAD_DIM = 64
SSM_HEADS = 8
SSM_GROUPS = 2
SSM_STATE = 128
SSM_CONV = 4
SSM_CONV_DIM = SSM_WIDTH + 2 * SSM_GROUPS * SSM_STATE
POOL_WIDTH = 512
POOL_WINDOWS = (2, 4, 8, 16)
POOL_GROUP = POOL_WIDTH // len(POOL_WINDOWS)
D_FF = 5632
FFN_RESID = 0.5
RMS_EPS = 1e-6
NEG = -1e30
LOG2E = math.log2(math.e)
IN_WIDTHS = (NSA_WIDTH, KV_WIDTH, KV_WIDTH, KV_WIDTH, KV_WIDTH, KV_WIDTH, KV_WIDTH,
             3 * NSA_HEADS, SSM_WIDTH, SSM_CONV_DIM, SSM_HEADS, POOL_WIDTH)

LANES = 128
SUBLANES = 8
VMEM_LIMIT_BYTES = 60000 * 1024

A_Q, A_KS, A_VS, A_KW, A_VW = 0, 1024, 1280, 1536, 1792
A_WIDTH = 2048
B_XBC, B_Z, B_POOL, B_DT, B_GATE, B_KC, B_VC = 0, 1024, 1536, 2048, 2176, 2432, 2688
B_WIDTH = 3072

FFN_ROW_TILE = 512
FFN_TILE = 512
CAST_STEPS = 8
ROW_TILE = 512
PROJ_B_STEPS = 2
ATT_TILE = 256
SLC_KEY_TILE = 512
SSD_CHUNK = 128
POOL_TILE = 512
POOL_HALO = 16
CONV_HALO = 8


def _params(semantics):
    return pltpu.CompilerParams(dimension_semantics=semantics, vmem_limit_bytes=VMEM_LIMIT_BYTES)


def _sigmoid(x):
    return 1.0 / (1.0 + jnp.exp(-x))


def _silu(x):
    return x * _sigmoid(x)


def _rms(x):
    return x * lax.rsqrt(jnp.mean(x * x, axis=-1, keepdims=True) + RMS_EPS)


def _dot(a, b):
    return jnp.dot(a, b, preferred_element_type=F32)


def _dot_nt(a, b):
    return lax.dot_general(a, b, (((1,), (1,)), ((), ())), preferred_element_type=F32)


def _split3(x):
    hi = x.astype(BF16)
    r = x - hi.astype(F32)
    mid = r.astype(BF16)
    lo = (r - mid.astype(F32)).astype(BF16)
    return hi, mid, lo


def _dot_exact_rhs(x, sel):
    hi, mid, lo = _split3(x)
    return _dot(hi, sel) + _dot(mid, sel) + _dot(lo, sel)


def _ffn_kernel(x_ref, gpre_ref, gpost_ref, wg_ref, wu_ref, wd_ref, o_ref, xn_ref):
    j = pl.program_id(1)
    last = pl.num_programs(1) - 1

    def step(first, final):
        if first:
            xn = (_rms(x_ref[...]) * gpre_ref[...]).astype(BF16)
            xn_ref[...] = xn
        else:
            xn = xn_ref[...]
        g = _dot(xn, wg_ref[...])
        u = _dot(xn, wu_ref[...])
        a = (_silu(g) * u).astype(BF16)
        acc = _dot(a, wd_ref[...])
        if not first:
            acc += o_ref[...]
        if final:
            o_ref[...] = x_ref[...] + FFN_RESID * (_rms(acc) * gpost_ref[...])
        else:
            o_ref[...] = acc

    pl.when(j == 0)(functools.partial(step, True, False))
    pl.when((j > 0) & (j < last))(functools.partial(step, False, False))
    pl.when(j == last)(functools.partial(step, False, True))


def _ffn(x, gpre, gpost, wg, wu, wd):
    n, d = x.shape
    dff = wg.shape[1]
    tm, tf = min(FFN_ROW_TILE, n), FFN_TILE
    return pl.pallas_call(
        _ffn_kernel,
        grid=(n // tm, dff // tf),
        in_specs=[
            pl.BlockSpec((tm, d), lambda i, j: (i, 0)),
            pl.BlockSpec((1, d), lambda i, j: (0, 0)),
            pl.BlockSpec((1, d), lambda i, j: (0, 0)),
            pl.BlockSpec((d, tf), lambda i, j: (0, j)),
            pl.BlockSpec((d, tf), lambda i, j: (0, j)),
            pl.BlockSpec((tf, d), lambda i, j: (j, 0)),
        ],
        out_specs=pl.BlockSpec((tm, d), lambda i, j: (i, 0)),
        out_shape=jax.ShapeDtypeStruct((n, d), F32),
        scratch_shapes=[pltpu.VMEM((tm, d), BF16)],
        compiler_params=_params(("parallel", "arbitrary")),
        name="ffn",
    )(x, gpre, gpost, wg, wu, wd)


def _cast3_kernel(a_ref, b_ref, c_ref, oa_ref, ob_ref, oc_ref):
    oa_ref[...] = a_ref[...].astype(oa_ref.dtype)
    ob_ref[...] = b_ref[...].astype(ob_ref.dtype)
    oc_ref[...] = c_ref[...].astype(oc_ref.dtype)


def _ffn_weights_bf16(wg, wu, wd, layer):
    steps = CAST_STEPS

    def spec_in(w):
        return pl.BlockSpec((None, w.shape[1] // steps, w.shape[2]), lambda r: (layer, r, 0))

    def spec_out(w):
        return pl.BlockSpec((w.shape[1] // steps, w.shape[2]), lambda r: (r, 0))

    ws = (wg, wu, wd)
    return pl.pallas_call(
        _cast3_kernel,
        grid=(steps,),
        in_specs=[spec_in(w) for w in ws],
        out_specs=[spec_out(w) for w in ws],
        out_shape=[jax.ShapeDtypeStruct(w.shape[1:], BF16) for w in ws],
        compiler_params=_params(("parallel",)),
        name="ffn_weights_bf16",
    )(*ws)


def _in_proj_kernel(x_ref, g_ref, wa_ref, wb_ref, oa_ref, ob_ref, xn_ref):
    j = pl.program_id(1)
    slab = ob_ref.shape[1]

    @pl.when(j == 0)
    def _():
        xn = (_rms(x_ref[...]) * g_ref[...]).astype(BF16)
        xn_ref[...] = xn
        oa_ref[...] = _dot(xn, wa_ref[...]).astype(oa_ref.dtype)

    for s in range(PROJ_B_STEPS):
        @pl.when(j == s + 1)
        def _():
            ob_ref[...] = _dot(xn_ref[...], wb_ref[:, s * slab:(s + 1) * slab])


def _in_proj(x, g, w_a, w_b):
    n, d = x.shape
    tm = min(ROW_TILE, n)
    slab = B_WIDTH // PROJ_B_STEPS
    return pl.pallas_call(
        _in_proj_kernel,
        grid=(n // tm, 1 + PROJ_B_STEPS),
        in_specs=[
            pl.BlockSpec((tm, d), lambda i, j: (i, 0)),
            pl.BlockSpec((1, d), lambda i, j: (0, 0)),
            pl.BlockSpec((d, A_WIDTH), lambda i, j: (0, 0), pipeline_mode=pl.Buffered(1)),
            pl.BlockSpec((d, B_WIDTH), lambda i, j: (0, 0), pipeline_mode=pl.Buffered(1)),
        ],
        out_specs=[
            pl.BlockSpec((tm, A_WIDTH), lambda i, j: (i, 0)),
            pl.BlockSpec((tm, slab), lambda i, j: (i, jnp.maximum(j - 1, 0))),
        ],
        out_shape=[jax.ShapeDtypeStruct((n, A_WIDTH), BF16),
                   jax.ShapeDtypeStruct((n, B_WIDTH), F32)],
        scratch_shapes=[pltpu.VMEM((tm, d), BF16)],
        compiler_params=_params(("parallel", "arbitrary")),
        name="in_proj",
    )(x, g, w_a, w_b)


def _out_proj_kernel(h_ref, a_ref, s_ref, p_ref, w_ref, g_ref, o_ref):
    m = _dot(a_ref[...], w_ref[0:NSA_WIDTH, :])
    m += _dot(s_ref[...], w_ref[NSA_WIDTH:NSA_WIDTH + SSM_WIDTH, :])
    m += _dot(p_ref[...], w_ref[NSA_WIDTH + SSM_WIDTH:, :])
    o_ref[...] = h_ref[...] + _rms(m) * g_ref[...]


def _out_proj(h, o_nsa, o_ssm, o_pool, w, g):
    n, d = h.shape
    tm = min(ROW_TILE, n)
    return pl.pallas_call(
        _out_proj_kernel,
        grid=(n // tm,),
        in_specs=[
            pl.BlockSpec((tm, d), lambda i: (i, 0)),
            pl.BlockSpec((tm, NSA_WIDTH), lambda i: (i, 0)),
            pl.BlockSpec((tm, SSM_WIDTH), lambda i: (i, 0)),
            pl.BlockSpec((tm, POOL_WIDTH), lambda i: (i, 0)),
            pl.BlockSpec((d, d), lambda i: (0, 0)),
            pl.BlockSpec((1, d), lambda i: (0, 0)),
        ],
        out_specs=pl.BlockSpec((tm, d), lambda i: (i, 0)),
        out_shape=jax.ShapeDtypeStruct((n, d), F32),
        compiler_params=_params(("parallel",)),
        name="out_proj",
    )(h, o_nsa, o_ssm, o_pool, w, g)


def _pool_kernel(halo_ref, x_ref, w_ref, scale_ref, o_ref, *, tiles_per_seq):
    tile = x_ref.shape[0]
    it = pl.program_id(0) % tiles_per_seq
    halo = jnp.where(it == 0, 0.0, halo_ref[...])
    ext = jnp.concatenate([halo, x_ref[...]], axis=0)
    pos = it * tile + lax.broadcasted_iota(jnp.int32, (tile, POOL_GROUP), 0)
    for gi, w in enumerate(POOL_WINDOWS):
        xg = ext[:, gi * POOL_GROUP:(gi + 1) * POOL_GROUP]
        s, k = xg, 1
        while k < w:
            s = s + pltpu.roll(s, k, axis=0)
            k *= 2
        cnt = jnp.minimum(pos + 1, w).astype(F32)
        dlt = s[POOL_HALO:] / cnt - xg[POOL_HALO:]
        y = _dot(dlt.astype(BF16), w_ref[gi])
        sl = slice(gi * POOL_GROUP, (gi + 1) * POOL_GROUP)
        o_ref[:, sl] = (y * scale_ref[:, sl]).astype(o_ref.dtype)


def _pool(proj_b, pool_w, pool_scale, seq_len):
    n = proj_b.shape[0]
    tile = min(POOL_TILE, seq_len)
    tiles_per_seq = seq_len // tile
    ratio = tile // POOL_HALO
    col = B_POOL // POOL_WIDTH
    return pl.pallas_call(
        functools.partial(_pool_kernel, tiles_per_seq=tiles_per_seq),
        grid=(n // tile,),
        in_specs=[
            pl.BlockSpec((POOL_HALO, POOL_WIDTH), lambda i: (jnp.maximum(i * ratio - 1, 0), col)),
            pl.BlockSpec((tile, POOL_WIDTH), lambda i: (i, col)),
            pl.BlockSpec((len(POOL_WINDOWS), POOL_GROUP, POOL_GROUP), lambda i: (0, 0, 0)),
            pl.BlockSpec((1, POOL_WIDTH), lambda i: (0, 0)),
        ],
        out_specs=pl.BlockSpec((tile, POOL_WIDTH), lambda i: (i, 0)),
        out_shape=jax.ShapeDtypeStruct((n, POOL_WIDTH), BF16),
        compiler_params=_params(("parallel",)),
        name="pool",
    )(proj_b, proj_b, pool_w, pool_scale)


def _ssd_kernel(halo_ref, xbc_ref, z_ref, dt_ref, convw_ref, convb_ref, dtb_ref, alog_ref,
                dskip_ref, normw_ref, o_ref, state_ref):
    L = xbc_ref.shape[0]
    c = pl.program_id(1)

    @pl.when(c == 0)
    def _():
        state_ref[...] = jnp.zeros_like(state_ref)

    halo = jnp.where(c == 0, 0.0, halo_ref[...])
    ext = jnp.concatenate([halo, xbc_ref[...]], axis=0)
    conv = convb_ref[...] + convw_ref[SSM_CONV - 1:SSM_CONV, :] * ext[CONV_HALO:]
    for k in range(1, SSM_CONV):
        conv += convw_ref[SSM_CONV - 1 - k:SSM_CONV - k, :] * pltpu.roll(ext, k, axis=0)[CONV_HALO:]
    act = _silu(conv)
    xs = act[:, :SSM_WIDTH]
    bm = act[:, SSM_WIDTH:SSM_WIDTH + SSM_GROUPS * SSM_STATE]
    cm = act[:, SSM_WIDTH + SSM_GROUPS * SSM_STATE:]

    lane = lax.broadcasted_iota(jnp.int32, (1, LANES), 1)
    xdt = dt_ref[...] + dtb_ref[...]
    dt = jnp.maximum(xdt, 0.0) + jnp.log1p(jnp.exp(-jnp.abs(xdt)))
    a_head = jnp.where(lane < SSM_HEADS, -jnp.exp(alog_ref[...]), 0.0)
    cs = a_head * dt
    row = lax.broadcasted_iota(jnp.int32, (L, LANES), 0)
    k = 1
    while k < L:
        cs = cs + jnp.where(row >= k, pltpu.roll(cs, k, axis=0), 0.0)
        k *= 2
    cs_t = cs.T

    er = lax.broadcasted_iota(jnp.int32, (LANES, SSM_WIDTH), 0)
    ec = lax.broadcasted_iota(jnp.int32, (LANES, SSM_WIDTH), 1)
    expand = jnp.where(er == ec // SSM_HEAD_DIM, 1.0, 0.0).astype(BF16)
    dtx = _dot_exact_rhs(dt, expand)
    csx = _dot_exact_rhs(cs, expand)
    cs_last = csx[L - 1:L, :]

    x_dt = xs * dtx
    x_end = (x_dt * jnp.exp(cs_last - csx)).astype(BF16)
    x_dt16 = x_dt.astype(BF16)
    ecs = jnp.exp(csx)
    li = lax.broadcasted_iota(jnp.int32, (L, L), 0)
    si = lax.broadcasted_iota(jnp.int32, (L, L), 1)
    gw = SSM_WIDTH // SSM_GROUPS
    hpg = SSM_HEADS // SSM_GROUPS
    lane_g = lax.broadcasted_iota(jnp.int32, (1, gw), 1)
    ys = []
    for g in range(SSM_GROUPS):
        bg = bm[:, g * SSM_STATE:(g + 1) * SSM_STATE]
        cg = cm[:, g * SSM_STATE:(g + 1) * SSM_STATE].astype(BF16)
        gs = slice(g * gw, (g + 1) * gw)
        cb = _dot_nt(cg, bg.astype(BF16))
        y_g = jnp.zeros((L, gw), F32)
        for hh in range(hpg):
            h = g * hpg + hh
            seg = cs[:, h:h + 1] - cs_t[h:h + 1, :]
            m_h = (cb * jnp.exp(jnp.where(li >= si, seg, NEG))).astype(BF16)
            x_h = jnp.where(lane_g // SSM_HEAD_DIM == hh, x_dt16[:, gs], jnp.zeros((), BF16))
            y_g += _dot(m_h, x_h)
        st = state_ref[g]
        y_g += _dot(cg, st.astype(BF16)) * ecs[:, gs]
        state_ref[g] = jnp.exp(cs_last[:, gs]) * st + _dot(bg.T.astype(BF16), x_end[:, gs])
        ys.append(y_g)
    y = jnp.concatenate(ys, axis=1) + dskip_ref[...] * xs
    y = y * _silu(z_ref[...])
    outs = []
    for g in range(SSM_GROUPS):
        outs.append(_rms(y[:, g * gw:(g + 1) * gw]))
    o_ref[...] = (jnp.concatenate(outs, axis=1) * normw_ref[...]).astype(o_ref.dtype)


def _ssd(proj_b, conv_w, conv_b, dt_bias, a_log, d_skip, norm_w, batch, seq_len):
    n = proj_b.shape[0]
    L = SSD_CHUNK
    nch = seq_len // L
    ratio = L // CONV_HALO
    return pl.pallas_call(
        _ssd_kernel,
        grid=(batch, nch),
        in_specs=[
            pl.BlockSpec((CONV_HALO, SSM_CONV_DIM),
                         lambda b, c: (jnp.maximum((b * nch + c) * ratio - 1, 0), B_XBC // SSM_CONV_DIM)),
            pl.BlockSpec((L, SSM_CONV_DIM), lambda b, c: (b * nch + c, B_XBC // SSM_CONV_DIM)),
            pl.BlockSpec((L, SSM_WIDTH), lambda b, c: (b * nch + c, B_Z // SSM_WIDTH)),
            pl.BlockSpec((L, LANES), lambda b, c: (b * nch + c, B_DT // LANES)),
            pl.BlockSpec((SSM_CONV, SSM_CONV_DIM), lambda b, c: (0, 0)),
            pl.BlockSpec((1, SSM_CONV_DIM), lambda b, c: (0, 0)),
            pl.BlockSpec((1, LANES), lambda b, c: (0, 0)),
            pl.BlockSpec((1, LANES), lambda b, c: (0, 0)),
            pl.BlockSpec((1, SSM_WIDTH), lambda b, c: (0, 0)),
            pl.BlockSpec((1, SSM_WIDTH), lambda b, c: (0, 0)),
        ],
        out_specs=pl.BlockSpec((L, SSM_WIDTH), lambda b, c: (b * nch + c, 0)),
        out_shape=jax.ShapeDtypeStruct((n, SSM_WIDTH), BF16),
        scratch_shapes=[pltpu.VMEM((SSM_GROUPS, SSM_STATE, SSM_WIDTH // SSM_GROUPS), F32)],
        compiler_params=_params(("parallel", "arbitrary")),
        name="ssd",
    )(proj_b, proj_b, proj_b, proj_b, conv_w, conv_b, dt_bias, a_log, d_skip, norm_w)


def _rope(x, cos, sin_signed):
    return x * cos + pltpu.roll(x, HEAD_DIM // 2, axis=1) * sin_signed


def _compress(src_ref, w1_ref, w2_ref, pe_ref):
    n_cmp = src_ref.shape[0] // CMP_STRIDE
    xs = [src_ref[pl.ds(p, n_cmp, stride=CMP_STRIDE), :] for p in range(CMP_STRIDE)]
    x = jnp.concatenate(xs, axis=1).astype(BF16)
    half = CMP_STRIDE * HEAD_DIM
    first = _dot(x, w1_ref[0:half, :])
    second = _dot(x, w1_ref[half:2 * half, :])
    pe = jnp.broadcast_to(pe_ref[...], (SUBLANES, 2 * half)).astype(BF16)
    pe_term = _dot(pe, w1_ref[...])[0:1, :]
    pre = first + pltpu.roll(second, n_cmp - 1, axis=0) + pe_term
    return _dot(_silu(pre).astype(BF16), w2_ref[...])


def _nsa_kernel(q_ref, gate_ref, kc_src_ref, vc_src_ref, ks_ref, vs_ref, kw_ref, vw_ref,
                cos_ref, sin_ref, w1k_ref, w2k_ref, w1v_ref, w2v_ref, pek_ref, pev_ref,
                o_ref,
                kc_scr, vct_scr, kx_scr, kwp_scr, vst_scr, vwt_scr, qx_scr, score_scr, rank_scr,
                m_scr, l_scr, acc_scr, oct_scr, owt_scr, s_scr, band_scr):
    tq = q_ref.shape[0]
    seq = ks_ref.shape[0]
    tk = min(SLC_KEY_TILE, seq)
    nh = NSA_HPG
    n_cmp, n_slc = seq // CMP_STRIDE, seq // SLC_LEN
    topk = min(SLC_TOPK, n_slc)
    qi = pl.program_id(2)
    q0 = qi * tq
    qk_scale = HEAD_DIM ** -0.5 * LOG2E

    @pl.when(qi == 0)
    def _():
        kc_scr[...] = _compress(kc_src_ref, w1k_ref, w2k_ref, pek_ref).astype(BF16)
        vct_scr[...] = _compress(vc_src_ref, w1v_ref, w2v_ref, pev_ref).T.astype(BF16)
        kwp_scr[0:WIN_LEN, :] = jnp.zeros((WIN_LEN, HEAD_DIM), BF16)
        for j in range(WIN_LEN // tq):
            vwt_scr[j] = jnp.zeros((HEAD_DIM, tq), BF16)
        kc_i = lax.broadcasted_iota(jnp.int32, (tq + WIN_LEN, 1), 0)
        qr_i = lax.broadcasted_iota(jnp.int32, (1, tq), 1)
        band_scr[...] = jnp.where((kc_i > qr_i) & (kc_i <= qr_i + WIN_LEN), 0.0, NEG)

        def prep_rows(i, carry):
            rows = pl.ds(pl.multiple_of(i * tq, tq), tq)
            prows = pl.ds(pl.multiple_of(WIN_LEN + i * tq, tq), tq)
            cos, sin = cos_ref[rows, :], sin_ref[rows, :]
            kx_scr[rows, 0:HEAD_DIM] = _rope(ks_ref[rows, :].astype(F32), cos, sin).astype(BF16)
            blk = (i * tq + lax.broadcasted_iota(jnp.int32, (tq, LANES), 0)) // SLC_LEN
            lane = lax.broadcasted_iota(jnp.int32, (tq, LANES), 1)
            kx_scr[rows, HEAD_DIM:2 * HEAD_DIM] = jnp.where(lane == blk, 1.0, 0.0).astype(BF16)
            kwp_scr[prows, :] = _rope(kw_ref[rows, :].astype(F32), cos, sin).astype(BF16)
            vst_scr[i] = vs_ref[rows, :].astype(F32).T.astype(BF16)
            vwt_scr[i + WIN_LEN // tq] = vw_ref[rows, :].astype(F32).T.astype(BF16)
            return carry

        lax.fori_loop(0, seq // tq, prep_rows, 0)

    qrows = pl.ds(pl.multiple_of(q0, tq), tq)
    qlane = lax.broadcasted_iota(jnp.int32, (1, tq), 1)
    tpos = q0 + qlane

    def q_rows(h):
        return slice(h * tq, (h + 1) * tq)

    cos_q, sin_q = cos_ref[qrows, :], sin_ref[qrows, :]
    for h in range(nh):
        hs = slice(h * HEAD_DIM, (h + 1) * HEAD_DIM)
        qx_scr[q_rows(h), 0:HEAD_DIM] = (_rope(q_ref[:, hs].astype(F32), cos_q, sin_q) * qk_scale).astype(BF16)

    def heads(x):
        return jnp.concatenate([x] * nh, axis=1)

    q_stack = jnp.concatenate([q_ref[:, h * HEAD_DIM:(h + 1) * HEAD_DIM] for h in range(nh)], axis=0)
    nblk = lax.broadcasted_iota(jnp.int32, (n_cmp, 1), 0)
    cmp_bias = jnp.where((nblk * CMP_STRIDE + (CMP_LEN - 1)) <= tpos, 0.0, NEG)
    any_visible = jnp.where(tpos >= CMP_LEN - 1, 1.0, 0.0)
    s = _dot_nt(kc_scr[...], q_stack) * qk_scale + heads(cmp_bias)
    e = jnp.exp2(s - jnp.max(s, axis=0, keepdims=True))
    p = e * (heads(any_visible) / jnp.sum(e, axis=0, keepdims=True))
    p_sum = p[:, 0:tq]
    for h in range(1, nh):
        p_sum += p[:, h * tq:(h + 1) * tq]
    oct_scr[...] = _dot(vct_scr[...], p.astype(BF16))

    wspan = tq + WIN_LEN
    wrows = pl.ds(pl.multiple_of(q0, tq), wspan)
    c = lax.broadcasted_iota(jnp.int32, (wspan, 1), 0)
    wb = band_scr[...] + jnp.where(c + q0 >= WIN_LEN, 0.0, NEG)
    s = _dot_nt(kwp_scr[wrows, :], qx_scr[:, 0:HEAD_DIM]) + heads(wb)
    p = jnp.exp2(s - jnp.max(s, axis=0, keepdims=True))
    pb = p.astype(BF16)
    o = _dot(vwt_scr[qi], pb[0:tq])
    for j in range(1, wspan // tq):
        o += _dot(vwt_scr[qi + j], pb[j * tq:(j + 1) * tq])
    owt_scr[...] = o / jnp.sum(p, axis=0, keepdims=True)

    nr = lax.broadcasted_iota(jnp.int32, (n_slc, n_cmp), 1) * CMP_STRIDE
    jr = lax.broadcasted_iota(jnp.int32, (n_slc, n_cmp), 0) * SLC_LEN
    overlap_t = jnp.where((nr < jr + SLC_LEN) & (nr + CMP_LEN > jr), 1.0, 0.0).astype(BF16)
    ps_hi = p_sum.astype(BF16)
    ps_lo = (p_sum - ps_hi.astype(F32)).astype(BF16)
    imp_t = _dot(overlap_t, ps_hi) + _dot(overlap_t, ps_lo)

    jblk = lax.broadcasted_iota(jnp.int32, (n_slc, tq), 0)
    qblk = (q0 + lax.broadcasted_iota(jnp.int32, (n_slc, tq), 1)) // SLC_LEN
    valid = jblk <= qblk
    forced = (jblk == 0) | (jblk == qblk) | (jblk == qblk - 1)
    score_scr[...] = jnp.where(valid, jnp.where(forced, 1e9, imp_t), -1e9)
    rank_scr[...] = jnp.zeros_like(rank_scr)
    ng = n_slc // SUBLANES
    g_last = ((q0 + tq - 1) // SLC_LEN) // SUBLANES
    row8 = lax.broadcasted_iota(jnp.int32, (SUBLANES, tq), 0)
    for gp in range(ng):
        @pl.when(gp <= g_last)
        def _():
            cnt = [jnp.zeros((SUBLANES, tq), F32) for _ in range(ng)]
            for jj in range(SUBLANES):
                jp = gp * SUBLANES + jj
                other = jnp.broadcast_to(score_scr[jp:jp + 1, :], (SUBLANES, tq))
                for g in range(ng):
                    sc = score_scr[g * SUBLANES:(g + 1) * SUBLANES, :]
                    if g > gp:
                        ahead = other >= sc
                    elif g < gp:
                        ahead = other > sc
                    else:
                        ahead = (other > sc) | ((other == sc) & (row8 > jj))
                    cnt[g] += jnp.where(ahead, 1.0, 0.0)
            for g in range(ng):
                rank_scr[g * SUBLANES:(g + 1) * SUBLANES, :] += cnt[g]

    sel_t = jnp.where(valid & (rank_scr[...] < topk), 1.0, 0.0).astype(BF16)
    sel_pad = jnp.concatenate([sel_t, jnp.zeros((LANES - n_slc, tq), BF16)], axis=0)
    ri = lax.broadcasted_iota(jnp.int32, (tq, tq), 0)
    ci = lax.broadcasted_iota(jnp.int32, (tq, tq), 1)
    eye = jnp.where(ri == ci, 1.0, 0.0).astype(BF16)
    sel = _dot_nt(eye, sel_pad)
    lane = lax.broadcasted_iota(jnp.int32, (tq, LANES), 1)
    sel_bias = jnp.where((lane < n_slc) & (sel < 0.5), NEG, 0.0).astype(BF16)

    for h in range(nh):
        qx_scr[q_rows(h), HEAD_DIM:2 * HEAD_DIM] = sel_bias
    m_scr[...] = jnp.full(m_scr.shape, NEG, F32)
    l_scr[...] = jnp.zeros_like(l_scr)
    acc_scr[...] = jnp.zeros_like(acc_scr)
    chunks = tk // tq

    def scores(kt):
        return _dot_nt(kx_scr[pl.ds(pl.multiple_of(kt * tk, tk), tk), :], qx_scr[...])

    def softmax_update(s, kt):
        m_prev = m_scr[...]
        m_new = jnp.maximum(m_prev, jnp.max(s, axis=0, keepdims=True))
        p = jnp.exp2(s - m_new)
        alpha = jnp.exp2(m_prev - m_new)
        l_scr[...] = alpha * l_scr[...] + jnp.sum(p, axis=0, keepdims=True)
        pb = p.astype(BF16)
        pv = _dot(vst_scr[kt * chunks], pb[0:tq])
        for j in range(1, chunks):
            pv += _dot(vst_scr[kt * chunks + j], pb[j * tq:(j + 1) * tq])
        acc_scr[...] = alpha * acc_scr[...] + pv
        m_scr[...] = m_new

    n_tiles = (q0 + tq + tk - 1) // tk
    s_scr[...] = scores(0)

    def slc_step(kt):
        s = s_scr[...]
        s_next = scores(kt + 1)
        softmax_update(s, kt)
        s_scr[...] = s_next

    def slc_pair(i, carry):
        slc_step(2 * i)
        slc_step(2 * i + 1)
        return carry

    n_steps = n_tiles - 1
    lax.fori_loop(0, n_steps // 2, slc_pair, 0)

    @pl.when(n_steps % 2 == 1)
    def _():
        slc_step(n_steps - 1)

    kpos = (n_tiles - 1) * tk + lax.broadcasted_iota(jnp.int32, (tk, 1), 0)
    softmax_update(s_scr[...] + heads(jnp.where(kpos <= tpos, 0.0, NEG)), n_tiles - 1)

    gates = _sigmoid(gate_ref[...]).T
    for h in range(nh):
        hs = slice(h * HEAD_DIM, (h + 1) * HEAD_DIM)
        ql = slice(h * tq, (h + 1) * tq)
        o = (gates[3 * h:3 * h + 1] * oct_scr[:, ql]
             + gates[3 * h + 1:3 * h + 2] * (acc_scr[:, ql] / l_scr[:, ql])
             + gates[3 * h + 2:3 * h + 3] * owt_scr[:, ql])
        o_ref[:, hs] = o.T.astype(o_ref.dtype)


def _nsa(proj_a, proj_b, cos, sin_signed, w1k, w2k, w1v, w2v, pe_k, pe_v, batch, seq_len):
    n = proj_a.shape[0]
    tq = min(ATT_TILE, seq_len)
    nq = seq_len // tq
    gw = NSA_HPG * HEAD_DIM
    n_cmp, n_slc = seq_len // CMP_STRIDE, seq_len // SLC_LEN
    assert n_slc <= LANES // 2 and n_slc % SUBLANES == 0 and seq_len % min(SLC_KEY_TILE, seq_len) == 0

    def kv_spec(col0):
        return pl.BlockSpec((seq_len, HEAD_DIM), lambda b, g, i: (b, col0 // HEAD_DIM + g))

    def full(shape):
        return pl.BlockSpec(shape, lambda b, g, i: (0,) * len(shape))

    return pl.pallas_call(
        _nsa_kernel,
        grid=(batch, NSA_GROUPS, nq),
        in_specs=[
            pl.BlockSpec((tq, gw), lambda b, g, i: (b * nq + i, A_Q // gw + g)),
            pl.BlockSpec((tq, LANES), lambda b, g, i: (b * nq + i, B_GATE // LANES + g)),
            kv_spec(B_KC), kv_spec(B_VC),
            kv_spec(A_KS), kv_spec(A_VS), kv_spec(A_KW), kv_spec(A_VW),
            full((seq_len, HEAD_DIM)), full((seq_len, HEAD_DIM)),
            full((CMP_LEN * HEAD_DIM, HEAD_DIM)), full((HEAD_DIM, HEAD_DIM)),
            full((CMP_LEN * HEAD_DIM, HEAD_DIM)), full((HEAD_DIM, HEAD_DIM)),
            full((1, CMP_LEN * HEAD_DIM)), full((1, CMP_LEN * HEAD_DIM)),
        ],
        out_specs=pl.BlockSpec((tq, gw), lambda b, g, i: (b * nq + i, g)),
        out_shape=jax.ShapeDtypeStruct((n, NSA_WIDTH), BF16),
        scratch_shapes=[
            pltpu.VMEM((n_cmp, HEAD_DIM), BF16),
            pltpu.VMEM((HEAD_DIM, n_cmp), BF16),
            pltpu.VMEM((seq_len, 2 * HEAD_DIM), BF16),
            pltpu.VMEM((seq_len + WIN_LEN, HEAD_DIM), BF16),
            pltpu.VMEM((seq_len // tq, HEAD_DIM, tq), BF16),
            pltpu.VMEM(((seq_len + WIN_LEN) // tq, HEAD_DIM, tq), BF16),
            pltpu.VMEM((NSA_HPG * tq, 2 * HEAD_DIM), BF16),
            pltpu.VMEM((n_slc, tq), F32),
            pltpu.VMEM((n_slc, tq), F32),
            pltpu.VMEM((1, NSA_HPG * tq), F32),
            pltpu.VMEM((1, NSA_HPG * tq), F32),
            pltpu.VMEM((HEAD_DIM, NSA_HPG * tq), F32),
            pltpu.VMEM((HEAD_DIM, NSA_HPG * tq), F32),
            pltpu.VMEM((HEAD_DIM, NSA_HPG * tq), F32),
            pltpu.VMEM((min(SLC_KEY_TILE, seq_len), NSA_HPG * tq), F32),
            pltpu.VMEM((tq + WIN_LEN, tq), F32),
        ],
        compiler_params=_params(("parallel", "parallel", "arbitrary")),
        name="nsa",
    )(proj_a, proj_b, proj_b, proj_b, proj_a, proj_a, proj_a, proj_a,
      cos, sin_signed, w1k, w2k, w1v, w2v, pe_k, pe_v)


def _rope_tables(seq_len):
    inv = ROPE_THETA ** (-jnp.arange(0, HEAD_DIM, 2, dtype=F32) / HEAD_DIM)
    ang = jnp.arange(seq_len, dtype=F32)[:, None] * inv[None, :]
    ang = jnp.concatenate([ang, ang], -1)
    sign = jnp.concatenate([-jnp.ones((HEAD_DIM // 2,), F32), jnp.ones((HEAD_DIM // 2,), F32)])
    return jnp.cos(ang), jnp.sin(ang) * sign


def _split_w_in(w_in):
    offs = [0]
    for wd in IN_WIDTHS:
        offs.append(offs[-1] + wd)
    (q, kc, vc, ks, vs, kw, vw, gates, z, xbc, dt, pv) = [
        w_in[:, offs[i]:offs[i + 1]] for i in range(len(IN_WIDTHS))]
    d = w_in.shape[0]

    def pad_to(w, width):
        return jnp.pad(w, ((0, 0), (0, width - w.shape[1])))

    per_group = 3 * NSA_HPG
    w_a = jnp.concatenate([q, ks, vs, kw, vw], axis=1)
    w_b = jnp.concatenate(
        [xbc, z, pv, pad_to(dt, LANES),
         pad_to(gates[:, :per_group], LANES), pad_to(gates[:, per_group:], LANES),
         kc, vc, jnp.zeros((d, B_WIDTH - B_VC - KV_WIDTH), w_in.dtype)], axis=1)
    return w_a.astype(BF16), w_b.astype(BF16)


def _pad_lanes(v):
    return jnp.pad(v, (0, LANES - v.shape[0]))[None, :]


def kernel(x, ffn1_norm_pre, ffn1_norm_post, ffn1_w_gate, ffn1_w_up, ffn1_w_down, mix_norm_pre, mix_norm_post, w_in, cmp_pe_k, cmp_pe_v, cmp_k_w1, cmp_k_w2, cmp_v_w1, cmp_v_w2, ssm_conv_w, ssm_conv_b, ssm_dt_bias, ssm_a_log, ssm_d, ssm_norm, pool_w, pool_scale, w_out, ffn2_norm_pre, ffn2_norm_post, ffn2_w_gate, ffn2_w_up, ffn2_w_down):
    batch, seq_len, d = x.shape
    depth = w_in.shape[0]
    cos, sin_signed = _rope_tables(seq_len)
    h = x.reshape(batch * seq_len, d)
    for i in range(depth):
        h = _ffn(h, ffn1_norm_pre[i][None], ffn1_norm_post[i][None],
                 *_ffn_weights_bf16(ffn1_w_gate, ffn1_w_up, ffn1_w_down, i))
        proj_a, proj_b = _in_proj(h, mix_norm_pre[i][None], *_split_w_in(w_in[i]))
        o_nsa = _nsa(proj_a, proj_b, cos, sin_signed,
                     cmp_k_w1[i].astype(BF16), cmp_k_w2[i].astype(BF16),
                     cmp_v_w1[i].astype(BF16), cmp_v_w2[i].astype(BF16),
                     cmp_pe_k[i].reshape(1, -1), cmp_pe_v[i].reshape(1, -1), batch, seq_len)
        o_ssm = _ssd(proj_b, ssm_conv_w[i], ssm_conv_b[i][None], _pad_lanes(ssm_dt_bias[i]),
                     _pad_lanes(ssm_a_log[i]), jnp.repeat(ssm_d[i], SSM_HEAD_DIM)[None],
                     ssm_norm[i][None], batch, seq_len)
        o_pool = _pool(proj_b, pool_w[i].astype(BF16), pool_scale[i][None], seq_len)
        h = _out_proj(h, o_nsa, o_ssm, o_pool, w_out[i].astype(BF16), mix_norm_post[i][None])
        h = _ffn(h, ffn2_norm_pre[i][None], ffn2_norm_post[i][None],
                 *_ffn_weights_bf16(ffn2_w_gate, ffn2_w_up, ffn2_w_down, i))
    return h.reshape(batch, seq_len, d)
```

```python
import functools
import math

import jax
import jax.numpy as jnp
from jax import lax
from jax.experimental import pallas as pl
from jax.experimental.pallas import tpu as pltpu

F32 = jnp.float32
BF16 = jnp.bfloat16

D_MODEL = 2048
HEAD_DIM = 128
NSA_WIDTH = 1024
NSA_HEADS = 8
NSA_GROUPS = 2
NSA_HPG = NSA_HEADS // NSA_GROUPS
KV_WIDTH = NSA_GROUPS * HEAD_DIM
CMP_LEN = 32
CMP_STRIDE = 16
SLC_LEN = 64
SLC_TOPK = 16
WIN_LEN = 512
ROPE_THETA = 10000.0
SSM_WIDTH = 512
SSM_HEAD_DIM = 64
SSM_HEADS = 8
SSM_GROUPS = 2
SSM_STATE = 128
SSM_CONV = 4
SSM_CONV_DIM = SSM_WIDTH + 2 * SSM_GROUPS * SSM_STATE
POOL_WIDTH = 512
POOL_WINDOWS = (2, 4, 8, 16)
POOL_GROUP = POOL_WIDTH // len(POOL_WINDOWS)
D_FF = 5632
FFN_RESID = 0.5
RMS_EPS = 1e-6
NEG = -1e30
LOG2E = math.log2(math.e)
IN_WIDTHS = (NSA_WIDTH, KV_WIDTH, KV_WIDTH, KV_WIDTH, KV_WIDTH, KV_WIDTH, KV_WIDTH,
             3 * NSA_HEADS, SSM_WIDTH, SSM_CONV_DIM, SSM_HEADS, POOL_WIDTH)

LANES = 128
SUBLANES = 8
VMEM_LIMIT_BYTES = 60000 * 1024

A_Q, A_KS, A_VS, A_KW, A_VW = 0, 1024, 1280, 1536, 1792
A_WIDTH = 2048
B_XBC, B_Z, B_POOL, B_DT, B_GATE, B_KC, B_VC = 0, 1024, 1536, 2048, 2176, 2432, 2688
B_WIDTH = 3072

FFN_ROW_TILE = 1024
FFN_ROW_SPLIT = 2
FFN_TILE = 512
CAST_STEPS = 8
ROW_TILE = 512
PROJ_B_STEPS = 1
ATT_TILE = 256
SLC_KEY_TILE = 512
SSD_CHUNK = 128
SSD_STEP_ROWS = 256
POOL_TILE = 512
POOL_HALO = 16
CONV_HALO = 8


def _params(semantics):
    return pltpu.CompilerParams(dimension_semantics=semantics, vmem_limit_bytes=VMEM_LIMIT_BYTES)


def _sigmoid(x):
    return 1.0 / (1.0 + jnp.exp(-x))


def _silu(x):
    return x * _sigmoid(x)


def _rms(x):
    return x * lax.rsqrt(jnp.mean(x * x, axis=-1, keepdims=True) + RMS_EPS)


def _dot(a, b):
    return jnp.dot(a, b, preferred_element_type=F32)


def _dot_nt(a, b):
    return lax.dot_general(a, b, (((1,), (1,)), ((), ())), preferred_element_type=F32)


def _split3(x):
    hi = x.astype(BF16)
    r = x - hi.astype(F32)
    mid = r.astype(BF16)
    lo = (r - mid.astype(F32)).astype(BF16)
    return hi, mid, lo


def _dot_exact_rhs(x, sel):
    hi, mid, lo = _split3(x)
    return _dot(hi, sel) + _dot(mid, sel) + _dot(lo, sel)


def _ffn_kernel(x_ref, gpre_ref, gpost_ref, wg_ref, wu_ref, wd_ref, o_ref, xn_ref):
    j = pl.program_id(1)
    last = pl.num_programs(1) - 1

    def step(first, final):
        part = x_ref.shape[0] // FFN_ROW_SPLIT
        for r in range(FFN_ROW_SPLIT):
            rows = slice(r * part, (r + 1) * part)
            if first:
                xn = (_rms(x_ref[rows, :]) * gpre_ref[...]).astype(BF16)
                xn_ref[rows, :] = xn
            else:
                xn = xn_ref[rows, :]
            g = _dot(xn, wg_ref[...])
            u = _dot(xn, wu_ref[...])
            a = (_silu(g) * u).astype(BF16)
            acc = _dot(a, wd_ref[...])
            if not first:
                acc += o_ref[rows, :]
            if final:
                o_ref[rows, :] = x_ref[rows, :] + FFN_RESID * (_rms(acc) * gpost_ref[...])
            else:
                o_ref[rows, :] = acc

    pl.when(j == 0)(functools.partial(step, True, False))
    pl.when((j > 0) & (j < last))(functools.partial(step, False, False))
    pl.when(j == last)(functools.partial(step, False, True))


def _ffn(x, gpre, gpost, wg, wu, wd):
    n, d = x.shape
    dff = wg.shape[1]
    tm, tf = min(FFN_ROW_TILE, n), FFN_TILE
    return pl.pallas_call(
        _ffn_kernel,
        grid=(n // tm, dff // tf),
        in_specs=[
            pl.BlockSpec((tm, d), lambda i, j: (i, 0)),
            pl.BlockSpec((1, d), lambda i, j: (0, 0)),
            pl.BlockSpec((1, d), lambda i, j: (0, 0)),
            pl.BlockSpec((d, tf), lambda i, j: (0, j)),
            pl.BlockSpec((d, tf), lambda i, j: (0, j)),
            pl.BlockSpec((tf, d), lambda i, j: (j, 0)),
        ],
        out_specs=pl.BlockSpec((tm, d), lambda i, j: (i, 0)),
        out_shape=jax.ShapeDtypeStruct((n, d), F32),
        scratch_shapes=[pltpu.VMEM((tm, d), BF16)],
        compiler_params=_params(("parallel", "arbitrary")),
        name="ffn",
    )(x, gpre, gpost, wg, wu, wd)


def _cast3_kernel(a_ref, b_ref, c_ref, oa_ref, ob_ref, oc_ref):
    oa_ref[...] = a_ref[...].astype(oa_ref.dtype)
    ob_ref[...] = b_ref[...].astype(ob_ref.dtype)
    oc_ref[...] = c_ref[...].astype(oc_ref.dtype)


def _ffn_weights_bf16(wg, wu, wd, layer):
    steps = CAST_STEPS

    def spec_in(w):
        return pl.BlockSpec((None, w.shape[1] // steps, w.shape[2]), lambda r: (layer, r, 0))

    def spec_out(w):
        return pl.BlockSpec((w.shape[1] // steps, w.shape[2]), lambda r: (r, 0))

    ws = (wg, wu, wd)
    return pl.pallas_call(
        _cast3_kernel,
        grid=(steps,),
        in_specs=[spec_in(w) for w in ws],
        out_specs=[spec_out(w) for w in ws],
        out_shape=[jax.ShapeDtypeStruct(w.shape[1:], BF16) for w in ws],
        compiler_params=_params(("parallel",)),
        name="ffn_weights_bf16",
    )(*ws)


def _in_proj_kernel(x_ref, g_ref, wa_ref, wb_ref, oa_ref, ob_ref, xn_ref):
    j = pl.program_id(1)
    slab = ob_ref.shape[1]

    @pl.when(j == 0)
    def _():
        xn = (_rms(x_ref[...]) * g_ref[...]).astype(BF16)
        xn_ref[...] = xn
        oa_ref[...] = _dot(xn, wa_ref[...]).astype(oa_ref.dtype)

    for s in range(PROJ_B_STEPS):
        @pl.when(j == s + 1)
        def _():
            ob_ref[...] = _dot(xn_ref[...], wb_ref[:, s * slab:(s + 1) * slab])


def _in_proj(x, g, w_a, w_b):
    n, d = x.shape
    tm = min(ROW_TILE, n)
    slab = B_WIDTH // PROJ_B_STEPS
    return pl.pallas_call(
        _in_proj_kernel,
        grid=(n // tm, 1 + PROJ_B_STEPS),
        in_specs=[
            pl.BlockSpec((tm, d), lambda i, j: (i, 0)),
            pl.BlockSpec((1, d), lambda i, j: (0, 0)),
            pl.BlockSpec((d, A_WIDTH), lambda i, j: (0, 0), pipeline_mode=pl.Buffered(1)),
            pl.BlockSpec((d, B_WIDTH), lambda i, j: (0, 0), pipeline_mode=pl.Buffered(1)),
        ],
        out_specs=[
            pl.BlockSpec((tm, A_WIDTH), lambda i, j: (i, 0)),
            pl.BlockSpec((tm, slab), lambda i, j: (i, jnp.maximum(j - 1, 0))),
        ],
        out_shape=[jax.ShapeDtypeStruct((n, A_WIDTH), BF16),
                   jax.ShapeDtypeStruct((n, B_WIDTH), F32)],
        scratch_shapes=[pltpu.VMEM((tm, d), BF16)],
        compiler_params=_params(("parallel", "arbitrary")),
        name="in_proj",
    )(x, g, w_a, w_b)


def _out_proj_kernel(h_ref, a_ref, s_ref, p_ref, w_ref, g_ref, o_ref):
    m = _dot(a_ref[...], w_ref[0:NSA_WIDTH, :])
    m += _dot(s_ref[...], w_ref[NSA_WIDTH:NSA_WIDTH + SSM_WIDTH, :])
    m += _dot(p_ref[...], w_ref[NSA_WIDTH + SSM_WIDTH:, :])
    o_ref[...] = h_ref[...] + _rms(m) * g_ref[...]


def _out_proj(h, o_nsa, o_ssm, o_pool, w, g):
    n, d = h.shape
    tm = min(ROW_TILE, n)
    return pl.pallas_call(
        _out_proj_kernel,
        grid=(n // tm,),
        in_specs=[
            pl.BlockSpec((tm, d), lambda i: (i, 0)),
            pl.BlockSpec((tm, NSA_WIDTH), lambda i: (i, 0)),
            pl.BlockSpec((tm, SSM_WIDTH), lambda i: (i, 0)),
            pl.BlockSpec((tm, POOL_WIDTH), lambda i: (i, 0)),
            pl.BlockSpec((d, d), lambda i: (0, 0)),
            pl.BlockSpec((1, d), lambda i: (0, 0)),
        ],
        out_specs=pl.BlockSpec((tm, d), lambda i: (i, 0)),
        out_shape=jax.ShapeDtypeStruct((n, d), F32),
        compiler_params=_params(("parallel",)),
        name="out_proj",
    )(h, o_nsa, o_ssm, o_pool, w, g)


def _pool_kernel(halo_ref, x_ref, w_ref, scale_ref, o_ref, *, tiles_per_seq):
    tile = x_ref.shape[0]
    it = pl.program_id(0) % tiles_per_seq
    halo = jnp.where(it == 0, 0.0, halo_ref[...])
    ext = jnp.concatenate([halo, x_ref[...]], axis=0)
    pos = it * tile + lax.broadcasted_iota(jnp.int32, (tile, POOL_GROUP), 0)
    for gi, w in enumerate(POOL_WINDOWS):
        xg = ext[:, gi * POOL_GROUP:(gi + 1) * POOL_GROUP]
        s, k = xg, 1
        while k < w:
            s = s + pltpu.roll(s, k, axis=0)
            k *= 2
        cnt = jnp.minimum(pos + 1, w).astype(F32)
        dlt = s[POOL_HALO:] / cnt - xg[POOL_HALO:]
        y = _dot(dlt.astype(BF16), w_ref[gi])
        sl = slice(gi * POOL_GROUP, (gi + 1) * POOL_GROUP)
        o_ref[:, sl] = (y * scale_ref[:, sl]).astype(o_ref.dtype)


def _pool(proj_b, pool_w, pool_scale, seq_len):
    n = proj_b.shape[0]
    tile = min(POOL_TILE, seq_len)
    tiles_per_seq = seq_len // tile
    ratio = tile // POOL_HALO
    col = B_POOL // POOL_WIDTH
    return pl.pallas_call(
        functools.partial(_pool_kernel, tiles_per_seq=tiles_per_seq),
        grid=(n // tile,),
        in_specs=[
            pl.BlockSpec((POOL_HALO, POOL_WIDTH), lambda i: (jnp.maximum(i * ratio - 1, 0), col)),
            pl.BlockSpec((tile, POOL_WIDTH), lambda i: (i, col)),
            pl.BlockSpec((len(POOL_WINDOWS), POOL_GROUP, POOL_GROUP), lambda i: (0, 0, 0)),
            pl.BlockSpec((1, POOL_WIDTH), lambda i: (0, 0)),
        ],
        out_specs=pl.BlockSpec((tile, POOL_WIDTH), lambda i: (i, 0)),
        out_shape=jax.ShapeDtypeStruct((n, POOL_WIDTH), BF16),
        compiler_params=_params(("parallel",)),
        name="pool",
    )(proj_b, proj_b, pool_w, pool_scale)


def _ssd_kernel(halo_ref, xbc_ref, z_ref, dt_ref, convw_ref, convb_ref, dtb_ref, alog_ref,
                dskip_ref, normw_ref, o_ref, state_ref):
    c = pl.program_id(1)

    @pl.when(c == 0)
    def _():
        state_ref[...] = jnp.zeros_like(state_ref)

    for r in range(xbc_ref.shape[0] // SSD_CHUNK):
        rows = slice(r * SSD_CHUNK, (r + 1) * SSD_CHUNK)
        if r == 0:
            halo = jnp.where(c == 0, 0.0, halo_ref[...])
        else:
            halo = xbc_ref[r * SSD_CHUNK - CONV_HALO:r * SSD_CHUNK, :]
        _ssd_chunk(halo, xbc_ref[rows, :], z_ref[rows, :], dt_ref[rows, :], convw_ref, convb_ref, dtb_ref,
                   alog_ref, dskip_ref, normw_ref, o_ref.at[rows, :], state_ref)


def _ssd_chunk(halo, xbc, z, dt_raw, convw_ref, convb_ref, dtb_ref, alog_ref, dskip_ref, normw_ref,
               o_ref, state_ref):
    L = xbc.shape[0]

    ext = jnp.concatenate([halo, xbc], axis=0)
    conv = convb_ref[...] + convw_ref[SSM_CONV - 1:SSM_CONV, :] * xbc
    for k in range(1, SSM_CONV):
        conv += convw_ref[SSM_CONV - 1 - k:SSM_CONV - k, :] * pltpu.roll(ext, k, axis=0)[CONV_HALO:]
    act = _silu(conv)
    xs = act[:, :SSM_WIDTH]
    bm = act[:, SSM_WIDTH:SSM_WIDTH + SSM_GROUPS * SSM_STATE]
    cm = act[:, SSM_WIDTH + SSM_GROUPS * SSM_STATE:]

    lane = lax.broadcasted_iota(jnp.int32, (1, LANES), 1)
    xdt = dt_raw + dtb_ref[...]
    dt = jnp.maximum(xdt, 0.0) + jnp.log1p(jnp.exp(-jnp.abs(xdt)))
    a_head = jnp.where(lane < SSM_HEADS, -jnp.exp(alog_ref[...]), 0.0)
    cs = a_head * dt
    row = lax.broadcasted_iota(jnp.int32, (L, LANES), 0)
    k = 1
    while k < L:
        cs = cs + jnp.where(row >= k, pltpu.roll(cs, k, axis=0), 0.0)
        k *= 2
    cs_t = cs.T

    er = lax.broadcasted_iota(jnp.int32, (LANES, SSM_WIDTH), 0)
    ec = lax.broadcasted_iota(jnp.int32, (LANES, SSM_WIDTH), 1)
    expand = jnp.where(er == ec // SSM_HEAD_DIM, 1.0, 0.0).astype(BF16)
    dtx = _dot_exact_rhs(dt, expand)
    csx = _dot_exact_rhs(cs, expand)
    cs_last = csx[L - 1:L, :]

    x_dt = xs * dtx
    x_end = (x_dt * jnp.exp(cs_last - csx)).astype(BF16)
    x_dt16 = x_dt.astype(BF16)
    ecs = jnp.exp(csx)
    li = lax.broadcasted_iota(jnp.int32, (L, L), 0)
    si = lax.broadcasted_iota(jnp.int32, (L, L), 1)
    gw = SSM_WIDTH // SSM_GROUPS
    hpg = SSM_HEADS // SSM_GROUPS
    lane_g = lax.broadcasted_iota(jnp.int32, (1, gw), 1)
    ys = []
    for g in range(SSM_GROUPS):
        bg = bm[:, g * SSM_STATE:(g + 1) * SSM_STATE]
        cg = cm[:, g * SSM_STATE:(g + 1) * SSM_STATE].astype(BF16)
        gs = slice(g * gw, (g + 1) * gw)
        cb = _dot_nt(cg, bg.astype(BF16))
        y_g = jnp.zeros((L, gw), F32)
        for hh in range(hpg):
            h = g * hpg + hh
            seg = cs[:, h:h + 1] - cs_t[h:h + 1, :]
            m_h = (cb * jnp.exp(jnp.where(li >= si, seg, NEG))).astype(BF16)
            x_h = jnp.where(lane_g // SSM_HEAD_DIM == hh, x_dt16[:, gs], jnp.zeros((), BF16))
            y_g += _dot(m_h, x_h)
        st = state_ref[g]
        y_g += _dot(cg, st.astype(BF16)) * ecs[:, gs]
        state_ref[g] = jnp.exp(cs_last[:, gs]) * st + _dot(bg.T.astype(BF16), x_end[:, gs])
        ys.append(y_g)
    y = jnp.concatenate(ys, axis=1) + dskip_ref[...] * xs
    y = y * _silu(z)
    outs = []
    for g in range(SSM_GROUPS):
        outs.append(_rms(y[:, g * gw:(g + 1) * gw]))
    o_ref[...] = (jnp.concatenate(outs, axis=1) * normw_ref[...]).astype(o_ref.dtype)


def _ssd(proj_b, conv_w, conv_b, dt_bias, a_log, d_skip, norm_w, batch, seq_len):
    n = proj_b.shape[0]
    L = min(SSD_STEP_ROWS, seq_len)
    nch = seq_len // L
    ratio = L // CONV_HALO
    return pl.pallas_call(
        _ssd_kernel,
        grid=(batch, nch),
        in_specs=[
            pl.BlockSpec((CONV_HALO, SSM_CONV_DIM),
                         lambda b, c: (jnp.maximum((b * nch + c) * ratio - 1, 0), B_XBC // SSM_CONV_DIM)),
            pl.BlockSpec((L, SSM_CONV_DIM), lambda b, c: (b * nch + c, B_XBC // SSM_CONV_DIM)),
            pl.BlockSpec((L, SSM_WIDTH), lambda b, c: (b * nch + c, B_Z // SSM_WIDTH)),
            pl.BlockSpec((L, LANES), lambda b, c: (b * nch + c, B_DT // LANES)),
            pl.BlockSpec((SSM_CONV, SSM_CONV_DIM), lambda b, c: (0, 0)),
            pl.BlockSpec((1, SSM_CONV_DIM), lambda b, c: (0, 0)),
            pl.BlockSpec((1, LANES), lambda b, c: (0, 0)),
            pl.BlockSpec((1, LANES), lambda b, c: (0, 0)),
            pl.BlockSpec((1, SSM_WIDTH), lambda b, c: (0, 0)),
            pl.BlockSpec((1, SSM_WIDTH), lambda b, c: (0, 0)),
        ],
        out_specs=pl.BlockSpec((L, SSM_WIDTH), lambda b, c: (b * nch + c, 0)),
        out_shape=jax.ShapeDtypeStruct((n, SSM_WIDTH), BF16),
        scratch_shapes=[pltpu.VMEM((SSM_GROUPS, SSM_STATE, SSM_WIDTH // SSM_GROUPS), F32)],
        compiler_params=_params(("parallel", "arbitrary")),
        name="ssd",
    )(proj_b, proj_b, proj_b, proj_b, conv_w, conv_b, dt_bias, a_log, d_skip, norm_w)


def _rope(x, cos, sin_signed):
    return x * cos + pltpu.roll(x, HEAD_DIM // 2, axis=1) * sin_signed


def _compress(src_ref, w1_ref, w2_ref, pe_ref):
    n_cmp = src_ref.shape[0] // CMP_STRIDE
    xs = [src_ref[pl.ds(p, n_cmp, stride=CMP_STRIDE), :] for p in range(CMP_STRIDE)]
    x = jnp.concatenate(xs, axis=1).astype(BF16)
    half = CMP_STRIDE * HEAD_DIM
    first = _dot(x, w1_ref[0:half, :])
    second = _dot(x, w1_ref[half:2 * half, :])
    pe = jnp.broadcast_to(pe_ref[...], (SUBLANES, 2 * half)).astype(BF16)
    pe_term = _dot(pe, w1_ref[...])[0:1, :]
    pre = first + pltpu.roll(second, n_cmp - 1, axis=0) + pe_term
    return _dot(_silu(pre).astype(BF16), w2_ref[...])


def _nsa_kernel(q_ref, gate_ref, kc_src_ref, vc_src_ref, ks_ref, vs_ref, kw_ref, vw_ref,
                cos_ref, sin_ref, w1k_ref, w2k_ref, w1v_ref, w2v_ref, pek_ref, pev_ref,
                o_ref,
                kc_scr, vct_scr, kx_scr, kwp_scr, vst_scr, vwt_scr, qx_scr, score_scr, rank_scr,
                m_scr, l_scr, acc_scr, oct_scr, owt_scr, s_scr, band_scr):
    tq = q_ref.shape[0]
    seq = ks_ref.shape[0]
    tk = min(SLC_KEY_TILE, seq)
    nh = NSA_HPG
    n_cmp, n_slc = seq // CMP_STRIDE, seq // SLC_LEN
    topk = min(SLC_TOPK, n_slc)
    qi = pl.program_id(2)
    q0 = qi * tq
    qk_scale = HEAD_DIM ** -0.5 * LOG2E

    @pl.when(qi == 0)
    def _():
        kc_scr[...] = _compress(kc_src_ref, w1k_ref, w2k_ref, pek_ref).astype(BF16)
        vct_scr[...] = _compress(vc_src_ref, w1v_ref, w2v_ref, pev_ref).T.astype(BF16)
        kwp_scr[0:WIN_LEN, :] = jnp.zeros((WIN_LEN, HEAD_DIM), BF16)
        for j in range(WIN_LEN // tq):
            vwt_scr[j] = jnp.zeros((HEAD_DIM, tq), BF16)
        kc_i = lax.broadcasted_iota(jnp.int32, (tq + WIN_LEN, 1), 0)
        qr_i = lax.broadcasted_iota(jnp.int32, (1, tq), 1)
        band_scr[...] = jnp.where((kc_i > qr_i) & (kc_i <= qr_i + WIN_LEN), 0.0, NEG)

        def prep_rows(i, carry):
            rows = pl.ds(pl.multiple_of(i * tq, tq), tq)
            prows = pl.ds(pl.multiple_of(WIN_LEN + i * tq, tq), tq)
            cos, sin = cos_ref[rows, :], sin_ref[rows, :]
            kx_scr[rows, 0:HEAD_DIM] = _rope(ks_ref[rows, :].astype(F32), cos, sin).astype(BF16)
            blk = (i * tq + lax.broadcasted_iota(jnp.int32, (tq, LANES), 0)) // SLC_LEN
            lane = lax.broadcasted_iota(jnp.int32, (tq, LANES), 1)
            kx_scr[rows, HEAD_DIM:2 * HEAD_DIM] = jnp.where(lane == blk, 1.0, 0.0).astype(BF16)
            kwp_scr[prows, :] = _rope(kw_ref[rows, :].astype(F32), cos, sin).astype(BF16)
            vst_scr[i] = vs_ref[rows, :].astype(F32).T.astype(BF16)
            vwt_scr[i + WIN_LEN // tq] = vw_ref[rows, :].astype(F32).T.astype(BF16)
            return carry

        lax.fori_loop(0, seq // tq, prep_rows, 0)

    qrows = pl.ds(pl.multiple_of(q0, tq), tq)
    qlane = lax.broadcasted_iota(jnp.int32, (1, tq), 1)
    tpos = q0 + qlane

    def q_rows(h):
        return slice(h * tq, (h + 1) * tq)

    cos_q, sin_q = cos_ref[qrows, :], sin_ref[qrows, :]
    for h in range(nh):
        hs = slice(h * HEAD_DIM, (h + 1) * HEAD_DIM)
        qx_scr[q_rows(h), 0:HEAD_DIM] = (_rope(q_ref[:, hs].astype(F32), cos_q, sin_q) * qk_scale).astype(BF16)

    def heads(x):
        return jnp.concatenate([x] * nh, axis=1)

    q_stack = jnp.concatenate([q_ref[:, h * HEAD_DIM:(h + 1) * HEAD_DIM] for h in range(nh)], axis=0)
    nblk = lax.broadcasted_iota(jnp.int32, (n_cmp, 1), 0)
    cmp_bias = jnp.where((nblk * CMP_STRIDE + (CMP_LEN - 1)) <= tpos, 0.0, NEG)
    any_visible = jnp.where(tpos >= CMP_LEN - 1, 1.0, 0.0)
    s = _dot_nt(kc_scr[...], q_stack) * qk_scale + heads(cmp_bias)
    e = jnp.exp2(s - jnp.max(s, axis=0, keepdims=True))
    p = e * (heads(any_visible) / jnp.sum(e, axis=0, keepdims=True))
    p_sum = p[:, 0:tq]
    for h in range(1, nh):
        p_sum += p[:, h * tq:(h + 1) * tq]
    oct_scr[...] = _dot(vct_scr[...], p.astype(BF16))

    wspan = tq + WIN_LEN
    wrows = pl.ds(pl.multiple_of(q0, tq), wspan)
    c = lax.broadcasted_iota(jnp.int32, (wspan, 1), 0)
    wb = band_scr[...] + jnp.where(c + q0 >= WIN_LEN, 0.0, NEG)
    s = _dot_nt(kwp_scr[wrows, :], qx_scr[:, 0:HEAD_DIM]) + heads(wb)
    p = jnp.exp2(s - jnp.max(s, axis=0, keepdims=True))
    pb = p.astype(BF16)
    o = _dot(vwt_scr[qi], pb[0:tq])
    for j in range(1, wspan // tq):
        o += _dot(vwt_scr[qi + j], pb[j * tq:(j + 1) * tq])
    owt_scr[...] = o / jnp.sum(p, axis=0, keepdims=True)

    nr = lax.broadcasted_iota(jnp.int32, (n_slc, n_cmp), 1) * CMP_STRIDE
    jr = lax.broadcasted_iota(jnp.int32, (n_slc, n_cmp), 0) * SLC_LEN
    overlap_t = jnp.where((nr < jr + SLC_LEN) & (nr + CMP_LEN > jr), 1.0, 0.0).astype(BF16)
    ps_hi = p_sum.astype(BF16)
    ps_lo = (p_sum - ps_hi.astype(F32)).astype(BF16)
    imp_t = _dot(overlap_t, ps_hi) + _dot(overlap_t, ps_lo)

    jblk = lax.broadcasted_iota(jnp.int32, (n_slc, tq), 0)
    qblk = (q0 + lax.broadcasted_iota(jnp.int32, (n_slc, tq), 1)) // SLC_LEN
    valid = jblk <= qblk
    forced = (jblk == 0) | (jblk == qblk) | (jblk == qblk - 1)
    score_scr[...] = jnp.where(valid, jnp.where(forced, 1e9, imp_t), -1e9)
    rank_scr[...] = jnp.zeros_like(rank_scr)
    ng = n_slc // SUBLANES
    g_last = ((q0 + tq - 1) // SLC_LEN) // SUBLANES
    row8 = lax.broadcasted_iota(jnp.int32, (SUBLANES, tq), 0)
    for gp in range(ng):
        @pl.when(gp <= g_last)
        def _():
            cnt = [jnp.zeros((SUBLANES, tq), F32) for _ in range(ng)]
            for jj in range(SUBLANES):
                jp = gp * SUBLANES + jj
                other = jnp.broadcast_to(score_scr[jp:jp + 1, :], (SUBLANES, tq))
                for g in range(ng):
                    sc = score_scr[g * SUBLANES:(g + 1) * SUBLANES, :]
                    if g > gp:
                        ahead = other >= sc
                    elif g < gp:
                        ahead = other > sc
                    else:
                        ahead = (other > sc) | ((other == sc) & (row8 > jj))
                    cnt[g] += jnp.where(ahead, 1.0, 0.0)
            for g in range(ng):
                rank_scr[g * SUBLANES:(g + 1) * SUBLANES, :] += cnt[g]

    sel_t = jnp.where(valid & (rank_scr[...] < topk), 1.0, 0.0).astype(BF16)
    sel_pad = jnp.concatenate([sel_t, jnp.zeros((LANES - n_slc, tq), BF16)], axis=0)
    ri = lax.broadcasted_iota(jnp.int32, (tq, tq), 0)
    ci = lax.broadcasted_iota(jnp.int32, (tq, tq), 1)
    eye = jnp.where(ri == ci, 1.0, 0.0).astype(BF16)
    sel = _dot_nt(eye, sel_pad)
    lane = lax.broadcasted_iota(jnp.int32, (tq, LANES), 1)
    sel_bias = jnp.where((lane < n_slc) & (sel < 0.5), NEG, 0.0).astype(BF16)

    for h in range(nh):
        qx_scr[q_rows(h), HEAD_DIM:2 * HEAD_DIM] = sel_bias
    m_scr[...] = jnp.full(m_scr.shape, NEG, F32)
    l_scr[...] = jnp.zeros_like(l_scr)
    acc_scr[...] = jnp.zeros_like(acc_scr)
    chunks = tk // tq

    def scores(kt):
        return _dot_nt(kx_scr[pl.ds(pl.multiple_of(kt * tk, tk), tk), :], qx_scr[...])

    def softmax_update(s, kt):
        m_prev = m_scr[...]
        m_new = jnp.maximum(m_prev, jnp.max(s, axis=0, keepdims=True))
        p = jnp.exp2(s - m_new)
        alpha = jnp.exp2(m_prev - m_new)
        l_scr[...] = alpha * l_scr[...] + jnp.sum(p, axis=0, keepdims=True)
        pb = p.astype(BF16)
        pv = _dot(vst_scr[kt * chunks], pb[0:tq])
        for j in range(1, chunks):
            pv += _dot(vst_scr[kt * chunks + j], pb[j * tq:(j + 1) * tq])
        acc_scr[...] = alpha * acc_scr[...] + pv
        m_scr[...] = m_new

    n_tiles = (q0 + tq + tk - 1) // tk
    s_scr[...] = scores(0)

    def slc_step(kt):
        s = s_scr[...]
        s_next = scores(kt + 1)
        softmax_update(s, kt)
        s_scr[...] = s_next

    def slc_pair(i, carry):
        slc_step(2 * i)
        slc_step(2 * i + 1)
        return carry

    n_steps = n_tiles - 1
    lax.fori_loop(0, n_steps // 2, slc_pair, 0)

    @pl.when(n_steps % 2 == 1)
    def _():
        slc_step(n_steps - 1)

    kpos = (n_tiles - 1) * tk + lax.broadcasted_iota(jnp.int32, (tk, 1), 0)
    softmax_update(s_scr[...] + heads(jnp.where(kpos <= tpos, 0.0, NEG)), n_tiles - 1)

    gates = _sigmoid(gate_ref[...]).T
    for h in range(nh):
        hs = slice(h * HEAD_DIM, (h + 1) * HEAD_DIM)
        ql = slice(h * tq, (h + 1) * tq)
        o = (gates[3 * h:3 * h + 1] * oct_scr[:, ql]
             + gates[3 * h + 1:3 * h + 2] * (acc_scr[:, ql] / l_scr[:, ql])
             + gates[3 * h + 2:3 * h + 3] * owt_scr[:, ql])
        o_ref[:, hs] = o.T.astype(o_ref.dtype)


def _nsa(proj_a, proj_b, cos, sin_signed, w1k, w2k, w1v, w2v, pe_k, pe_v, batch, seq_len):
    n = proj_a.shape[0]
    tq = min(ATT_TILE, seq_len)
    nq = seq_len // tq
    gw = NSA_HPG * HEAD_DIM
    n_cmp, n_slc = seq_len // CMP_STRIDE, seq_len // SLC_LEN
    assert n_slc <= LANES // 2 and n_slc % SUBLANES == 0 and seq_len % min(SLC_KEY_TILE, seq_len) == 0

    def kv_spec(col0):
        return pl.BlockSpec((seq_len, HEAD_DIM), lambda b, g, i: (b, col0 // HEAD_DIM + g))

    def full(shape):
        return pl.BlockSpec(shape, lambda b, g, i: (0,) * len(shape))

    return pl.pallas_call(
        _nsa_kernel,
        grid=(batch, NSA_GROUPS, nq),
        in_specs=[
            pl.BlockSpec((tq, gw), lambda b, g, i: (b * nq + i, A_Q // gw + g)),
            pl.BlockSpec((tq, LANES), lambda b, g, i: (b * nq + i, B_GATE // LANES + g)),
            kv_spec(B_KC), kv_spec(B_VC),
            kv_spec(A_KS), kv_spec(A_VS), kv_spec(A_KW), kv_spec(A_VW),
            full((seq_len, HEAD_DIM)), full((seq_len, HEAD_DIM)),
            full((CMP_LEN * HEAD_DIM, HEAD_DIM)), full((HEAD_DIM, HEAD_DIM)),
            full((CMP_LEN * HEAD_DIM, HEAD_DIM)), full((HEAD_DIM, HEAD_DIM)),
            full((1, CMP_LEN * HEAD_DIM)), full((1, CMP_LEN * HEAD_DIM)),
        ],
        out_specs=pl.BlockSpec((tq, gw), lambda b, g, i: (b * nq + i, g)),
        out_shape=jax.ShapeDtypeStruct((n, NSA_WIDTH), BF16),
        scratch_shapes=[
            pltpu.VMEM((n_cmp, HEAD_DIM), BF16),
            pltpu.VMEM((HEAD_DIM, n_cmp), BF16),
            pltpu.VMEM((seq_len, 2 * HEAD_DIM), BF16),
            pltpu.VMEM((seq_len + WIN_LEN, HEAD_DIM), BF16),
            pltpu.VMEM((seq_len // tq, HEAD_DIM, tq), BF16),
            pltpu.VMEM(((seq_len + WIN_LEN) // tq, HEAD_DIM, tq), BF16),
            pltpu.VMEM((NSA_HPG * tq, 2 * HEAD_DIM), BF16),
            pltpu.VMEM((n_slc, tq), F32),
            pltpu.VMEM((n_slc, tq), F32),
            pltpu.VMEM((1, NSA_HPG * tq), F32),
            pltpu.VMEM((1, NSA_HPG * tq), F32),
            pltpu.VMEM((HEAD_DIM, NSA_HPG * tq), F32),
            pltpu.VMEM((HEAD_DIM, NSA_HPG * tq), F32),
            pltpu.VMEM((HEAD_DIM, NSA_HPG * tq), F32),
            pltpu.VMEM((min(SLC_KEY_TILE, seq_len), NSA_HPG * tq), F32),
            pltpu.VMEM((tq + WIN_LEN, tq), F32),
        ],
        compiler_params=_params(("parallel", "parallel", "arbitrary")),
        name="nsa",
    )(proj_a, proj_b, proj_b, proj_b, proj_a, proj_a, proj_a, proj_a,
      cos, sin_signed, w1k, w2k, w1v, w2v, pe_k, pe_v)


def _rope_tables(seq_len):
    inv = ROPE_THETA ** (-jnp.arange(0, HEAD_DIM, 2, dtype=F32) / HEAD_DIM)
    ang = jnp.arange(seq_len, dtype=F32)[:, None] * inv[None, :]
    ang = jnp.concatenate([ang, ang], -1)
    sign = jnp.concatenate([-jnp.ones((HEAD_DIM // 2,), F32), jnp.ones((HEAD_DIM // 2,), F32)])
    return jnp.cos(ang), jnp.sin(ang) * sign


def _split_w_in(w_in):
    offs = [0]
    for wd in IN_WIDTHS:
        offs.append(offs[-1] + wd)
    (q, kc, vc, ks, vs, kw, vw, gates, z, xbc, dt, pv) = [
        w_in[:, offs[i]:offs[i + 1]] for i in range(len(IN_WIDTHS))]
    d = w_in.shape[0]

    def pad_to(w, width):
        return jnp.pad(w, ((0, 0), (0, width - w.shape[1])))

    per_group = 3 * NSA_HPG
    w_a = jnp.concatenate([q, ks, vs, kw, vw], axis=1)
    w_b = jnp.concatenate(
        [xbc, z, pv, pad_to(dt, LANES),
         pad_to(gates[:, :per_group], LANES), pad_to(gates[:, per_group:], LANES),
         kc, vc, jnp.zeros((d, B_WIDTH - B_VC - KV_WIDTH), w_in.dtype)], axis=1)
    return w_a.astype(BF16), w_b.astype(BF16)


def _pad_lanes(v):
    return jnp.pad(v, (0, LANES - v.shape[0]))[None, :]


def kernel(x, ffn1_norm_pre, ffn1_norm_post, ffn1_w_gate, ffn1_w_up, ffn1_w_down, mix_norm_pre, mix_norm_post, w_in, cmp_pe_k, cmp_pe_v, cmp_k_w1, cmp_k_w2, cmp_v_w1, cmp_v_w2, ssm_conv_w, ssm_conv_b, ssm_dt_bias, ssm_a_log, ssm_d, ssm_norm, pool_w, pool_scale, w_out, ffn2_norm_pre, ffn2_norm_post, ffn2_w_gate, ffn2_w_up, ffn2_w_down):
    batch, seq_len, d = x.shape
    depth = w_in.shape[0]
    cos, sin_signed = _rope_tables(seq_len)
    h = x.reshape(batch * seq_len, d)
    for i in range(depth):
        h = _ffn(h, ffn1_norm_pre[i][None], ffn1_norm_post[i][None],
                 *_ffn_weights_bf16(ffn1_w_gate, ffn1_w_up, ffn1_w_down, i))
        proj_a, proj_b = _in_proj(h, mix_norm_pre[i][None], *_split_w_in(w_in[i]))
        o_nsa = _nsa(proj_a, proj_b, cos, sin_signed,
                     cmp_k_w1[i].astype(BF16), cmp_k_w2[i].astype(BF16),
                     cmp_v_w1[i].astype(BF16), cmp_v_w2[i].astype(BF16),
                     cmp_pe_k[i].reshape(1, -1), cmp_pe_v[i].reshape(1, -1), batch, seq_len)
        o_ssm = _ssd(proj_b, ssm_conv_w[i], ssm_conv_b[i][None], _pad_lanes(ssm_dt_bias[i]),
                     _pad_lanes(ssm_a_log[i]), jnp.repeat(ssm_d[i], SSM_HEAD_DIM)[None],
                     ssm_norm[i][None], batch, seq_len)
        o_pool = _pool(proj_b, pool_w[i].astype(BF16), pool_scale[i][None], seq_len)
        h = _out_proj(h, o_nsa, o_ssm, o_pool, w_out[i].astype(BF16), mix_norm_post[i][None])
        h = _ffn(h, ffn2_norm_pre[i][None], ffn2_norm_post[i][None],
                 *_ffn_weights_bf16(ffn2_w_gate, ffn2_w_up, ffn2_w_down, i))
    return h.reshape(batch, seq_len, d)
```

```python
import functools
import math

import jax
import jax.numpy as jnp
from jax import lax
from jax.experimental import pallas as pl
from jax.experimental.pallas import tpu as pltpu

F32 = jnp.float32
BF16 = jnp.bfloat16

D_MODEL = 2048
HEAD_DIM = 128
NSA_WIDTH = 1024
NSA_HEADS = 8
NSA_GROUPS = 2
NSA_HPG = NSA_HEADS // NSA_GROUPS
KV_WIDTH = NSA_GROUPS * HEAD_DIM
CMP_LEN = 32
CMP_STRIDE = 16
SLC_LEN = 64
SLC_TOPK = 16
WIN_LEN = 512
ROPE_THETA = 10000.0
SSM_WIDTH = 512
SSM_HEAD_DIM = 64
SSM_HEADS = 8
SSM_GROUPS = 2
SSM_STATE = 128
SSM_CONV = 4
SSM_CONV_DIM = SSM_WIDTH + 2 * SSM_GROUPS * SSM_STATE
POOL_WIDTH = 512
POOL_WINDOWS = (2, 4, 8, 16)
POOL_GROUP = POOL_WIDTH // len(POOL_WINDOWS)
D_FF = 5632
FFN_RESID = 0.5
RMS_EPS = 1e-6
NEG = -1e30
LOG2E = math.log2(math.e)
IN_WIDTHS = (NSA_WIDTH, KV_WIDTH, KV_WIDTH, KV_WIDTH, KV_WIDTH, KV_WIDTH, KV_WIDTH,
             3 * NSA_HEADS, SSM_WIDTH, SSM_CONV_DIM, SSM_HEADS, POOL_WIDTH)

LANES = 128
SUBLANES = 8
VMEM_LIMIT_BYTES = 60000 * 1024

A_Q, A_KS, A_VS, A_KW, A_VW = 0, 1024, 1280, 1536, 1792
A_WIDTH = 2048
B_XBC, B_Z, B_POOL, B_DT, B_GATE, B_KC, B_VC = 0, 1024, 1536, 2048, 2176, 2432, 2688
B_WIDTH = 3072

FFN_ROW_TILE = 1024
FFN_ROW_SPLIT = 2
FFN_TILE = 512
CAST_STEPS = 8
ROW_TILE = 512
PROJ_B_STEPS = 1
ATT_TILE = 256
SLC_KEY_TILE = 512
SSD_CHUNK = 128
SSD_STEP_ROWS = 512
POOL_TILE = 1024
POOL_HALO = 16
CONV_HALO = 8


def _params(semantics):
    return pltpu.CompilerParams(dimension_semantics=semantics, vmem_limit_bytes=VMEM_LIMIT_BYTES)


def _sigmoid(x):
    return 1.0 / (1.0 + jnp.exp(-x))


def _silu(x):
    return x * _sigmoid(x)


def _rms(x):
    return x * lax.rsqrt(jnp.mean(x * x, axis=-1, keepdims=True) + RMS_EPS)


def _dot(a, b):
    return jnp.dot(a, b, preferred_element_type=F32)


def _dot_nt(a, b):
    return lax.dot_general(a, b, (((1,), (1,)), ((), ())), preferred_element_type=F32)


def _split3(x):
    hi = x.astype(BF16)
    r = x - hi.astype(F32)
    mid = r.astype(BF16)
    lo = (r - mid.astype(F32)).astype(BF16)
    return hi, mid, lo


def _dot_exact_rhs(x, sel):
    hi, mid, lo = _split3(x)
    return _dot(hi, sel) + _dot(mid, sel) + _dot(lo, sel)


def _ffn_kernel(x_ref, gpre_ref, gpost_ref, wg_ref, wu_ref, wd_ref, o_ref, xn_ref):
    j = pl.program_id(1)
    last = pl.num_programs(1) - 1

    def step(first, final):
        part = x_ref.shape[0] // FFN_ROW_SPLIT
        for r in range(FFN_ROW_SPLIT):
            rows = slice(r * part, (r + 1) * part)
            if first:
                xn = (_rms(x_ref[rows, :]) * gpre_ref[...]).astype(BF16)
                xn_ref[rows, :] = xn
            else:
                xn = xn_ref[rows, :]
            g = _dot(xn, wg_ref[...])
            u = _dot(xn, wu_ref[...])
            a = (_silu(g) * u).astype(BF16)
            acc = _dot(a, wd_ref[...])
            if not first:
                acc += o_ref[rows, :]
            if final:
                o_ref[rows, :] = x_ref[rows, :] + FFN_RESID * (_rms(acc) * gpost_ref[...])
            else:
                o_ref[rows, :] = acc

    pl.when(j == 0)(functools.partial(step, True, False))
    pl.when((j > 0) & (j < last))(functools.partial(step, False, False))
    pl.when(j == last)(functools.partial(step, False, True))


def _ffn(x, gpre, gpost, wg, wu, wd):
    n, d = x.shape
    dff = wg.shape[1]
    tm, tf = min(FFN_ROW_TILE, n), FFN_TILE
    return pl.pallas_call(
        _ffn_kernel,
        grid=(n // tm, dff // tf),
        in_specs=[
            pl.BlockSpec((tm, d), lambda i, j: (i, 0)),
            pl.BlockSpec((1, d), lambda i, j: (0, 0)),
            pl.BlockSpec((1, d), lambda i, j: (0, 0)),
            pl.BlockSpec((d, tf), lambda i, j: (0, j)),
            pl.BlockSpec((d, tf), lambda i, j: (0, j)),
            pl.BlockSpec((tf, d), lambda i, j: (j, 0)),
        ],
        out_specs=pl.BlockSpec((tm, d), lambda i, j: (i, 0)),
        out_shape=jax.ShapeDtypeStruct((n, d), F32),
        scratch_shapes=[pltpu.VMEM((tm, d), BF16)],
        compiler_params=_params(("parallel", "arbitrary")),
        name="ffn",
    )(x, gpre, gpost, wg, wu, wd)


def _cast3_kernel(a_ref, b_ref, c_ref, oa_ref, ob_ref, oc_ref):
    oa_ref[...] = a_ref[...].astype(oa_ref.dtype)
    ob_ref[...] = b_ref[...].astype(ob_ref.dtype)
    oc_ref[...] = c_ref[...].astype(oc_ref.dtype)


def _ffn_weights_bf16(wg, wu, wd, layer):
    steps = CAST_STEPS

    def spec_in(w):
        return pl.BlockSpec((None, w.shape[1] // steps, w.shape[2]), lambda r: (layer, r, 0))

    def spec_out(w):
        return pl.BlockSpec((w.shape[1] // steps, w.shape[2]), lambda r: (r, 0))

    ws = (wg, wu, wd)
    return pl.pallas_call(
        _cast3_kernel,
        grid=(steps,),
        in_specs=[spec_in(w) for w in ws],
        out_specs=[spec_out(w) for w in ws],
        out_shape=[jax.ShapeDtypeStruct(w.shape[1:], BF16) for w in ws],
        compiler_params=_params(("parallel",)),
        name="ffn_weights_bf16",
    )(*ws)


def _in_proj_kernel(x_ref, g_ref, wa_ref, wb_ref, oa_ref, ob_ref, xn_ref):
    j = pl.program_id(1)
    slab = ob_ref.shape[1]

    @pl.when(j == 0)
    def _():
        xn = (_rms(x_ref[...]) * g_ref[...]).astype(BF16)
        xn_ref[...] = xn
        oa_ref[...] = _dot(xn, wa_ref[...]).astype(oa_ref.dtype)

    for s in range(PROJ_B_STEPS):
        @pl.when(j == s + 1)
        def _():
            ob_ref[...] = _dot(xn_ref[...], wb_ref[:, s * slab:(s + 1) * slab])


def _in_proj(x, g, w_a, w_b):
    n, d = x.shape
    tm = min(ROW_TILE, n)
    slab = B_WIDTH // PROJ_B_STEPS
    return pl.pallas_call(
        _in_proj_kernel,
        grid=(n // tm, 1 + PROJ_B_STEPS),
        in_specs=[
            pl.BlockSpec((tm, d), lambda i, j: (i, 0)),
            pl.BlockSpec((1, d), lambda i, j: (0, 0)),
            pl.BlockSpec((d, A_WIDTH), lambda i, j: (0, 0), pipeline_mode=pl.Buffered(1)),
            pl.BlockSpec((d, B_WIDTH), lambda i, j: (0, 0), pipeline_mode=pl.Buffered(1)),
        ],
        out_specs=[
            pl.BlockSpec((tm, A_WIDTH), lambda i, j: (i, 0)),
            pl.BlockSpec((tm, slab), lambda i, j: (i, jnp.maximum(j - 1, 0))),
        ],
        out_shape=[jax.ShapeDtypeStruct((n, A_WIDTH), BF16),
                   jax.ShapeDtypeStruct((n, B_WIDTH), F32)],
        scratch_shapes=[pltpu.VMEM((tm, d), BF16)],
        compiler_params=_params(("parallel", "arbitrary")),
        name="in_proj",
    )(x, g, w_a, w_b)


def _out_proj_kernel(h_ref, a_ref, s_ref, p_ref, w_ref, g_ref, o_ref):
    m = _dot(a_ref[...], w_ref[0:NSA_WIDTH, :])
    m += _dot(s_ref[...], w_ref[NSA_WIDTH:NSA_WIDTH + SSM_WIDTH, :])
    m += _dot(p_ref[...], w_ref[NSA_WIDTH + SSM_WIDTH:, :])
    o_ref[...] = h_ref[...] + _rms(m) * g_ref[...]


def _out_proj(h, o_nsa, o_ssm, o_pool, w, g):
    n, d = h.shape
    tm = min(ROW_TILE, n)
    return pl.pallas_call(
        _out_proj_kernel,
        grid=(n // tm,),
        in_specs=[
            pl.BlockSpec((tm, d), lambda i: (i, 0)),
            pl.BlockSpec((tm, NSA_WIDTH), lambda i: (i, 0)),
            pl.BlockSpec((tm, SSM_WIDTH), lambda i: (i, 0)),
            pl.BlockSpec((tm, POOL_WIDTH), lambda i: (i, 0)),
            pl.BlockSpec((d, d), lambda i: (0, 0)),
            pl.BlockSpec((1, d), lambda i: (0, 0)),
        ],
        out_specs=pl.BlockSpec((tm, d), lambda i: (i, 0)),
        out_shape=jax.ShapeDtypeStruct((n, d), F32),
        compiler_params=_params(("parallel",)),
        name="out_proj",
    )(h, o_nsa, o_ssm, o_pool, w, g)


def _pool_kernel(halo_ref, x_ref, w_ref, scale_ref, o_ref, *, tiles_per_seq):
    tile = x_ref.shape[0]
    it = pl.program_id(0) % tiles_per_seq
    halo = jnp.where(it == 0, 0.0, halo_ref[...])
    ext = jnp.concatenate([halo, x_ref[...]], axis=0)
    pos = it * tile + lax.broadcasted_iota(jnp.int32, (tile, POOL_GROUP), 0)
    for gi, w in enumerate(POOL_WINDOWS):
        xg = ext[:, gi * POOL_GROUP:(gi + 1) * POOL_GROUP]
        s, k = xg, 1
        while k < w:
            s = s + pltpu.roll(s, k, axis=0)
            k *= 2
        cnt = jnp.minimum(pos + 1, w).astype(F32)
        dlt = s[POOL_HALO:] / cnt - xg[POOL_HALO:]
        y = _dot(dlt.astype(BF16), w_ref[gi])
        sl = slice(gi * POOL_GROUP, (gi + 1) * POOL_GROUP)
        o_ref[:, sl] = (y * scale_ref[:, sl]).astype(o_ref.dtype)


def _pool(proj_b, pool_w, pool_scale, seq_len):
    n = proj_b.shape[0]
    tile = min(POOL_TILE, seq_len)
    tiles_per_seq = seq_len // tile
    ratio = tile // POOL_HALO
    col = B_POOL // POOL_WIDTH
    return pl.pallas_call(
        functools.partial(_pool_kernel, tiles_per_seq=tiles_per_seq),
        grid=(n // tile,),
        in_specs=[
            pl.BlockSpec((POOL_HALO, POOL_WIDTH), lambda i: (jnp.maximum(i * ratio - 1, 0), col)),
            pl.BlockSpec((tile, POOL_WIDTH), lambda i: (i, col)),
            pl.BlockSpec((len(POOL_WINDOWS), POOL_GROUP, POOL_GROUP), lambda i: (0, 0, 0)),
            pl.BlockSpec((1, POOL_WIDTH), lambda i: (0, 0)),
        ],
        out_specs=pl.BlockSpec((tile, POOL_WIDTH), lambda i: (i, 0)),
        out_shape=jax.ShapeDtypeStruct((n, POOL_WIDTH), BF16),
        compiler_params=_params(("parallel",)),
        name="pool",
    )(proj_b, proj_b, pool_w, pool_scale)


def _ssd_kernel(halo_ref, xbc_ref, z_ref, dt_ref, convw_ref, convb_ref, dtb_ref, alog_ref,
                dskip_ref, normw_ref, o_ref, state_ref):
    c = pl.program_id(1)

    @pl.when(c == 0)
    def _():
        state_ref[...] = jnp.zeros_like(state_ref)

    for r in range(xbc_ref.shape[0] // SSD_CHUNK):
        rows = slice(r * SSD_CHUNK, (r + 1) * SSD_CHUNK)
        if r == 0:
            halo = jnp.where(c == 0, 0.0, halo_ref[...])
        else:
            halo = xbc_ref[r * SSD_CHUNK - CONV_HALO:r * SSD_CHUNK, :]
        _ssd_chunk(halo, xbc_ref[rows, :], z_ref[rows, :], dt_ref[rows, :], convw_ref, convb_ref, dtb_ref,
                   alog_ref, dskip_ref, normw_ref, o_ref.at[rows, :], state_ref)


def _ssd_chunk(halo, xbc, z, dt_raw, convw_ref, convb_ref, dtb_ref, alog_ref, dskip_ref, normw_ref,
               o_ref, state_ref):
    L = xbc.shape[0]

    ext = jnp.concatenate([halo, xbc], axis=0)
    conv = convb_ref[...] + convw_ref[SSM_CONV - 1:SSM_CONV, :] * xbc
    for k in range(1, SSM_CONV):
        conv += convw_ref[SSM_CONV - 1 - k:SSM_CONV - k, :] * pltpu.roll(ext, k, axis=0)[CONV_HALO:]
    act = _silu(conv)
    xs = act[:, :SSM_WIDTH]
    bm = act[:, SSM_WIDTH:SSM_WIDTH + SSM_GROUPS * SSM_STATE]
    cm = act[:, SSM_WIDTH + SSM_GROUPS * SSM_STATE:]

    lane = lax.broadcasted_iota(jnp.int32, (1, LANES), 1)
    xdt = dt_raw + dtb_ref[...]
    dt = jnp.maximum(xdt, 0.0) + jnp.log1p(jnp.exp(-jnp.abs(xdt)))
    a_head = jnp.where(lane < SSM_HEADS, -jnp.exp(alog_ref[...]), 0.0)
    cs = a_head * dt
    row = lax.broadcasted_iota(jnp.int32, (L, LANES), 0)
    k = 1
    while k < L:
        cs = cs + jnp.where(row >= k, pltpu.roll(cs, k, axis=0), 0.0)
        k *= 2
    cs_t = cs.T

    er = lax.broadcasted_iota(jnp.int32, (LANES, SSM_WIDTH), 0)
    ec = lax.broadcasted_iota(jnp.int32, (LANES, SSM_WIDTH), 1)
    expand = jnp.where(er == ec // SSM_HEAD_DIM, 1.0, 0.0).astype(BF16)
    dtx = _dot_exact_rhs(dt, expand)
    csx = _dot_exact_rhs(cs, expand)
    cs_last = csx[L - 1:L, :]

    x_dt = xs * dtx
    x_end = (x_dt * jnp.exp(cs_last - csx)).astype(BF16)
    x_dt16 = x_dt.astype(BF16)
    ecs = jnp.exp(csx)
    li = lax.broadcasted_iota(jnp.int32, (L, L), 0)
    si = lax.broadcasted_iota(jnp.int32, (L, L), 1)
    gw = SSM_WIDTH // SSM_GROUPS
    hpg = SSM_HEADS // SSM_GROUPS
    lane_g = lax.broadcasted_iota(jnp.int32, (1, gw), 1)
    ys = []
    for g in range(SSM_GROUPS):
        bg = bm[:, g * SSM_STATE:(g + 1) * SSM_STATE]
        cg = cm[:, g * SSM_STATE:(g + 1) * SSM_STATE].astype(BF16)
        gs = slice(g * gw, (g + 1) * gw)
        cb = _dot_nt(cg, bg.astype(BF16))
        y_g = jnp.zeros((L, gw), F32)
        for hh in range(hpg):
            h = g * hpg + hh
            seg = cs[:, h:h + 1] - cs_t[h:h + 1, :]
            m_h = (cb * jnp.exp(jnp.where(li >= si, seg, NEG))).astype(BF16)
            x_h = jnp.where(lane_g // SSM_HEAD_DIM == hh, x_dt16[:, gs], jnp.zeros((), BF16))
            y_g += _dot(m_h, x_h)
        st = state_ref[g]
        y_g += _dot(cg, st.astype(BF16)) * ecs[:, gs]
        state_ref[g] = jnp.exp(cs_last[:, gs]) * st + _dot(bg.T.astype(BF16), x_end[:, gs])
        ys.append(y_g)
    y = jnp.concatenate(ys, axis=1) + dskip_ref[...] * xs
    y = y * _silu(z)
    outs = []
    for g in range(SSM_GROUPS):
        outs.append(_rms(y[:, g * gw:(g + 1) * gw]))
    o_ref[...] = (jnp.concatenate(outs, axis=1) * normw_ref[...]).astype(o_ref.dtype)


def _ssd(proj_b, conv_w, conv_b, dt_bias, a_log, d_skip, norm_w, batch, seq_len):
    n = proj_b.shape[0]
    L = min(SSD_STEP_ROWS, seq_len)
    nch = seq_len // L
    ratio = L // CONV_HALO
    return pl.pallas_call(
        _ssd_kernel,
        grid=(batch, nch),
        in_specs=[
            pl.BlockSpec((CONV_HALO, SSM_CONV_DIM),
                         lambda b, c: (jnp.maximum((b * nch + c) * ratio - 1, 0), B_XBC // SSM_CONV_DIM)),
            pl.BlockSpec((L, SSM_CONV_DIM), lambda b, c: (b * nch + c, B_XBC // SSM_CONV_DIM)),
            pl.BlockSpec((L, SSM_WIDTH), lambda b, c: (b * nch + c, B_Z // SSM_WIDTH)),
            pl.BlockSpec((L, LANES), lambda b, c: (b * nch + c, B_DT // LANES)),
            pl.BlockSpec((SSM_CONV, SSM_CONV_DIM), lambda b, c: (0, 0)),
            pl.BlockSpec((1, SSM_CONV_DIM), lambda b, c: (0, 0)),
            pl.BlockSpec((1, LANES), lambda b, c: (0, 0)),
            pl.BlockSpec((1, LANES), lambda b, c: (0, 0)),
            pl.BlockSpec((1, SSM_WIDTH), lambda b, c: (0, 0)),
            pl.BlockSpec((1, SSM_WIDTH), lambda b, c: (0, 0)),
        ],
        out_specs=pl.BlockSpec((L, SSM_WIDTH), lambda b, c: (b * nch + c, 0)),
        out_shape=jax.ShapeDtypeStruct((n, SSM_WIDTH), BF16),
        scratch_shapes=[pltpu.VMEM((SSM_GROUPS, SSM_STATE, SSM_WIDTH // SSM_GROUPS), F32)],
        compiler_params=_params(("parallel", "arbitrary")),
        name="ssd",
    )(proj_b, proj_b, proj_b, proj_b, conv_w, conv_b, dt_bias, a_log, d_skip, norm_w)


def _rope(x, cos, sin_signed):
    return x * cos + pltpu.roll(x, HEAD_DIM // 2, axis=1) * sin_signed


def _compress(src_ref, w1_ref, w2_ref, pe_ref):
    n_cmp = src_ref.shape[0] // CMP_STRIDE
    xs = [src_ref[pl.ds(p, n_cmp, stride=CMP_STRIDE), :] for p in range(CMP_STRIDE)]
    x = jnp.concatenate(xs, axis=1).astype(BF16)
    half = CMP_STRIDE * HEAD_DIM
    first = _dot(x, w1_ref[0:half, :])
    second = _dot(x, w1_ref[half:2 * half, :])
    pe = jnp.broadcast_to(pe_ref[...], (SUBLANES, 2 * half)).astype(BF16)
    pe_term = _dot(pe, w1_ref[...])[0:1, :]
    pre = first + pltpu.roll(second, n_cmp - 1, axis=0) + pe_term
    return _dot(_silu(pre).astype(BF16), w2_ref[...])


def _nsa_kernel(q_ref, gate_ref, kc_src_ref, vc_src_ref, ks_ref, vs_ref, kw_ref, vw_ref,
                cos_ref, sin_ref, w1k_ref, w2k_ref, w1v_ref, w2v_ref, pek_ref, pev_ref,
                o_ref,
                kc_scr, vct_scr, kx_scr, kwp_scr, vst_scr, vwt_scr, qx_scr, score_scr, rank_scr,
                m_scr, l_scr, acc_scr, oct_scr, owt_scr, s_scr, band_scr):
    tq = q_ref.shape[0]
    seq = ks_ref.shape[0]
    tk = min(SLC_KEY_TILE, seq)
    nh = NSA_HPG
    n_cmp, n_slc = seq // CMP_STRIDE, seq // SLC_LEN
    topk = min(SLC_TOPK, n_slc)
    qi = pl.program_id(2)
    q0 = qi * tq
    qk_scale = HEAD_DIM ** -0.5 * LOG2E

    @pl.when(qi == 0)
    def _():
        kc_scr[...] = _compress(kc_src_ref, w1k_ref, w2k_ref, pek_ref).astype(BF16)
        vct_scr[...] = _compress(vc_src_ref, w1v_ref, w2v_ref, pev_ref).T.astype(BF16)
        kwp_scr[0:WIN_LEN, :] = jnp.zeros((WIN_LEN, HEAD_DIM), BF16)
        for j in range(WIN_LEN // tq):
            vwt_scr[j] = jnp.zeros((HEAD_DIM, tq), BF16)
        kc_i = lax.broadcasted_iota(jnp.int32, (tq + WIN_LEN, 1), 0)
        qr_i = lax.broadcasted_iota(jnp.int32, (1, tq), 1)
        band_scr[...] = jnp.where((kc_i > qr_i) & (kc_i <= qr_i + WIN_LEN), 0.0, NEG)

        def prep_rows(i, carry):
            rows = pl.ds(pl.multiple_of(i * tq, tq), tq)
            prows = pl.ds(pl.multiple_of(WIN_LEN + i * tq, tq), tq)
            cos, sin = cos_ref[rows, :], sin_ref[rows, :]
            kx_scr[rows, 0:HEAD_DIM] = _rope(ks_ref[rows, :].astype(F32), cos, sin).astype(BF16)
            blk = (i * tq + lax.broadcasted_iota(jnp.int32, (tq, LANES), 0)) // SLC_LEN
            lane = lax.broadcasted_iota(jnp.int32, (tq, LANES), 1)
            kx_scr[rows, HEAD_DIM:2 * HEAD_DIM] = jnp.where(lane == blk, 1.0, 0.0).astype(BF16)
            kwp_scr[prows, :] = _rope(kw_ref[rows, :].astype(F32), cos, sin).astype(BF16)
            vst_scr[i] = vs_ref[rows, :].astype(F32).T.astype(BF16)
            vwt_scr[i + WIN_LEN // tq] = vw_ref[rows, :].astype(F32).T.astype(BF16)
            return carry

        lax.fori_loop(0, seq // tq, prep_rows, 0)

    qrows = pl.ds(pl.multiple_of(q0, tq), tq)
    qlane = lax.broadcasted_iota(jnp.int32, (1, tq), 1)
    tpos = q0 + qlane

    def q_rows(h):
        return slice(h * tq, (h + 1) * tq)

    cos_q, sin_q = cos_ref[qrows, :], sin_ref[qrows, :]
    for h in range(nh):
        hs = slice(h * HEAD_DIM, (h + 1) * HEAD_DIM)
        qx_scr[q_rows(h), 0:HEAD_DIM] = (_rope(q_ref[:, hs].astype(F32), cos_q, sin_q) * qk_scale).astype(BF16)

    def heads(x):
        return jnp.concatenate([x] * nh, axis=1)

    q_stack = jnp.concatenate([q_ref[:, h * HEAD_DIM:(h + 1) * HEAD_DIM] for h in range(nh)], axis=0)
    nblk = lax.broadcasted_iota(jnp.int32, (n_cmp, 1), 0)
    cmp_bias = jnp.where((nblk * CMP_STRIDE + (CMP_LEN - 1)) <= tpos, 0.0, NEG)
    any_visible = jnp.where(tpos >= CMP_LEN - 1, 1.0, 0.0)
    s = _dot_nt(kc_scr[...], q_stack) * qk_scale + heads(cmp_bias)
    e = jnp.exp2(s - jnp.max(s, axis=0, keepdims=True))
    p = e * (heads(any_visible) / jnp.sum(e, axis=0, keepdims=True))
    p_sum = p[:, 0:tq]
    for h in range(1, nh):
        p_sum += p[:, h * tq:(h + 1) * tq]
    oct_scr[...] = _dot(vct_scr[...], p.astype(BF16))

    wspan = tq + WIN_LEN
    wrows = pl.ds(pl.multiple_of(q0, tq), wspan)
    c = lax.broadcasted_iota(jnp.int32, (wspan, 1), 0)
    wb = band_scr[...] + jnp.where(c + q0 >= WIN_LEN, 0.0, NEG)
    s = _dot_nt(kwp_scr[wrows, :], qx_scr[:, 0:HEAD_DIM]) + heads(wb)
    p = jnp.exp2(s - jnp.max(s, axis=0, keepdims=True))
    pb = p.astype(BF16)
    o = _dot(vwt_scr[qi], pb[0:tq])
    for j in range(1, wspan // tq):
        o += _dot(vwt_scr[qi + j], pb[j * tq:(j + 1) * tq])
    owt_scr[...] = o / jnp.sum(p, axis=0, keepdims=True)

    nr = lax.broadcasted_iota(jnp.int32, (n_slc, n_cmp), 1) * CMP_STRIDE
    jr = lax.broadcasted_iota(jnp.int32, (n_slc, n_cmp), 0) * SLC_LEN
    overlap_t = jnp.where((nr < jr + SLC_LEN) & (nr + CMP_LEN > jr), 1.0, 0.0).astype(BF16)
    ps_hi = p_sum.astype(BF16)
    ps_lo = (p_sum - ps_hi.astype(F32)).astype(BF16)
    imp_t = _dot(overlap_t, ps_hi) + _dot(overlap_t, ps_lo)

    jblk = lax.broadcasted_iota(jnp.int32, (n_slc, tq), 0)
    qblk = (q0 + lax.broadcasted_iota(jnp.int32, (n_slc, tq), 1)) // SLC_LEN
    valid = jblk <= qblk
    forced = (jblk == 0) | (jblk == qblk) | (jblk == qblk - 1)
    score_scr[...] = jnp.where(valid, jnp.where(forced, 1e9, imp_t), -1e9)
    rank_scr[...] = jnp.zeros_like(rank_scr)
    ng = n_slc // SUBLANES
    g_last = ((q0 + tq - 1) // SLC_LEN) // SUBLANES
    row8 = lax.broadcasted_iota(jnp.int32, (SUBLANES, tq), 0)
    for gp in range(ng):
        @pl.when(gp <= g_last)
        def _():
            cnt = [jnp.zeros((SUBLANES, tq), F32) for _ in range(ng)]
            for jj in range(SUBLANES):
                jp = gp * SUBLANES + jj
                other = jnp.broadcast_to(score_scr[jp:jp + 1, :], (SUBLANES, tq))
                for g in range(ng):
                    sc = score_scr[g * SUBLANES:(g + 1) * SUBLANES, :]
                    if g > gp:
                        ahead = other >= sc
                    elif g < gp:
                        ahead = other > sc
                    else:
                        ahead = (other > sc) | ((other == sc) & (row8 > jj))
                    cnt[g] += jnp.where(ahead, 1.0, 0.0)
            for g in range(ng):
                rank_scr[g * SUBLANES:(g + 1) * SUBLANES, :] += cnt[g]

    sel_t = jnp.where(valid & (rank_scr[...] < topk), 1.0, 0.0).astype(BF16)
    sel_pad = jnp.concatenate([sel_t, jnp.zeros((LANES - n_slc, tq), BF16)], axis=0)
    ri = lax.broadcasted_iota(jnp.int32, (tq, tq), 0)
    ci = lax.broadcasted_iota(jnp.int32, (tq, tq), 1)
    eye = jnp.where(ri == ci, 1.0, 0.0).astype(BF16)
    sel = _dot_nt(eye, sel_pad)
    lane = lax.broadcasted_iota(jnp.int32, (tq, LANES), 1)
    sel_bias = jnp.where((lane < n_slc) & (sel < 0.5), NEG, 0.0).astype(BF16)

    for h in range(nh):
        qx_scr[q_rows(h), HEAD_DIM:2 * HEAD_DIM] = sel_bias
    m_scr[...] = jnp.full(m_scr.shape, NEG, F32)
    l_scr[...] = jnp.zeros_like(l_scr)
    acc_scr[...] = jnp.zeros_like(acc_scr)
    chunks = tk // tq

    def scores(kt):
        return _dot_nt(kx_scr[pl.ds(pl.multiple_of(kt * tk, tk), tk), :], qx_scr[...])

    def softmax_update(s, kt):
        m_prev = m_scr[...]
        m_new = jnp.maximum(m_prev, jnp.max(s, axis=0, keepdims=True))
        p = jnp.exp2(s - m_new)
        alpha = jnp.exp2(m_prev - m_new)
        l_scr[...] = alpha * l_scr[...] + jnp.sum(p, axis=0, keepdims=True)
        pb = p.astype(BF16)
        pv = _dot(vst_scr[kt * chunks], pb[0:tq])
        for j in range(1, chunks):
            pv += _dot(vst_scr[kt * chunks + j], pb[j * tq:(j + 1) * tq])
        acc_scr[...] = alpha * acc_scr[...] + pv
        m_scr[...] = m_new

    n_tiles = (q0 + tq + tk - 1) // tk
    s_scr[...] = scores(0)

    def slc_step(kt):
        s = s_scr[...]
        s_next = scores(kt + 1)
        softmax_update(s, kt)
        s_scr[...] = s_next

    def slc_pair(i, carry):
        slc_step(2 * i)
        slc_step(2 * i + 1)
        return carry

    n_steps = n_tiles - 1
    lax.fori_loop(0, n_steps // 2, slc_pair, 0)

    @pl.when(n_steps % 2 == 1)
    def _():
        slc_step(n_steps - 1)

    kpos = (n_tiles - 1) * tk + lax.broadcasted_iota(jnp.int32, (tk, 1), 0)
    softmax_update(s_scr[...] + heads(jnp.where(kpos <= tpos, 0.0, NEG)), n_tiles - 1)

    gates = _sigmoid(gate_ref[...]).T
    for h in range(nh):
        hs = slice(h * HEAD_DIM, (h + 1) * HEAD_DIM)
        ql = slice(h * tq, (h + 1) * tq)
        o = (gates[3 * h:3 * h + 1] * oct_scr[:, ql]
             + gates[3 * h + 1:3 * h + 2] * (acc_scr[:, ql] / l_scr[:, ql])
             + gates[3 * h + 2:3 * h + 3] * owt_scr[:, ql])
        o_ref[:, hs] = o.T.astype(o_ref.dtype)


def _nsa(proj_a, proj_b, cos, sin_signed, w1k, w2k, w1v, w2v, pe_k, pe_v, batch, seq_len):
    n = proj_a.shape[0]
    tq = min(ATT_TILE, seq_len)
    nq = seq_len // tq
    gw = NSA_HPG * HEAD_DIM
    n_cmp, n_slc = seq_len // CMP_STRIDE, seq_len // SLC_LEN
    assert n_slc <= LANES // 2 and n_slc % SUBLANES == 0 and seq_len % min(SLC_KEY_TILE, seq_len) == 0

    def kv_spec(col0):
        return pl.BlockSpec((seq_len, HEAD_DIM), lambda b, g, i: (b, col0 // HEAD_DIM + g))

    def full(shape):
        return pl.BlockSpec(shape, lambda b, g, i: (0,) * len(shape))

    return pl.pallas_call(
        _nsa_kernel,
        grid=(batch, NSA_GROUPS, nq),
        in_specs=[
            pl.BlockSpec((tq, gw), lambda b, g, i: (b * nq + i, A_Q // gw + g)),
            pl.BlockSpec((tq, LANES), lambda b, g, i: (b * nq + i, B_GATE // LANES + g)),
            kv_spec(B_KC), kv_spec(B_VC),
            kv_spec(A_KS), kv_spec(A_VS), kv_spec(A_KW), kv_spec(A_VW),
            full((seq_len, HEAD_DIM)), full((seq_len, HEAD_DIM)),
            full((CMP_LEN * HEAD_DIM, HEAD_DIM)), full((HEAD_DIM, HEAD_DIM)),
            full((CMP_LEN * HEAD_DIM, HEAD_DIM)), full((HEAD_DIM, HEAD_DIM)),
            full((1, CMP_LEN * HEAD_DIM)), full((1, CMP_LEN * HEAD_DIM)),
        ],
        out_specs=pl.BlockSpec((tq, gw), lambda b, g, i: (b * nq + i, g)),
        out_shape=jax.ShapeDtypeStruct((n, NSA_WIDTH), BF16),
        scratch_shapes=[
            pltpu.VMEM((n_cmp, HEAD_DIM), BF16),
            pltpu.VMEM((HEAD_DIM, n_cmp), BF16),
            pltpu.VMEM((seq_len, 2 * HEAD_DIM), BF16),
            pltpu.VMEM((seq_len + WIN_LEN, HEAD_DIM), BF16),
            pltpu.VMEM((seq_len // tq, HEAD_DIM, tq), BF16),
            pltpu.VMEM(((seq_len + WIN_LEN) // tq, HEAD_DIM, tq), BF16),
            pltpu.VMEM((NSA_HPG * tq, 2 * HEAD_DIM), BF16),
            pltpu.VMEM((n_slc, tq), F32),
            pltpu.VMEM((n_slc, tq), F32),
            pltpu.VMEM((1, NSA_HPG * tq), F32),
            pltpu.VMEM((1, NSA_HPG * tq), F32),
            pltpu.VMEM((HEAD_DIM, NSA_HPG * tq), F32),
            pltpu.VMEM((HEAD_DIM, NSA_HPG * tq), F32),
            pltpu.VMEM((HEAD_DIM, NSA_HPG * tq), F32),
            pltpu.VMEM((min(SLC_KEY_TILE, seq_len), NSA_HPG * tq), F32),
            pltpu.VMEM((tq + WIN_LEN, tq), F32),
        ],
        compiler_params=_params(("parallel", "parallel", "arbitrary")),
        name="nsa",
    )(proj_a, proj_b, proj_b, proj_b, proj_a, proj_a, proj_a, proj_a,
      cos, sin_signed, w1k, w2k, w1v, w2v, pe_k, pe_v)


def _rope_tables(seq_len):
    inv = ROPE_THETA ** (-jnp.arange(0, HEAD_DIM, 2, dtype=F32) / HEAD_DIM)
    ang = jnp.arange(seq_len, dtype=F32)[:, None] * inv[None, :]
    ang = jnp.concatenate([ang, ang], -1)
    sign = jnp.concatenate([-jnp.ones((HEAD_DIM // 2,), F32), jnp.ones((HEAD_DIM // 2,), F32)])
    return jnp.cos(ang), jnp.sin(ang) * sign


def _split_w_in(w_in):
    offs = [0]
    for wd in IN_WIDTHS:
        offs.append(offs[-1] + wd)
    (q, kc, vc, ks, vs, kw, vw, gates, z, xbc, dt, pv) = [
        w_in[:, offs[i]:offs[i + 1]] for i in range(len(IN_WIDTHS))]
    d = w_in.shape[0]

    def pad_to(w, width):
        return jnp.pad(w, ((0, 0), (0, width - w.shape[1])))

    per_group = 3 * NSA_HPG
    w_a = jnp.concatenate([q, ks, vs, kw, vw], axis=1)
    w_b = jnp.concatenate(
        [xbc, z, pv, pad_to(dt, LANES),
         pad_to(gates[:, :per_group], LANES), pad_to(gates[:, per_group:], LANES),
         kc, vc, jnp.zeros((d, B_WIDTH - B_VC - KV_WIDTH), w_in.dtype)], axis=1)
    return w_a.astype(BF16), w_b.astype(BF16)


def _pad_lanes(v):
    return jnp.pad(v, (0, LANES - v.shape[0]))[None, :]


def kernel(x, ffn1_norm_pre, ffn1_norm_post, ffn1_w_gate, ffn1_w_up, ffn1_w_down, mix_norm_pre, mix_norm_post, w_in, cmp_pe_k, cmp_pe_v, cmp_k_w1, cmp_k_w2, cmp_v_w1, cmp_v_w2, ssm_conv_w, ssm_conv_b, ssm_dt_bias, ssm_a_log, ssm_d, ssm_norm, pool_w, pool_scale, w_out, ffn2_norm_pre, ffn2_norm_post, ffn2_w_gate, ffn2_w_up, ffn2_w_down):
    batch, seq_len, d = x.shape
    depth = w_in.shape[0]
    cos, sin_signed = _rope_tables(seq_len)
    h = x.reshape(batch * seq_len, d)
    for i in range(depth):
        h = _ffn(h, ffn1_norm_pre[i][None], ffn1_norm_post[i][None],
                 *_ffn_weights_bf16(ffn1_w_gate, ffn1_w_up, ffn1_w_down, i))
        proj_a, proj_b = _in_proj(h, mix_norm_pre[i][None], *_split_w_in(w_in[i]))
        o_nsa = _nsa(proj_a, proj_b, cos, sin_signed,
                     cmp_k_w1[i].astype(BF16), cmp_k_w2[i].astype(BF16),
                     cmp_v_w1[i].astype(BF16), cmp_v_w2[i].astype(BF16),
                     cmp_pe_k[i].reshape(1, -1), cmp_pe_v[i].reshape(1, -1), batch, seq_len)
        o_ssm = _ssd(proj_b, ssm_conv_w[i], ssm_conv_b[i][None], _pad_lanes(ssm_dt_bias[i]),
                     _pad_lanes(ssm_a_log[i]), jnp.repeat(ssm_d[i], SSM_HEAD_DIM)[None],
                     ssm_norm[i][None], batch, seq_len)
        o_pool = _pool(proj_b, pool_w[i].astype(BF16), pool_scale[i][None], seq_len)
        h = _out_proj(h, o_nsa, o_ssm, o_pool, w_out[i].astype(BF16), mix_norm_post[i][None])
        h = _ffn(h, ffn2_norm_pre[i][None], ffn2_norm_post[i][None],
                 *_ffn_weights_bf16(ffn2_w_gate, ffn2_w_up, ffn2_w_down, i))
    return h.reshape(batch, seq_len, d)
```

```python
import functools
import math

import jax
import jax.numpy as jnp
from jax import lax
from jax.experimental import pallas as pl
from jax.experimental.pallas import tpu as pltpu

F32 = jnp.float32
BF16 = jnp.bfloat16

D_MODEL = 2048
HEAD_DIM = 128
NSA_WIDTH = 1024
NSA_HEADS = 8
NSA_GROUPS = 2
NSA_HPG = NSA_HEADS // NSA_GROUPS
KV_WIDTH = NSA_GROUPS * HEAD_DIM
CMP_LEN = 32
CMP_STRIDE = 16
SLC_LEN = 64
SLC_TOPK = 16
WIN_LEN = 512
ROPE_THETA = 10000.0
SSM_WIDTH = 512
SSM_HEAD_DIM = 64
SSM_HEADS = 8
SSM_GROUPS = 2
SSM_STATE = 128
SSM_CONV = 4
SSM_CONV_DIM = SSM_WIDTH + 2 * SSM_GROUPS * SSM_STATE
POOL_WIDTH = 512
POOL_WINDOWS = (2, 4, 8, 16)
POOL_GROUP = POOL_WIDTH // len(POOL_WINDOWS)
D_FF = 5632
FFN_RESID = 0.5
RMS_EPS = 1e-6
NEG = -1e30
LOG2E = math.log2(math.e)
IN_WIDTHS = (NSA_WIDTH, KV_WIDTH, KV_WIDTH, KV_WIDTH, KV_WIDTH, KV_WIDTH, KV_WIDTH,
             3 * NSA_HEADS, SSM_WIDTH, SSM_CONV_DIM, SSM_HEADS, POOL_WIDTH)

LANES = 128
SUBLANES = 8
VMEM_LIMIT_BYTES = 60000 * 1024

A_Q, A_KS, A_VS, A_KW, A_VW = 0, 1024, 1280, 1536, 1792
A_WIDTH = 2048
B_XBC, B_Z, B_POOL, B_DT, B_GATE, B_KC, B_VC = 0, 1024, 1536, 2048, 2176, 2432, 2688
B_WIDTH = 3072

FFN_ROW_TILE = 1024
FFN_ROW_SPLIT = 2
FFN_TILE = 512
CAST_STEPS = 8
ROW_TILE = 512
PROJ_B_STEPS = 1
ATT_TILE = 256
SLC_KEY_TILE = 512
SSD_CHUNK = 128
SSD_STEP_ROWS = 512
POOL_TILE = 1024
POOL_HALO = 16
CONV_HALO = 8


def _params(semantics):
    return pltpu.CompilerParams(dimension_semantics=semantics, vmem_limit_bytes=VMEM_LIMIT_BYTES)


def _sigmoid(x):
    return 1.0 / (1.0 + jnp.exp(-x))


def _silu(x):
    return x * _sigmoid(x)


def _rms(x):
    return x * lax.rsqrt(jnp.mean(x * x, axis=-1, keepdims=True) + RMS_EPS)


def _dot(a, b):
    return jnp.dot(a, b, preferred_element_type=F32)


def _dot_nt(a, b):
    return lax.dot_general(a, b, (((1,), (1,)), ((), ())), preferred_element_type=F32)


def _split3(x):
    hi = x.astype(BF16)
    r = x - hi.astype(F32)
    mid = r.astype(BF16)
    lo = (r - mid.astype(F32)).astype(BF16)
    return hi, mid, lo


def _dot_exact_rhs(x, sel):
    hi, mid, lo = _split3(x)
    return _dot(hi, sel) + _dot(mid, sel) + _dot(lo, sel)


def _ffn_kernel(x_ref, gpre_ref, gpost_ref, wg_ref, wu_ref, wd_ref, o_ref, xn_ref):
    j = pl.program_id(1)
    last = pl.num_programs(1) - 1

    def step(first, final):
        part = x_ref.shape[0] // FFN_ROW_SPLIT
        for r in range(FFN_ROW_SPLIT):
            rows = slice(r * part, (r + 1) * part)
            if first:
                xn = (_rms(x_ref[rows, :]) * gpre_ref[...]).astype(BF16)
                xn_ref[rows, :] = xn
            else:
                xn = xn_ref[rows, :]
            g = _dot(xn, wg_ref[...])
            u = _dot(xn, wu_ref[...])
            a = (_silu(g) * u).astype(BF16)
            acc = _dot(a, wd_ref[...])
            if not first:
                acc += o_ref[rows, :]
            if final:
                o_ref[rows, :] = x_ref[rows, :] + FFN_RESID * (_rms(acc) * gpost_ref[...])
            else:
                o_ref[rows, :] = acc

    pl.when(j == 0)(functools.partial(step, True, False))
    pl.when((j > 0) & (j < last))(functools.partial(step, False, False))
    pl.when(j == last)(functools.partial(step, False, True))


def _ffn(x, gpre, gpost, wg, wu, wd):
    n, d = x.shape
    dff = wg.shape[1]
    tm, tf = min(FFN_ROW_TILE, n), FFN_TILE
    return pl.pallas_call(
        _ffn_kernel,
        grid=(n // tm, dff // tf),
        in_specs=[
            pl.BlockSpec((tm, d), lambda i, j: (i, 0)),
            pl.BlockSpec((1, d), lambda i, j: (0, 0)),
            pl.BlockSpec((1, d), lambda i, j: (0, 0)),
            pl.BlockSpec((d, tf), lambda i, j: (0, j)),
            pl.BlockSpec((d, tf), lambda i, j: (0, j)),
            pl.BlockSpec((tf, d), lambda i, j: (j, 0)),
        ],
        out_specs=pl.BlockSpec((tm, d), lambda i, j: (i, 0)),
        out_shape=jax.ShapeDtypeStruct((n, d), F32),
        scratch_shapes=[pltpu.VMEM((tm, d), BF16)],
        compiler_params=_params(("parallel", "arbitrary")),
        name="ffn",
    )(x, gpre, gpost, wg, wu, wd)


def _cast3_kernel(a_ref, b_ref, c_ref, oa_ref, ob_ref, oc_ref):
    oa_ref[...] = a_ref[...].astype(oa_ref.dtype)
    ob_ref[...] = b_ref[...].astype(ob_ref.dtype)
    oc_ref[...] = c_ref[...].astype(oc_ref.dtype)


def _ffn_weights_bf16(wg, wu, wd, layer):
    steps = CAST_STEPS

    def spec_in(w):
        return pl.BlockSpec((None, w.shape[1] // steps, w.shape[2]), lambda r: (layer, r, 0))

    def spec_out(w):
        return pl.BlockSpec((w.shape[1] // steps, w.shape[2]), lambda r: (r, 0))

    ws = (wg, wu, wd)
    return pl.pallas_call(
        _cast3_kernel,
        grid=(steps,),
        in_specs=[spec_in(w) for w in ws],
        out_specs=[spec_out(w) for w in ws],
        out_shape=[jax.ShapeDtypeStruct(w.shape[1:], BF16) for w in ws],
        compiler_params=_params(("parallel",)),
        name="ffn_weights_bf16",
    )(*ws)


def _in_proj_kernel(x_ref, g_ref, wa_ref, wb_ref, oa_ref, ob_ref, xn_ref):
    j = pl.program_id(1)
    slab = ob_ref.shape[1]

    @pl.when(j == 0)
    def _():
        xn = (_rms(x_ref[...]) * g_ref[...]).astype(BF16)
        xn_ref[...] = xn
        oa_ref[...] = _dot(xn, wa_ref[...]).astype(oa_ref.dtype)

    for s in range(PROJ_B_STEPS):
        @pl.when(j == s + 1)
        def _():
            ob_ref[...] = _dot(xn_ref[...], wb_ref[:, s * slab:(s + 1) * slab])


def _in_proj(x, g, w_a, w_b):
    n, d = x.shape
    tm = min(ROW_TILE, n)
    slab = B_WIDTH // PROJ_B_STEPS
    return pl.pallas_call(
        _in_proj_kernel,
        grid=(n // tm, 1 + PROJ_B_STEPS),
        in_specs=[
            pl.BlockSpec((tm, d), lambda i, j: (i, 0)),
            pl.BlockSpec((1, d), lambda i, j: (0, 0)),
            pl.BlockSpec((d, A_WIDTH), lambda i, j: (0, 0), pipeline_mode=pl.Buffered(1)),
            pl.BlockSpec((d, B_WIDTH), lambda i, j: (0, 0), pipeline_mode=pl.Buffered(1)),
        ],
        out_specs=[
            pl.BlockSpec((tm, A_WIDTH), lambda i, j: (i, 0)),
            pl.BlockSpec((tm, slab), lambda i, j: (i, jnp.maximum(j - 1, 0))),
        ],
        out_shape=[jax.ShapeDtypeStruct((n, A_WIDTH), BF16),
                   jax.ShapeDtypeStruct((n, B_WIDTH), F32)],
        scratch_shapes=[pltpu.VMEM((tm, d), BF16)],
        compiler_params=_params(("parallel", "arbitrary")),
        name="in_proj",
    )(x, g, w_a, w_b)


def _out_proj_kernel(h_ref, a_ref, s_ref, p_ref, w_ref, g_ref, o_ref):
    m = _dot(a_ref[...], w_ref[0:NSA_WIDTH, :])
    m += _dot(s_ref[...], w_ref[NSA_WIDTH:NSA_WIDTH + SSM_WIDTH, :])
    m += _dot(p_ref[...], w_ref[NSA_WIDTH + SSM_WIDTH:, :])
    o_ref[...] = h_ref[...] + _rms(m) * g_ref[...]


def _out_proj(h, o_nsa, o_ssm, o_pool, w, g):
    n, d = h.shape
    tm = min(ROW_TILE, n)
    return pl.pallas_call(
        _out_proj_kernel,
        grid=(n // tm,),
        in_specs=[
            pl.BlockSpec((tm, d), lambda i: (i, 0)),
            pl.BlockSpec((tm, NSA_WIDTH), lambda i: (i, 0)),
            pl.BlockSpec((tm, SSM_WIDTH), lambda i: (i, 0)),
            pl.BlockSpec((tm, POOL_WIDTH), lambda i: (i, 0)),
            pl.BlockSpec((d, d), lambda i: (0, 0)),
            pl.BlockSpec((1, d), lambda i: (0, 0)),
        ],
        out_specs=pl.BlockSpec((tm, d), lambda i: (i, 0)),
        out_shape=jax.ShapeDtypeStruct((n, d), F32),
        compiler_params=_params(("parallel",)),
        name="out_proj",
    )(h, o_nsa, o_ssm, o_pool, w, g)


def _pool_kernel(halo_ref, x_ref, w_ref, scale_ref, o_ref, *, tiles_per_seq):
    tile = x_ref.shape[0]
    it = pl.program_id(0) % tiles_per_seq
    halo = jnp.where(it == 0, 0.0, halo_ref[...])
    ext = jnp.concatenate([halo, x_ref[...]], axis=0)
    pos = it * tile + lax.broadcasted_iota(jnp.int32, (tile, POOL_GROUP), 0)
    for gi, w in enumerate(POOL_WINDOWS):
        xg = ext[:, gi * POOL_GROUP:(gi + 1) * POOL_GROUP]
        s, k = xg, 1
        while k < w:
            s = s + pltpu.roll(s, k, axis=0)
            k *= 2
        cnt = jnp.minimum(pos + 1, w).astype(F32)
        dlt = s[POOL_HALO:] / cnt - xg[POOL_HALO:]
        y = _dot(dlt.astype(BF16), w_ref[gi])
        sl = slice(gi * POOL_GROUP, (gi + 1) * POOL_GROUP)
        o_ref[:, sl] = (y * scale_ref[:, sl]).astype(o_ref.dtype)


def _pool(proj_b, pool_w, pool_scale, seq_len):
    n = proj_b.shape[0]
    tile = min(POOL_TILE, seq_len)
    tiles_per_seq = seq_len // tile
    ratio = tile // POOL_HALO
    col = B_POOL // POOL_WIDTH
    return pl.pallas_call(
        functools.partial(_pool_kernel, tiles_per_seq=tiles_per_seq),
        grid=(n // tile,),
        in_specs=[
            pl.BlockSpec((POOL_HALO, POOL_WIDTH), lambda i: (jnp.maximum(i * ratio - 1, 0), col)),
            pl.BlockSpec((tile, POOL_WIDTH), lambda i: (i, col)),
            pl.BlockSpec((len(POOL_WINDOWS), POOL_GROUP, POOL_GROUP), lambda i: (0, 0, 0)),
            pl.BlockSpec((1, POOL_WIDTH), lambda i: (0, 0)),
        ],
        out_specs=pl.BlockSpec((tile, POOL_WIDTH), lambda i: (i, 0)),
        out_shape=jax.ShapeDtypeStruct((n, POOL_WIDTH), BF16),
        compiler_params=_params(("parallel",)),
        name="pool",
    )(proj_b, proj_b, pool_w, pool_scale)


def _ssd_kernel(halo_ref, xbc_ref, z_ref, dt_ref, convw_ref, convb_ref, dtb_ref, alog_ref,
                dskip_ref, normw_ref, o_ref, state_ref):
    c = pl.program_id(1)

    @pl.when(c == 0)
    def _():
        state_ref[...] = jnp.zeros_like(state_ref)

    for r in range(xbc_ref.shape[0] // SSD_CHUNK):
        rows = slice(r * SSD_CHUNK, (r + 1) * SSD_CHUNK)
        if r == 0:
            halo = jnp.where(c == 0, 0.0, halo_ref[...])
        else:
            halo = xbc_ref[r * SSD_CHUNK - CONV_HALO:r * SSD_CHUNK, :]
        _ssd_chunk(halo, xbc_ref[rows, :], z_ref[rows, :], dt_ref[rows, :], convw_ref, convb_ref, dtb_ref,
                   alog_ref, dskip_ref, normw_ref, o_ref.at[rows, :], state_ref)


def _ssd_chunk(halo, xbc, z, dt_raw, convw_ref, convb_ref, dtb_ref, alog_ref, dskip_ref, normw_ref,
               o_ref, state_ref):
    L = xbc.shape[0]

    ext = jnp.concatenate([halo, xbc], axis=0)
    conv = convb_ref[...] + convw_ref[SSM_CONV - 1:SSM_CONV, :] * xbc
    for k in range(1, SSM_CONV):
        conv += convw_ref[SSM_CONV - 1 - k:SSM_CONV - k, :] * pltpu.roll(ext, k, axis=0)[CONV_HALO:]
    act = _silu(conv)
    xs = act[:, :SSM_WIDTH]
    bm = act[:, SSM_WIDTH:SSM_WIDTH + SSM_GROUPS * SSM_STATE]
    cm = act[:, SSM_WIDTH + SSM_GROUPS * SSM_STATE:]

    lane = lax.broadcasted_iota(jnp.int32, (1, LANES), 1)
    xdt = dt_raw + dtb_ref[...]
    dt = jnp.maximum(xdt, 0.0) + jnp.log1p(jnp.exp(-jnp.abs(xdt)))
    a_head = jnp.where(lane < SSM_HEADS, -jnp.exp(alog_ref[...]), 0.0)
    cs = a_head * dt
    row = lax.broadcasted_iota(jnp.int32, (L, LANES), 0)
    k = 1
    while k < L:
        cs = cs + jnp.where(row >= k, pltpu.roll(cs, k, axis=0), 0.0)
        k *= 2
    cs_t = cs.T

    er = lax.broadcasted_iota(jnp.int32, (LANES, SSM_WIDTH), 0)
    ec = lax.broadcasted_iota(jnp.int32, (LANES, SSM_WIDTH), 1)
    expand = jnp.where(er == ec // SSM_HEAD_DIM, 1.0, 0.0).astype(BF16)
    dtx = _dot_exact_rhs(dt, expand)
    csx = _dot_exact_rhs(cs, expand)
    cs_last = csx[L - 1:L, :]

    x_dt = xs * dtx
    x_end = (x_dt * jnp.exp(cs_last - csx)).astype(BF16)
    x_dt16 = x_dt.astype(BF16)
    ecs = jnp.exp(csx)
    li = lax.broadcasted_iota(jnp.int32, (L, L), 0)
    si = lax.broadcasted_iota(jnp.int32, (L, L), 1)
    gw = SSM_WIDTH // SSM_GROUPS
    hpg = SSM_HEADS // SSM_GROUPS
    lane_g = lax.broadcasted_iota(jnp.int32, (1, gw), 1)
    ys = []
    for g in range(SSM_GROUPS):
        bg = bm[:, g * SSM_STATE:(g + 1) * SSM_STATE]
        cg = cm[:, g * SSM_STATE:(g + 1) * SSM_STATE].astype(BF16)
        gs = slice(g * gw, (g + 1) * gw)
        cb = _dot_nt(cg, bg.astype(BF16))
        y_g = jnp.zeros((L, gw), F32)
        for hh in range(hpg):
            h = g * hpg + hh
            seg = cs[:, h:h + 1] - cs_t[h:h + 1, :]
            m_h = (cb * jnp.exp(jnp.where(li >= si, seg, NEG))).astype(BF16)
            x_h = jnp.where(lane_g // SSM_HEAD_DIM == hh, x_dt16[:, gs], jnp.zeros((), BF16))
            y_g += _dot(m_h, x_h)
        st = state_ref[g]
        y_g += _dot(cg, st.astype(BF16)) * ecs[:, gs]
        state_ref[g] = jnp.exp(cs_last[:, gs]) * st + _dot(bg.T.astype(BF16), x_end[:, gs])
        ys.append(y_g)
    y = jnp.concatenate(ys, axis=1) + dskip_ref[...] * xs
    y = y * _silu(z)
    outs = []
    for g in range(SSM_GROUPS):
        outs.append(_rms(y[:, g * gw:(g + 1) * gw]))
    o_ref[...] = (jnp.concatenate(outs, axis=1) * normw_ref[...]).astype(o_ref.dtype)


def _ssd(proj_b, conv_w, conv_b, dt_bias, a_log, d_skip, norm_w, batch, seq_len):
    n = proj_b.shape[0]
    L = min(SSD_STEP_ROWS, seq_len)
    nch = seq_len // L
    ratio = L // CONV_HALO
    return pl.pallas_call(
        _ssd_kernel,
        grid=(batch, nch),
        in_specs=[
            pl.BlockSpec((CONV_HALO, SSM_CONV_DIM),
                         lambda b, c: (jnp.maximum((b * nch + c) * ratio - 1, 0), B_XBC // SSM_CONV_DIM)),
            pl.BlockSpec((L, SSM_CONV_DIM), lambda b, c: (b * nch + c, B_XBC // SSM_CONV_DIM)),
            pl.BlockSpec((L, SSM_WIDTH), lambda b, c: (b * nch + c, B_Z // SSM_WIDTH)),
            pl.BlockSpec((L, LANES), lambda b, c: (b * nch + c, B_DT // LANES)),
            pl.BlockSpec((SSM_CONV, SSM_CONV_DIM), lambda b, c: (0, 0)),
            pl.BlockSpec((1, SSM_CONV_DIM), lambda b, c: (0, 0)),
            pl.BlockSpec((1, LANES), lambda b, c: (0, 0)),
            pl.BlockSpec((1, LANES), lambda b, c: (0, 0)),
            pl.BlockSpec((1, SSM_WIDTH), lambda b, c: (0, 0)),
            pl.BlockSpec((1, SSM_WIDTH), lambda b, c: (0, 0)),
        ],
        out_specs=pl.BlockSpec((L, SSM_WIDTH), lambda b, c: (b * nch + c, 0)),
        out_shape=jax.ShapeDtypeStruct((n, SSM_WIDTH), BF16),
        scratch_shapes=[pltpu.VMEM((SSM_GROUPS, SSM_STATE, SSM_WIDTH // SSM_GROUPS), F32)],
        compiler_params=_params(("parallel", "arbitrary")),
        name="ssd",
    )(proj_b, proj_b, proj_b, proj_b, conv_w, conv_b, dt_bias, a_log, d_skip, norm_w)


def _rope(x, cos, sin_signed):
    return x * cos + pltpu.roll(x, HEAD_DIM // 2, axis=1) * sin_signed


def _compress(src_ref, w1_ref, w2_ref, pe_ref):
    n_cmp = src_ref.shape[0] // CMP_STRIDE
    xs = [src_ref[pl.ds(p, n_cmp, stride=CMP_STRIDE), :] for p in range(CMP_STRIDE)]
    x = jnp.concatenate(xs, axis=1).astype(BF16)
    half = CMP_STRIDE * HEAD_DIM
    first = _dot(x, w1_ref[0:half, :])
    second = _dot(x, w1_ref[half:2 * half, :])
    pe = jnp.broadcast_to(pe_ref[...], (SUBLANES, 2 * half)).astype(BF16)
    pe_term = _dot(pe, w1_ref[...])[0:1, :]
    pre = first + pltpu.roll(second, n_cmp - 1, axis=0) + pe_term
    return _dot(_silu(pre).astype(BF16), w2_ref[...])


def _nsa_kernel(q_ref, gate_ref, kc_src_ref, vc_src_ref, ks_ref, vs_ref, kw_ref, vw_ref,
                cos_ref, sin_ref, w1k_ref, w2k_ref, w1v_ref, w2v_ref, pek_ref, pev_ref,
                o_ref,
                kc_scr, vct_scr, kx_scr, kwp_scr, vst_scr, vwt_scr, qx_scr, score_scr, rank_scr,
                m_scr, l_scr, acc_scr, oct_scr, owt_scr, s_scr, band_scr):
    tq = q_ref.shape[0]
    seq = ks_ref.shape[0]
    tk = min(SLC_KEY_TILE, seq)
    nh = NSA_HPG
    n_cmp, n_slc = seq // CMP_STRIDE, seq // SLC_LEN
    topk = min(SLC_TOPK, n_slc)
    qi = pl.program_id(2)
    q0 = qi * tq
    qk_scale = HEAD_DIM ** -0.5 * LOG2E

    @pl.when(qi == 0)
    def _():
        kc_scr[...] = _compress(kc_src_ref, w1k_ref, w2k_ref, pek_ref).astype(BF16)
        vct_scr[...] = _compress(vc_src_ref, w1v_ref, w2v_ref, pev_ref).T.astype(BF16)
        kwp_scr[0:WIN_LEN, :] = jnp.zeros((WIN_LEN, HEAD_DIM), BF16)
        for j in range(WIN_LEN // tq):
            vwt_scr[j] = jnp.zeros((HEAD_DIM, tq), BF16)
        kc_i = lax.broadcasted_iota(jnp.int32, (tq + WIN_LEN, 1), 0)
        qr_i = lax.broadcasted_iota(jnp.int32, (1, tq), 1)
        band_scr[...] = jnp.where((kc_i > qr_i) & (kc_i <= qr_i + WIN_LEN), 0.0, NEG)

        def prep_rows(i, carry):
            rows = pl.ds(pl.multiple_of(i * tq, tq), tq)
            prows = pl.ds(pl.multiple_of(WIN_LEN + i * tq, tq), tq)
            cos, sin = cos_ref[rows, :], sin_ref[rows, :]
            kx_scr[rows, 0:HEAD_DIM] = _rope(ks_ref[rows, :].astype(F32), cos, sin).astype(BF16)
            blk = (i * tq + lax.broadcasted_iota(jnp.int32, (tq, LANES), 0)) // SLC_LEN
            lane = lax.broadcasted_iota(jnp.int32, (tq, LANES), 1)
            kx_scr[rows, HEAD_DIM:2 * HEAD_DIM] = jnp.where(lane == blk, 1.0, 0.0).astype(BF16)
            kwp_scr[prows, :] = _rope(kw_ref[rows, :].astype(F32), cos, sin).astype(BF16)
            vst_scr[i] = vs_ref[rows, :].astype(F32).T.astype(BF16)
            vwt_scr[i + WIN_LEN // tq] = vw_ref[rows, :].astype(F32).T.astype(BF16)
            return carry

        lax.fori_loop(0, seq // tq, prep_rows, 0)

    qrows = pl.ds(pl.multiple_of(q0, tq), tq)
    qlane = lax.broadcasted_iota(jnp.int32, (1, tq), 1)
    tpos = q0 + qlane

    def q_rows(h):
        return slice(h * tq, (h + 1) * tq)

    cos_q, sin_q = cos_ref[qrows, :], sin_ref[qrows, :]
    for h in range(nh):
        hs = slice(h * HEAD_DIM, (h + 1) * HEAD_DIM)
        qx_scr[q_rows(h), 0:HEAD_DIM] = (_rope(q_ref[:, hs].astype(F32), cos_q, sin_q) * qk_scale).astype(BF16)

    def heads(x):
        return jnp.concatenate([x] * nh, axis=1)

    q_stack = jnp.concatenate([q_ref[:, h * HEAD_DIM:(h + 1) * HEAD_DIM] for h in range(nh)], axis=0)
    nblk = lax.broadcasted_iota(jnp.int32, (n_cmp, 1), 0)
    cmp_bias = jnp.where((nblk * CMP_STRIDE + (CMP_LEN - 1)) <= tpos, 0.0, NEG)
    any_visible = jnp.where(tpos >= CMP_LEN - 1, 1.0, 0.0)
    s = _dot_nt(kc_scr[...], q_stack) * qk_scale + heads(cmp_bias)
    e = jnp.exp2(s - jnp.max(s, axis=0, keepdims=True))
    p = e * (heads(any_visible) / jnp.sum(e, axis=0, keepdims=True))
    p_sum = p[:, 0:tq]
    for h in range(1, nh):
        p_sum += p[:, h * tq:(h + 1) * tq]
    oct_scr[...] = _dot(vct_scr[...], p.astype(BF16))

    nr = lax.broadcasted_iota(jnp.int32, (n_slc, n_cmp), 1) * CMP_STRIDE
    jr = lax.broadcasted_iota(jnp.int32, (n_slc, n_cmp), 0) * SLC_LEN
    overlap_t = jnp.where((nr < jr + SLC_LEN) & (nr + CMP_LEN > jr), 1.0, 0.0).astype(BF16)
    ps_hi = p_sum.astype(BF16)
    ps_lo = (p_sum - ps_hi.astype(F32)).astype(BF16)
    imp_t = _dot(overlap_t, ps_hi) + _dot(overlap_t, ps_lo)

    jblk = lax.broadcasted_iota(jnp.int32, (n_slc, tq), 0)
    qblk = (q0 + lax.broadcasted_iota(jnp.int32, (n_slc, tq), 1)) // SLC_LEN
    valid = jblk <= qblk
    forced = (jblk == 0) | (jblk == qblk) | (jblk == qblk - 1)
    score_scr[...] = jnp.where(valid, jnp.where(forced, 1e9, imp_t), -1e9)
    rank_scr[...] = jnp.zeros_like(rank_scr)
    ng = n_slc // SUBLANES
    g_last = ((q0 + tq - 1) // SLC_LEN) // SUBLANES
    row8 = lax.broadcasted_iota(jnp.int32, (SUBLANES, tq), 0)
    for gp in range(ng):
        @pl.when(gp <= g_last)
        def _():
            cnt = [jnp.zeros((SUBLANES, tq), F32) for _ in range(ng)]
            for jj in range(SUBLANES):
                jp = gp * SUBLANES + jj
                other = jnp.broadcast_to(score_scr[jp:jp + 1, :], (SUBLANES, tq))
                for g in range(ng):
                    sc = score_scr[g * SUBLANES:(g + 1) * SUBLANES, :]
                    if g > gp:
                        ahead = other >= sc
                    elif g < gp:
                        ahead = other > sc
                    else:
                        ahead = (other > sc) | ((other == sc) & (row8 > jj))
                    cnt[g] += jnp.where(ahead, 1.0, 0.0)
            for g in range(ng):
                rank_scr[g * SUBLANES:(g + 1) * SUBLANES, :] += cnt[g]

    wspan = tq + WIN_LEN
    wrows = pl.ds(pl.multiple_of(q0, tq), wspan)
    c = lax.broadcasted_iota(jnp.int32, (wspan, 1), 0)
    wb = band_scr[...] + jnp.where(c + q0 >= WIN_LEN, 0.0, NEG)
    s = _dot_nt(kwp_scr[wrows, :], qx_scr[:, 0:HEAD_DIM]) + heads(wb)
    p = jnp.exp2(s - jnp.max(s, axis=0, keepdims=True))
    pb = p.astype(BF16)
    o = _dot(vwt_scr[qi], pb[0:tq])
    for j in range(1, wspan // tq):
        o += _dot(vwt_scr[qi + j], pb[j * tq:(j + 1) * tq])
    owt_scr[...] = o / jnp.sum(p, axis=0, keepdims=True)

    sel_t = jnp.where(valid & (rank_scr[...] < topk), 1.0, 0.0).astype(BF16)
    sel_pad = jnp.concatenate([sel_t, jnp.zeros((LANES - n_slc, tq), BF16)], axis=0)
    ri = lax.broadcasted_iota(jnp.int32, (tq, tq), 0)
    ci = lax.broadcasted_iota(jnp.int32, (tq, tq), 1)
    eye = jnp.where(ri == ci, 1.0, 0.0).astype(BF16)
    sel = _dot_nt(eye, sel_pad)
    lane = lax.broadcasted_iota(jnp.int32, (tq, LANES), 1)
    sel_bias = jnp.where((lane < n_slc) & (sel < 0.5), NEG, 0.0).astype(BF16)

    for h in range(nh):
        qx_scr[q_rows(h), HEAD_DIM:2 * HEAD_DIM] = sel_bias
    m_scr[...] = jnp.full(m_scr.shape, NEG, F32)
    l_scr[...] = jnp.zeros_like(l_scr)
    acc_scr[...] = jnp.zeros_like(acc_scr)
    chunks = tk // tq

    def scores(kt):
        return _dot_nt(kx_scr[pl.ds(pl.multiple_of(kt * tk, tk), tk), :], qx_scr[...])

    def softmax_update(s, kt):
        m_prev = m_scr[...]
        m_new = jnp.maximum(m_prev, jnp.max(s, axis=0, keepdims=True))
        p = jnp.exp2(s - m_new)
        alpha = jnp.exp2(m_prev - m_new)
        l_scr[...] = alpha * l_scr[...] + jnp.sum(p, axis=0, keepdims=True)
        pb = p.astype(BF16)
        pv = _dot(vst_scr[kt * chunks], pb[0:tq])
        for j in range(1, chunks):
            pv += _dot(vst_scr[kt * chunks + j], pb[j * tq:(j + 1) * tq])
        acc_scr[...] = alpha * acc_scr[...] + pv
        m_scr[...] = m_new

    n_tiles = (q0 + tq + tk - 1) // tk
    s_scr[...] = scores(0)

    def slc_step(kt):
        s = s_scr[...]
        s_next = scores(kt + 1)
        softmax_update(s, kt)
        s_scr[...] = s_next

    def slc_pair(i, carry):
        slc_step(2 * i)
        slc_step(2 * i + 1)
        return carry

    n_steps = n_tiles - 1
    lax.fori_loop(0, n_steps // 2, slc_pair, 0)

    @pl.when(n_steps % 2 == 1)
    def _():
        slc_step(n_steps - 1)

    kpos = (n_tiles - 1) * tk + lax.broadcasted_iota(jnp.int32, (tk, 1), 0)
    softmax_update(s_scr[...] + heads(jnp.where(kpos <= tpos, 0.0, NEG)), n_tiles - 1)

    gates = _sigmoid(gate_ref[...]).T
    for h in range(nh):
        hs = slice(h * HEAD_DIM, (h + 1) * HEAD_DIM)
        ql = slice(h * tq, (h + 1) * tq)
        o = (gates[3 * h:3 * h + 1] * oct_scr[:, ql]
             + gates[3 * h + 1:3 * h + 2] * (acc_scr[:, ql] / l_scr[:, ql])
             + gates[3 * h + 2:3 * h + 3] * owt_scr[:, ql])
        o_ref[:, hs] = o.T.astype(o_ref.dtype)


def _nsa(proj_a, proj_b, cos, sin_signed, w1k, w2k, w1v, w2v, pe_k, pe_v, batch, seq_len):
    n = proj_a.shape[0]
    tq = min(ATT_TILE, seq_len)
    nq = seq_len // tq
    gw = NSA_HPG * HEAD_DIM
    n_cmp, n_slc = seq_len // CMP_STRIDE, seq_len // SLC_LEN
    assert n_slc <= LANES // 2 and n_slc % SUBLANES == 0 and seq_len % min(SLC_KEY_TILE, seq_len) == 0

    def kv_spec(col0):
        return pl.BlockSpec((seq_len, HEAD_DIM), lambda b, g, i: (b, col0 // HEAD_DIM + g))

    def full(shape):
        return pl.BlockSpec(shape, lambda b, g, i: (0,) * len(shape))

    return pl.pallas_call(
        _nsa_kernel,
        grid=(batch, NSA_GROUPS, nq),
        in_specs=[
            pl.BlockSpec((tq, gw), lambda b, g, i: (b * nq + i, A_Q // gw + g)),
            pl.BlockSpec((tq, LANES), lambda b, g, i: (b * nq + i, B_GATE // LANES + g)),
            kv_spec(B_KC), kv_spec(B_VC),
            kv_spec(A_KS), kv_spec(A_VS), kv_spec(A_KW), kv_spec(A_VW),
            full((seq_len, HEAD_DIM)), full((seq_len, HEAD_DIM)),
            full((CMP_LEN * HEAD_DIM, HEAD_DIM)), full((HEAD_DIM, HEAD_DIM)),
            full((CMP_LEN * HEAD_DIM, HEAD_DIM)), full((HEAD_DIM, HEAD_DIM)),
            full((1, CMP_LEN * HEAD_DIM)), full((1, CMP_LEN * HEAD_DIM)),
        ],
        out_specs=pl.BlockSpec((tq, gw), lambda b, g, i: (b * nq + i, g)),
        out_shape=jax.ShapeDtypeStruct((n, NSA_WIDTH), BF16),
        scratch_shapes=[
            pltpu.VMEM((n_cmp, HEAD_DIM), BF16),
            pltpu.VMEM((HEAD_DIM, n_cmp), BF16),
            pltpu.VMEM((seq_len, 2 * HEAD_DIM), BF16),
            pltpu.VMEM((seq_len + WIN_LEN, HEAD_DIM), BF16),
            pltpu.VMEM((seq_len // tq, HEAD_DIM, tq), BF16),
            pltpu.VMEM(((seq_len + WIN_LEN) // tq, HEAD_DIM, tq), BF16),
            pltpu.VMEM((NSA_HPG * tq, 2 * HEAD_DIM), BF16),
            pltpu.VMEM((n_slc, tq), F32),
            pltpu.VMEM((n_slc, tq), F32),
            pltpu.VMEM((1, NSA_HPG * tq), F32),
            pltpu.VMEM((1, NSA_HPG * tq), F32),
            pltpu.VMEM((HEAD_DIM, NSA_HPG * tq), F32),
            pltpu.VMEM((HEAD_DIM, NSA_HPG * tq), F32),
            pltpu.VMEM((HEAD_DIM, NSA_HPG * tq), F32),
            pltpu.VMEM((min(SLC_KEY_TILE, seq_len), NSA_HPG * tq), F32),
            pltpu.VMEM((tq + WIN_LEN, tq), F32),
        ],
        compiler_params=_params(("parallel", "parallel", "arbitrary")),
        name="nsa",
    )(proj_a, proj_b, proj_b, proj_b, proj_a, proj_a, proj_a, proj_a,
      cos, sin_signed, w1k, w2k, w1v, w2v, pe_k, pe_v)


def _rope_tables(seq_len):
    inv = ROPE_THETA ** (-jnp.arange(0, HEAD_DIM, 2, dtype=F32) / HEAD_DIM)
    ang = jnp.arange(seq_len, dtype=F32)[:, None] * inv[None, :]
    ang = jnp.concatenate([ang, ang], -1)
    sign = jnp.concatenate([-jnp.ones((HEAD_DIM // 2,), F32), jnp.ones((HEAD_DIM // 2,), F32)])
    return jnp.cos(ang), jnp.sin(ang) * sign


def _split_w_in(w_in):
    offs = [0]
    for wd in IN_WIDTHS:
        offs.append(offs[-1] + wd)
    (q, kc, vc, ks, vs, kw, vw, gates, z, xbc, dt, pv) = [
        w_in[:, offs[i]:offs[i + 1]] for i in range(len(IN_WIDTHS))]
    d = w_in.shape[0]

    def pad_to(w, width):
        return jnp.pad(w, ((0, 0), (0, width - w.shape[1])))

    per_group = 3 * NSA_HPG
    w_a = jnp.concatenate([q, ks, vs, kw, vw], axis=1)
    w_b = jnp.concatenate(
        [xbc, z, pv, pad_to(dt, LANES),
         pad_to(gates[:, :per_group], LANES), pad_to(gates[:, per_group:], LANES),
         kc, vc, jnp.zeros((d, B_WIDTH - B_VC - KV_WIDTH), w_in.dtype)], axis=1)
    return w_a.astype(BF16), w_b.astype(BF16)


def _pad_lanes(v):
    return jnp.pad(v, (0, LANES - v.shape[0]))[None, :]


def kernel(x, ffn1_norm_pre, ffn1_norm_post, ffn1_w_gate, ffn1_w_up, ffn1_w_down, mix_norm_pre, mix_norm_post, w_in, cmp_pe_k, cmp_pe_v, cmp_k_w1, cmp_k_w2, cmp_v_w1, cmp_v_w2, ssm_conv_w, ssm_conv_b, ssm_dt_bias, ssm_a_log, ssm_d, ssm_norm, pool_w, pool_scale, w_out, ffn2_norm_pre, ffn2_norm_post, ffn2_w_gate, ffn2_w_up, ffn2_w_down):
    batch, seq_len, d = x.shape
    depth = w_in.shape[0]
    assert d == D_MODEL and ffn1_w_gate.shape[1:] == (D_MODEL, D_FF) and w_in.shape[1:] == (D_MODEL, sum(IN_WIDTHS))
    assert seq_len % max(ATT_TILE, SLC_KEY_TILE, SSD_STEP_ROWS, POOL_TILE) == 0
    assert (batch * seq_len) % max(FFN_ROW_TILE, ROW_TILE) == 0 and D_FF % FFN_TILE == 0
    cos, sin_signed = _rope_tables(seq_len)
    h = x.reshape(batch * seq_len, d)
    for i in range(depth):
        h = _ffn(h, ffn1_norm_pre[i][None], ffn1_norm_post[i][None],
                 *_ffn_weights_bf16(ffn1_w_gate, ffn1_w_up, ffn1_w_down, i))
        proj_a, proj_b = _in_proj(h, mix_norm_pre[i][None], *_split_w_in(w_in[i]))
        o_nsa = _nsa(proj_a, proj_b, cos, sin_signed,
                     cmp_k_w1[i].astype(BF16), cmp_k_w2[i].astype(BF16),
                     cmp_v_w1[i].astype(BF16), cmp_v_w2[i].astype(BF16),
                     cmp_pe_k[i].reshape(1, -1), cmp_pe_v[i].reshape(1, -1), batch, seq_len)
        o_ssm = _ssd(proj_b, ssm_conv_w[i], ssm_conv_b[i][None], _pad_lanes(ssm_dt_bias[i]),
                     _pad_lanes(ssm_a_log[i]), jnp.repeat(ssm_d[i], SSM_HEAD_DIM)[None],
                     ssm_norm[i][None], batch, seq_len)
        o_pool = _pool(proj_b, pool_w[i].astype(BF16), pool_scale[i][None], seq_len)
        h = _out_proj(h, o_nsa, o_ssm, o_pool, w_out[i].astype(BF16), mix_norm_post[i][None])
        h = _ffn(h, ffn2_norm_pre[i][None], ffn2_norm_post[i][None],
                 *_ffn_weights_bf16(ffn2_w_gate, ffn2_w_up, ffn2_w_down, i))
    return h.reshape(batch, seq_len, d)
```

```python
import functools
import math

import jax
import jax.numpy as jnp
from jax import lax
from jax.experimental import pallas as pl
from jax.experimental.pallas import tpu as pltpu

F32 = jnp.float32
BF16 = jnp.bfloat16

D_MODEL = 2048
HEAD_DIM = 128
NSA_WIDTH = 1024
NSA_HEADS = 8
NSA_GROUPS = 2
NSA_HPG = NSA_HEADS // NSA_GROUPS
KV_WIDTH = NSA_GROUPS * HEAD_DIM
CMP_LEN = 32
CMP_STRIDE = 16
SLC_LEN = 64
SLC_TOPK = 16
WIN_LEN = 512
ROPE_THETA = 10000.0
SSM_WIDTH = 512
SSM_HEAD_DIM = 64
SSM_HEADS = 8
SSM_GROUPS = 2
SSM_STATE = 128
SSM_CONV = 4
SSM_CONV_DIM = SSM_WIDTH + 2 * SSM_GROUPS * SSM_STATE
POOL_WIDTH = 512
POOL_WINDOWS = (2, 4, 8, 16)
POOL_GROUP = POOL_WIDTH // len(POOL_WINDOWS)
D_FF = 5632
FFN_RESID = 0.5
RMS_EPS = 1e-6
NEG = -1e30
LOG2E = math.log2(math.e)
IN_WIDTHS = (NSA_WIDTH, KV_WIDTH, KV_WIDTH, KV_WIDTH, KV_WIDTH, KV_WIDTH, KV_WIDTH,
             3 * NSA_HEADS, SSM_WIDTH, SSM_CONV_DIM, SSM_HEADS, POOL_WIDTH)

LANES = 128
SUBLANES = 8
VMEM_LIMIT_BYTES = 60000 * 1024

A_Q, A_KS, A_VS, A_KW, A_VW = 0, 1024, 1280, 1536, 1792
A_WIDTH = 2048
B_XBC, B_Z, B_POOL, B_MISC, B_KC, B_VC = 0, 1024, 1536, 2048, 2176, 2432
B_WIDTH = 2688
GATE_LANE0 = 8

FFN_ROW_TILE = 1024
FFN_ROW_SPLIT = 2
FFN_TILE = 512
CAST_STEPS = 8
ROW_TILE = 512
OUT_ROW_TILE = 1024
OUT_ROW_SPLIT = 2
PROJ_B_STEPS = 1
ATT_TILE = 256
SLC_KEY_TILE = 512
SSD_CHUNK = 128
SSD_STEP_ROWS = 512
POOL_TILE = 1024
POOL_HALO = 16
CONV_HALO = 8


def _params(semantics):
    return pltpu.CompilerParams(dimension_semantics=semantics, vmem_limit_bytes=VMEM_LIMIT_BYTES)


def _sigmoid(x):
    return 1.0 / (1.0 + jnp.exp(-x))


def _silu(x):
    return x * _sigmoid(x)


def _rms(x):
    return x * lax.rsqrt(jnp.mean(x * x, axis=-1, keepdims=True) + RMS_EPS)


def _dot(a, b):
    return jnp.dot(a, b, preferred_element_type=F32)


def _dot_nt(a, b):
    return lax.dot_general(a, b, (((1,), (1,)), ((), ())), preferred_element_type=F32)


def _split3(x):
    hi = x.astype(BF16)
    r = x - hi.astype(F32)
    mid = r.astype(BF16)
    lo = (r - mid.astype(F32)).astype(BF16)
    return hi, mid, lo


def _dot_exact_rhs(x, sel):
    hi, mid, lo = _split3(x)
    return _dot(hi, sel) + _dot(mid, sel) + _dot(lo, sel)


def _ffn_kernel(x_ref, gpre_ref, gpost_ref, wg_ref, wu_ref, wd_ref, o_ref, xn_ref):
    j = pl.program_id(1)
    last = pl.num_programs(1) - 1

    def step(first, final):
        part = x_ref.shape[0] // FFN_ROW_SPLIT
        for r in range(FFN_ROW_SPLIT):
            rows = slice(r * part, (r + 1) * part)
            if first:
                xn = (_rms(x_ref[rows, :]) * gpre_ref[...]).astype(BF16)
                xn_ref[rows, :] = xn
            else:
                xn = xn_ref[rows, :]
            g = _dot(xn, wg_ref[...])
            u = _dot(xn, wu_ref[...])
            a = (_silu(g) * u).astype(BF16)
            acc = _dot(a, wd_ref[...])
            if not first:
                acc += o_ref[rows, :]
            if final:
                o_ref[rows, :] = x_ref[rows, :] + FFN_RESID * (_rms(acc) * gpost_ref[...])
            else:
                o_ref[rows, :] = acc

    pl.when(j == 0)(functools.partial(step, True, False))
    pl.when((j > 0) & (j < last))(functools.partial(step, False, False))
    pl.when(j == last)(functools.partial(step, False, True))


def _ffn(x, gpre, gpost, wg, wu, wd):
    n, d = x.shape
    dff = wg.shape[1]
    tm, tf = min(FFN_ROW_TILE, n), FFN_TILE
    return pl.pallas_call(
        _ffn_kernel,
        grid=(n // tm, dff // tf),
        in_specs=[
            pl.BlockSpec((tm, d), lambda i, j: (i, 0)),
            pl.BlockSpec((1, d), lambda i, j: (0, 0)),
            pl.BlockSpec((1, d), lambda i, j: (0, 0)),
            pl.BlockSpec((d, tf), lambda i, j: (0, j)),
            pl.BlockSpec((d, tf), lambda i, j: (0, j)),
            pl.BlockSpec((tf, d), lambda i, j: (j, 0)),
        ],
        out_specs=pl.BlockSpec((tm, d), lambda i, j: (i, 0)),
        out_shape=jax.ShapeDtypeStruct((n, d), F32),
        scratch_shapes=[pltpu.VMEM((tm, d), BF16)],
        compiler_params=_params(("parallel", "arbitrary")),
        name="ffn",
    )(x, gpre, gpost, wg, wu, wd)


def _cast3_kernel(a_ref, b_ref, c_ref, oa_ref, ob_ref, oc_ref):
    oa_ref[...] = a_ref[...].astype(oa_ref.dtype)
    ob_ref[...] = b_ref[...].astype(ob_ref.dtype)
    oc_ref[...] = c_ref[...].astype(oc_ref.dtype)


def _ffn_weights_bf16(wg, wu, wd, layer):
    steps = CAST_STEPS

    def spec_in(w):
        return pl.BlockSpec((None, w.shape[1] // steps, w.shape[2]), lambda r: (layer, r, 0))

    def spec_out(w):
        return pl.BlockSpec((w.shape[1] // steps, w.shape[2]), lambda r: (r, 0))

    ws = (wg, wu, wd)
    return pl.pallas_call(
        _cast3_kernel,
        grid=(steps,),
        in_specs=[spec_in(w) for w in ws],
        out_specs=[spec_out(w) for w in ws],
        out_shape=[jax.ShapeDtypeStruct(w.shape[1:], BF16) for w in ws],
        compiler_params=_params(("parallel",)),
        name="ffn_weights_bf16",
    )(*ws)


def _in_proj_kernel(x_ref, g_ref, wa_ref, wb_ref, oa_ref, ob_ref, xn_ref):
    j = pl.program_id(1)
    slab = ob_ref.shape[1]

    @pl.when(j == 0)
    def _():
        xn = (_rms(x_ref[...]) * g_ref[...]).astype(BF16)
        xn_ref[...] = xn
        oa_ref[...] = _dot(xn, wa_ref[...]).astype(oa_ref.dtype)

    for s in range(PROJ_B_STEPS):
        @pl.when(j == s + 1)
        def _():
            ob_ref[...] = _dot(xn_ref[...], wb_ref[:, s * slab:(s + 1) * slab])


def _in_proj(x, g, w_a, w_b):
    n, d = x.shape
    tm = min(ROW_TILE, n)
    slab = B_WIDTH // PROJ_B_STEPS
    return pl.pallas_call(
        _in_proj_kernel,
        grid=(n // tm, 1 + PROJ_B_STEPS),
        in_specs=[
            pl.BlockSpec((tm, d), lambda i, j: (i, 0)),
            pl.BlockSpec((1, d), lambda i, j: (0, 0)),
            pl.BlockSpec((d, A_WIDTH), lambda i, j: (0, 0), pipeline_mode=pl.Buffered(1)),
            pl.BlockSpec((d, B_WIDTH), lambda i, j: (0, 0), pipeline_mode=pl.Buffered(1)),
        ],
        out_specs=[
            pl.BlockSpec((tm, A_WIDTH), lambda i, j: (i, 0)),
            pl.BlockSpec((tm, slab), lambda i, j: (i, jnp.maximum(j - 1, 0))),
        ],
        out_shape=[jax.ShapeDtypeStruct((n, A_WIDTH), BF16),
                   jax.ShapeDtypeStruct((n, B_WIDTH), F32)],
        scratch_shapes=[pltpu.VMEM((tm, d), BF16)],
        compiler_params=_params(("parallel", "arbitrary")),
        name="in_proj",
    )(x, g, w_a, w_b)


def _out_proj_kernel(h_ref, a_ref, s_ref, p_ref, w_ref, g_ref, o_ref):
    part = h_ref.shape[0] // OUT_ROW_SPLIT
    for r in range(OUT_ROW_SPLIT):
        rows = slice(r * part, (r + 1) * part)
        m = _dot(a_ref[rows, :], w_ref[0:NSA_WIDTH, :])
        m += _dot(s_ref[rows, :], w_ref[NSA_WIDTH:NSA_WIDTH + SSM_WIDTH, :])
        m += _dot(p_ref[rows, :], w_ref[NSA_WIDTH + SSM_WIDTH:, :])
        o_ref[rows, :] = h_ref[rows, :] + _rms(m) * g_ref[...]


def _out_proj(h, o_nsa, o_ssm, o_pool, w, g):
    n, d = h.shape
    tm = min(OUT_ROW_TILE, n)
    return pl.pallas_call(
        _out_proj_kernel,
        grid=(n // tm,),
        in_specs=[
            pl.BlockSpec((tm, d), lambda i: (i, 0)),
            pl.BlockSpec((tm, NSA_WIDTH), lambda i: (i, 0)),
            pl.BlockSpec((tm, SSM_WIDTH), lambda i: (i, 0)),
            pl.BlockSpec((tm, POOL_WIDTH), lambda i: (i, 0)),
            pl.BlockSpec((d, d), lambda i: (0, 0), pipeline_mode=pl.Buffered(1)),
            pl.BlockSpec((1, d), lambda i: (0, 0)),
        ],
        out_specs=pl.BlockSpec((tm, d), lambda i: (i, 0)),
        out_shape=jax.ShapeDtypeStruct((n, d), F32),
        compiler_params=_params(("parallel",)),
        name="out_proj",
    )(h, o_nsa, o_ssm, o_pool, w, g)


def _pool_kernel(halo_ref, x_ref, w_ref, scale_ref, o_ref, *, tiles_per_seq):
    tile = x_ref.shape[0]
    it = pl.program_id(0) % tiles_per_seq
    halo = jnp.where(it == 0, 0.0, halo_ref[...])
    ext = jnp.concatenate([halo, x_ref[...]], axis=0)
    pos = it * tile + lax.broadcasted_iota(jnp.int32, (tile, POOL_GROUP), 0)
    for gi, w in enumerate(POOL_WINDOWS):
        xg = ext[:, gi * POOL_GROUP:(gi + 1) * POOL_GROUP]
        s, k = xg, 1
        while k < w:
            s = s + pltpu.roll(s, k, axis=0)
            k *= 2
        cnt = jnp.minimum(pos + 1, w).astype(F32)
        dlt = s[POOL_HALO:] / cnt - xg[POOL_HALO:]
        y = _dot(dlt.astype(BF16), w_ref[gi])
        sl = slice(gi * POOL_GROUP, (gi + 1) * POOL_GROUP)
        o_ref[:, sl] = (y * scale_ref[:, sl]).astype(o_ref.dtype)


def _pool(proj_b, pool_w, pool_scale, seq_len):
    n = proj_b.shape[0]
    tile = min(POOL_TILE, seq_len)
    tiles_per_seq = seq_len // tile
    ratio = tile // POOL_HALO
    col = B_POOL // POOL_WIDTH
    return pl.pallas_call(
        functools.partial(_pool_kernel, tiles_per_seq=tiles_per_seq),
        grid=(n // tile,),
        in_specs=[
            pl.BlockSpec((POOL_HALO, POOL_WIDTH), lambda i: (jnp.maximum(i * ratio - 1, 0), col)),
            pl.BlockSpec((tile, POOL_WIDTH), lambda i: (i, col)),
            pl.BlockSpec((len(POOL_WINDOWS), POOL_GROUP, POOL_GROUP), lambda i: (0, 0, 0)),
            pl.BlockSpec((1, POOL_WIDTH), lambda i: (0, 0)),
        ],
        out_specs=pl.BlockSpec((tile, POOL_WIDTH), lambda i: (i, 0)),
        out_shape=jax.ShapeDtypeStruct((n, POOL_WIDTH), BF16),
        compiler_params=_params(("parallel",)),
        name="pool",
    )(proj_b, proj_b, pool_w, pool_scale)


def _ssd_kernel(halo_ref, xbc_ref, z_ref, dt_ref, convw_ref, convb_ref, dtb_ref, alog_ref,
                dskip_ref, normw_ref, o_ref, state_ref):
    c = pl.program_id(1)

    @pl.when(c == 0)
    def _():
        state_ref[...] = jnp.zeros_like(state_ref)

    for r in range(xbc_ref.shape[0] // SSD_CHUNK):
        rows = slice(r * SSD_CHUNK, (r + 1) * SSD_CHUNK)
        if r == 0:
            halo = jnp.where(c == 0, 0.0, halo_ref[...])
        else:
            halo = xbc_ref[r * SSD_CHUNK - CONV_HALO:r * SSD_CHUNK, :]
        _ssd_chunk(halo, xbc_ref[rows, :], z_ref[rows, :], dt_ref[rows, :], convw_ref, convb_ref, dtb_ref,
                   alog_ref, dskip_ref, normw_ref, o_ref.at[rows, :], state_ref)


def _ssd_chunk(halo, xbc, z, dt_raw, convw_ref, convb_ref, dtb_ref, alog_ref, dskip_ref, normw_ref,
               o_ref, state_ref):
    L = xbc.shape[0]

    ext = jnp.concatenate([halo, xbc], axis=0)
    conv = convb_ref[...] + convw_ref[SSM_CONV - 1:SSM_CONV, :] * xbc
    for k in range(1, SSM_CONV):
        conv += convw_ref[SSM_CONV - 1 - k:SSM_CONV - k, :] * pltpu.roll(ext, k, axis=0)[CONV_HALO:]
    act = _silu(conv)
    xs = act[:, :SSM_WIDTH]
    bm = act[:, SSM_WIDTH:SSM_WIDTH + SSM_GROUPS * SSM_STATE]
    cm = act[:, SSM_WIDTH + SSM_GROUPS * SSM_STATE:]

    lane = lax.broadcasted_iota(jnp.int32, (1, LANES), 1)
    xdt = dt_raw + dtb_ref[...]
    dt = jnp.maximum(xdt, 0.0) + jnp.log1p(jnp.exp(-jnp.abs(xdt)))
    a_head = jnp.where(lane < SSM_HEADS, -jnp.exp(alog_ref[...]), 0.0)
    cs = a_head * dt
    row = lax.broadcasted_iota(jnp.int32, (L, LANES), 0)
    k = 1
    while k < L:
        cs = cs + jnp.where(row >= k, pltpu.roll(cs, k, axis=0), 0.0)
        k *= 2
    cs_t = cs.T

    er = lax.broadcasted_iota(jnp.int32, (LANES, SSM_WIDTH), 0)
    ec = lax.broadcasted_iota(jnp.int32, (LANES, SSM_WIDTH), 1)
    expand = jnp.where(er == ec // SSM_HEAD_DIM, 1.0, 0.0).astype(BF16)
    dtx = _dot_exact_rhs(dt, expand)
    csx = _dot_exact_rhs(cs, expand)
    cs_last = csx[L - 1:L, :]

    x_dt = xs * dtx
    x_end = (x_dt * jnp.exp(cs_last - csx)).astype(BF16)
    x_dt16 = x_dt.astype(BF16)
    ecs = jnp.exp(csx)
    li = lax.broadcasted_iota(jnp.int32, (L, L), 0)
    si = lax.broadcasted_iota(jnp.int32, (L, L), 1)
    gw = SSM_WIDTH // SSM_GROUPS
    hpg = SSM_HEADS // SSM_GROUPS
    lane_g = lax.broadcasted_iota(jnp.int32, (1, gw), 1)
    ys = []
    for g in range(SSM_GROUPS):
        bg = bm[:, g * SSM_STATE:(g + 1) * SSM_STATE]
        cg = cm[:, g * SSM_STATE:(g + 1) * SSM_STATE].astype(BF16)
        gs = slice(g * gw, (g + 1) * gw)
        cb = _dot_nt(cg, bg.astype(BF16))
        y_g = jnp.zeros((L, gw), F32)
        for hh in range(hpg):
            h = g * hpg + hh
            seg = cs[:, h:h + 1] - cs_t[h:h + 1, :]
            m_h = (cb * jnp.exp(jnp.where(li >= si, seg, NEG))).astype(BF16)
            x_h = jnp.where(lane_g // SSM_HEAD_DIM == hh, x_dt16[:, gs], jnp.zeros((), BF16))
            y_g += _dot(m_h, x_h)
        st = state_ref[g]
        y_g += _dot(cg, st.astype(BF16)) * ecs[:, gs]
        state_ref[g] = jnp.exp(cs_last[:, gs]) * st + _dot(bg.T.astype(BF16), x_end[:, gs])
        ys.append(y_g)
    y = jnp.concatenate(ys, axis=1) + dskip_ref[...] * xs
    y = y * _silu(z)
    outs = []
    for g in range(SSM_GROUPS):
        outs.append(_rms(y[:, g * gw:(g + 1) * gw]))
    o_ref[...] = (jnp.concatenate(outs, axis=1) * normw_ref[...]).astype(o_ref.dtype)


def _ssd(proj_b, conv_w, conv_b, dt_bias, a_log, d_skip, norm_w, batch, seq_len):
    n = proj_b.shape[0]
    L = min(SSD_STEP_ROWS, seq_len)
    nch = seq_len // L
    ratio = L // CONV_HALO
    return pl.pallas_call(
        _ssd_kernel,
        grid=(batch, nch),
        in_specs=[
            pl.BlockSpec((CONV_HALO, SSM_CONV_DIM),
                         lambda b, c: (jnp.maximum((b * nch + c) * ratio - 1, 0), B_XBC // SSM_CONV_DIM)),
            pl.BlockSpec((L, SSM_CONV_DIM), lambda b, c: (b * nch + c, B_XBC // SSM_CONV_DIM)),
            pl.BlockSpec((L, SSM_WIDTH), lambda b, c: (b * nch + c, B_Z // SSM_WIDTH)),
            pl.BlockSpec((L, LANES), lambda b, c: (b * nch + c, B_MISC // LANES)),
            pl.BlockSpec((SSM_CONV, SSM_CONV_DIM), lambda b, c: (0, 0)),
            pl.BlockSpec((1, SSM_CONV_DIM), lambda b, c: (0, 0)),
            pl.BlockSpec((1, LANES), lambda b, c: (0, 0)),
            pl.BlockSpec((1, LANES), lambda b, c: (0, 0)),
            pl.BlockSpec((1, SSM_WIDTH), lambda b, c: (0, 0)),
            pl.BlockSpec((1, SSM_WIDTH), lambda b, c: (0, 0)),
        ],
        out_specs=pl.BlockSpec((L, SSM_WIDTH), lambda b, c: (b * nch + c, 0)),
        out_shape=jax.ShapeDtypeStruct((n, SSM_WIDTH), BF16),
        scratch_shapes=[pltpu.VMEM((SSM_GROUPS, SSM_STATE, SSM_WIDTH // SSM_GROUPS), F32)],
        compiler_params=_params(("parallel", "arbitrary")),
        name="ssd",
    )(proj_b, proj_b, proj_b, proj_b, conv_w, conv_b, dt_bias, a_log, d_skip, norm_w)


def _rope(x, cos, sin_signed):
    return x * cos + pltpu.roll(x, HEAD_DIM // 2, axis=1) * sin_signed


def _compress(src_ref, w1_ref, w2_ref, pe_ref):
    n_cmp = src_ref.shape[0] // CMP_STRIDE
    xs = [src_ref[pl.ds(p, n_cmp, stride=CMP_STRIDE), :] for p in range(CMP_STRIDE)]
    x = jnp.concatenate(xs, axis=1).astype(BF16)
    half = CMP_STRIDE * HEAD_DIM
    first = _dot(x, w1_ref[0:half, :])
    second = _dot(x, w1_ref[half:2 * half, :])
    pe = jnp.broadcast_to(pe_ref[...], (SUBLANES, 2 * half)).astype(BF16)
    pe_term = _dot(pe, w1_ref[...])[0:1, :]
    pre = first + pltpu.roll(second, n_cmp - 1, axis=0) + pe_term
    return _dot(_silu(pre).astype(BF16), w2_ref[...])


def _nsa_kernel(q_ref, gate_ref, kc_src_ref, vc_src_ref, ks_ref, vs_ref, kw_ref, vw_ref,
                cos_ref, sin_ref, w1k_ref, w2k_ref, w1v_ref, w2v_ref, pek_ref, pev_ref,
                o_ref,
                kc_scr, vct_scr, kx_scr, kwp_scr, vst_scr, vwt_scr, qx_scr, score_scr, rank_scr,
                m_scr, l_scr, acc_scr, oct_scr, owt_scr, s_scr, band_scr):
    tq = q_ref.shape[0]
    seq = ks_ref.shape[0]
    tk = min(SLC_KEY_TILE, seq)
    nh = NSA_HPG
    n_cmp, n_slc = seq // CMP_STRIDE, seq // SLC_LEN
    topk = min(SLC_TOPK, n_slc)
    qi = pl.program_id(2)
    q0 = qi * tq
    qk_scale = HEAD_DIM ** -0.5 * LOG2E

    @pl.when(qi == 0)
    def _():
        kc_scr[...] = _compress(kc_src_ref, w1k_ref, w2k_ref, pek_ref).astype(BF16)
        vct_scr[...] = _compress(vc_src_ref, w1v_ref, w2v_ref, pev_ref).T.astype(BF16)
        kwp_scr[0:WIN_LEN, :] = jnp.zeros((WIN_LEN, HEAD_DIM), BF16)
        for j in range(WIN_LEN // tq):
            vwt_scr[j] = jnp.zeros((HEAD_DIM, tq), BF16)
        kc_i = lax.broadcasted_iota(jnp.int32, (tq + WIN_LEN, 1), 0)
        qr_i = lax.broadcasted_iota(jnp.int32, (1, tq), 1)
        band_scr[...] = jnp.where((kc_i > qr_i) & (kc_i <= qr_i + WIN_LEN), 0.0, NEG)

        def prep_rows(i, carry):
            rows = pl.ds(pl.multiple_of(i * tq, tq), tq)
            prows = pl.ds(pl.multiple_of(WIN_LEN + i * tq, tq), tq)
            cos, sin = cos_ref[rows, :], sin_ref[rows, :]
            kx_scr[rows, 0:HEAD_DIM] = _rope(ks_ref[rows, :].astype(F32), cos, sin).astype(BF16)
            blk = (i * tq + lax.broadcasted_iota(jnp.int32, (tq, LANES), 0)) // SLC_LEN
            lane = lax.broadcasted_iota(jnp.int32, (tq, LANES), 1)
            kx_scr[rows, HEAD_DIM:2 * HEAD_DIM] = jnp.where(lane == blk, 1.0, 0.0).astype(BF16)
            kwp_scr[prows, :] = _rope(kw_ref[rows, :].astype(F32), cos, sin).astype(BF16)
            vst_scr[i] = vs_ref[rows, :].astype(F32).T.astype(BF16)
            vwt_scr[i + WIN_LEN // tq] = vw_ref[rows, :].astype(F32).T.astype(BF16)
            return carry

        lax.fori_loop(0, seq // tq, prep_rows, 0)

    qrows = pl.ds(pl.multiple_of(q0, tq), tq)
    qlane = lax.broadcasted_iota(jnp.int32, (1, tq), 1)
    tpos = q0 + qlane

    def q_rows(h):
        return slice(h * tq, (h + 1) * tq)

    cos_q, sin_q = cos_ref[qrows, :], sin_ref[qrows, :]
    for h in range(nh):
        hs = slice(h * HEAD_DIM, (h + 1) * HEAD_DIM)
        qx_scr[q_rows(h), 0:HEAD_DIM] = (_rope(q_ref[:, hs].astype(F32), cos_q, sin_q) * qk_scale).astype(BF16)

    def heads(x):
        return jnp.concatenate([x] * nh, axis=1)

    q_stack = jnp.concatenate([q_ref[:, h * HEAD_DIM:(h + 1) * HEAD_DIM] for h in range(nh)], axis=0)
    nblk = lax.broadcasted_iota(jnp.int32, (n_cmp, 1), 0)
    cmp_bias = jnp.where((nblk * CMP_STRIDE + (CMP_LEN - 1)) <= tpos, 0.0, NEG)
    any_visible = jnp.where(tpos >= CMP_LEN - 1, 1.0, 0.0)
    s = _dot_nt(kc_scr[...], q_stack) * qk_scale + heads(cmp_bias)
    e = jnp.exp2(s - jnp.max(s, axis=0, keepdims=True))
    p = e * (heads(any_visible) / jnp.sum(e, axis=0, keepdims=True))
    p_sum = p[:, 0:tq]
    for h in range(1, nh):
        p_sum += p[:, h * tq:(h + 1) * tq]
    oct_scr[...] = _dot(vct_scr[...], p.astype(BF16))

    nr = lax.broadcasted_iota(jnp.int32, (n_slc, n_cmp), 1) * CMP_STRIDE
    jr = lax.broadcasted_iota(jnp.int32, (n_slc, n_cmp), 0) * SLC_LEN
    overlap_t = jnp.where((nr < jr + SLC_LEN) & (nr + CMP_LEN > jr), 1.0, 0.0).astype(BF16)
    ps_hi = p_sum.astype(BF16)
    ps_lo = (p_sum - ps_hi.astype(F32)).astype(BF16)
    imp_t = _dot(overlap_t, ps_hi) + _dot(overlap_t, ps_lo)

    jblk = lax.broadcasted_iota(jnp.int32, (n_slc, tq), 0)
    qblk = (q0 + lax.broadcasted_iota(jnp.int32, (n_slc, tq), 1)) // SLC_LEN
    valid = jblk <= qblk
    forced = (jblk == 0) | (jblk == qblk) | (jblk == qblk - 1)
    score_scr[...] = jnp.where(valid, jnp.where(forced, 1e9, imp_t), -1e9)
    rank_scr[...] = jnp.zeros_like(rank_scr)
    ng = n_slc // SUBLANES
    g_last = ((q0 + tq - 1) // SLC_LEN) // SUBLANES
    row8 = lax.broadcasted_iota(jnp.int32, (SUBLANES, tq), 0)
    for gp in range(ng):
        @pl.when(gp <= g_last)
        def _():
            cnt = [jnp.zeros((SUBLANES, tq), F32) for _ in range(ng)]
            for jj in range(SUBLANES):
                jp = gp * SUBLANES + jj
                other = jnp.broadcast_to(score_scr[jp:jp + 1, :], (SUBLANES, tq))
                for g in range(ng):
                    sc = score_scr[g * SUBLANES:(g + 1) * SUBLANES, :]
                    if g > gp:
                        ahead = other >= sc
                    elif g < gp:
                        ahead = other > sc
                    else:
                        ahead = (other > sc) | ((other == sc) & (row8 > jj))
                    cnt[g] += jnp.where(ahead, 1.0, 0.0)
            for g in range(ng):
                rank_scr[g * SUBLANES:(g + 1) * SUBLANES, :] += cnt[g]

    wspan = tq + WIN_LEN
    wrows = pl.ds(pl.multiple_of(q0, tq), wspan)
    c = lax.broadcasted_iota(jnp.int32, (wspan, 1), 0)
    wb = band_scr[...] + jnp.where(c + q0 >= WIN_LEN, 0.0, NEG)
    s = _dot_nt(kwp_scr[wrows, :], qx_scr[:, 0:HEAD_DIM]) + heads(wb)
    p = jnp.exp2(s - jnp.max(s, axis=0, keepdims=True))
    pb = p.astype(BF16)
    o = _dot(vwt_scr[qi], pb[0:tq])
    for j in range(1, wspan // tq):
        o += _dot(vwt_scr[qi + j], pb[j * tq:(j + 1) * tq])
    owt_scr[...] = o / jnp.sum(p, axis=0, keepdims=True)

    sel_t = jnp.where(valid & (rank_scr[...] < topk), 1.0, 0.0).astype(BF16)
    sel_pad = jnp.concatenate([sel_t, jnp.zeros((LANES - n_slc, tq), BF16)], axis=0)
    ri = lax.broadcasted_iota(jnp.int32, (tq, tq), 0)
    ci = lax.broadcasted_iota(jnp.int32, (tq, tq), 1)
    eye = jnp.where(ri == ci, 1.0, 0.0).astype(BF16)
    sel = _dot_nt(eye, sel_pad)
    lane = lax.broadcasted_iota(jnp.int32, (tq, LANES), 1)
    sel_bias = jnp.where((lane < n_slc) & (sel < 0.5), NEG, 0.0).astype(BF16)

    for h in range(nh):
        qx_scr[q_rows(h), HEAD_DIM:2 * HEAD_DIM] = sel_bias
    m_scr[...] = jnp.full(m_scr.shape, NEG, F32)
    l_scr[...] = jnp.zeros_like(l_scr)
    acc_scr[...] = jnp.zeros_like(acc_scr)
    chunks = tk // tq

    def scores(kt):
        return _dot_nt(kx_scr[pl.ds(pl.multiple_of(kt * tk, tk), tk), :], qx_scr[...])

    def softmax_update(s, kt):
        m_prev = m_scr[...]
        m_new = jnp.maximum(m_prev, jnp.max(s, axis=0, keepdims=True))
        p = jnp.exp2(s - m_new)
        alpha = jnp.exp2(m_prev - m_new)
        l_scr[...] = alpha * l_scr[...] + jnp.sum(p, axis=0, keepdims=True)
        pb = p.astype(BF16)
        pv = _dot(vst_scr[kt * chunks], pb[0:tq])
        for j in range(1, chunks):
            pv += _dot(vst_scr[kt * chunks + j], pb[j * tq:(j + 1) * tq])
        acc_scr[...] = alpha * acc_scr[...] + pv
        m_scr[...] = m_new

    n_tiles = (q0 + tq + tk - 1) // tk
    s_scr[...] = scores(0)

    def slc_step(kt):
        s = s_scr[...]
        s_next = scores(kt + 1)
        softmax_update(s, kt)
        s_scr[...] = s_next

    def slc_pair(i, carry):
        slc_step(2 * i)
        slc_step(2 * i + 1)
        return carry

    n_steps = n_tiles - 1
    lax.fori_loop(0, n_steps // 2, slc_pair, 0)

    @pl.when(n_steps % 2 == 1)
    def _():
        slc_step(n_steps - 1)

    kpos = (n_tiles - 1) * tk + lax.broadcasted_iota(jnp.int32, (tk, 1), 0)
    softmax_update(s_scr[...] + heads(jnp.where(kpos <= tpos, 0.0, NEG)), n_tiles - 1)

    gates_all = _sigmoid(gate_ref[...]).T
    per_group = 3 * nh
    gates = jnp.where(pl.program_id(1) == 0,
                      gates_all[GATE_LANE0:GATE_LANE0 + per_group],
                      gates_all[GATE_LANE0 + per_group:GATE_LANE0 + 2 * per_group])
    for h in range(nh):
        hs = slice(h * HEAD_DIM, (h + 1) * HEAD_DIM)
        ql = slice(h * tq, (h + 1) * tq)
        o = (gates[3 * h:3 * h + 1] * oct_scr[:, ql]
             + gates[3 * h + 1:3 * h + 2] * (acc_scr[:, ql] / l_scr[:, ql])
             + gates[3 * h + 2:3 * h + 3] * owt_scr[:, ql])
        o_ref[:, hs] = o.T.astype(o_ref.dtype)


def _nsa(proj_a, proj_b, cos, sin_signed, w1k, w2k, w1v, w2v, pe_k, pe_v, batch, seq_len):
    n = proj_a.shape[0]
    tq = min(ATT_TILE, seq_len)
    nq = seq_len // tq
    gw = NSA_HPG * HEAD_DIM
    n_cmp, n_slc = seq_len // CMP_STRIDE, seq_len // SLC_LEN
    assert n_slc <= LANES // 2 and n_slc % SUBLANES == 0 and seq_len % min(SLC_KEY_TILE, seq_len) == 0

    def kv_spec(col0):
        return pl.BlockSpec((seq_len, HEAD_DIM), lambda b, g, i: (b, col0 // HEAD_DIM + g))

    def full(shape):
        return pl.BlockSpec(shape, lambda b, g, i: (0,) * len(shape))

    return pl.pallas_call(
        _nsa_kernel,
        grid=(batch, NSA_GROUPS, nq),
        in_specs=[
            pl.BlockSpec((tq, gw), lambda b, g, i: (b * nq + i, A_Q // gw + g)),
            pl.BlockSpec((tq, LANES), lambda b, g, i: (b * nq + i, B_MISC // LANES)),
            kv_spec(B_KC), kv_spec(B_VC),
            kv_spec(A_KS), kv_spec(A_VS), kv_spec(A_KW), kv_spec(A_VW),
            full((seq_len, HEAD_DIM)), full((seq_len, HEAD_DIM)),
            full((CMP_LEN * HEAD_DIM, HEAD_DIM)), full((HEAD_DIM, HEAD_DIM)),
            full((CMP_LEN * HEAD_DIM, HEAD_DIM)), full((HEAD_DIM, HEAD_DIM)),
            full((1, CMP_LEN * HEAD_DIM)), full((1, CMP_LEN * HEAD_DIM)),
        ],
        out_specs=pl.BlockSpec((tq, gw), lambda b, g, i: (b * nq + i, g)),
        out_shape=jax.ShapeDtypeStruct((n, NSA_WIDTH), BF16),
        scratch_shapes=[
            pltpu.VMEM((n_cmp, HEAD_DIM), BF16),
            pltpu.VMEM((HEAD_DIM, n_cmp), BF16),
            pltpu.VMEM((seq_len, 2 * HEAD_DIM), BF16),
            pltpu.VMEM((seq_len + WIN_LEN, HEAD_DIM), BF16),
            pltpu.VMEM((seq_len // tq, HEAD_DIM, tq), BF16),
            pltpu.VMEM(((seq_len + WIN_LEN) // tq, HEAD_DIM, tq), BF16),
            pltpu.VMEM((NSA_HPG * tq, 2 * HEAD_DIM), BF16),
            pltpu.VMEM((n_slc, tq), F32),
            pltpu.VMEM((n_slc, tq), F32),
            pltpu.VMEM((1, NSA_HPG * tq), F32),
            pltpu.VMEM((1, NSA_HPG * tq), F32),
            pltpu.VMEM((HEAD_DIM, NSA_HPG * tq), F32),
            pltpu.VMEM((HEAD_DIM, NSA_HPG * tq), F32),
            pltpu.VMEM((HEAD_DIM, NSA_HPG * tq), F32),
            pltpu.VMEM((min(SLC_KEY_TILE, seq_len), NSA_HPG * tq), F32),
            pltpu.VMEM((tq + WIN_LEN, tq), F32),
        ],
        compiler_params=_params(("parallel", "parallel", "arbitrary")),
        name="nsa",
    )(proj_a, proj_b, proj_b, proj_b, proj_a, proj_a, proj_a, proj_a,
      cos, sin_signed, w1k, w2k, w1v, w2v, pe_k, pe_v)


def _rope_tables(seq_len):
    inv = ROPE_THETA ** (-jnp.arange(0, HEAD_DIM, 2, dtype=F32) / HEAD_DIM)
    ang = jnp.arange(seq_len, dtype=F32)[:, None] * inv[None, :]
    ang = jnp.concatenate([ang, ang], -1)
    sign = jnp.concatenate([-jnp.ones((HEAD_DIM // 2,), F32), jnp.ones((HEAD_DIM // 2,), F32)])
    return jnp.cos(ang), jnp.sin(ang) * sign


def _split_w_in(w_in):
    offs = [0]
    for wd in IN_WIDTHS:
        offs.append(offs[-1] + wd)
    (q, kc, vc, ks, vs, kw, vw, gates, z, xbc, dt, pv) = [
        w_in[:, offs[i]:offs[i + 1]] for i in range(len(IN_WIDTHS))]
    d = w_in.shape[0]

    def pad_to(w, width):
        return jnp.pad(w, ((0, 0), (0, width - w.shape[1])))

    w_a = jnp.concatenate([q, ks, vs, kw, vw], axis=1)
    w_b = jnp.concatenate([xbc, z, pv, pad_to(jnp.concatenate([dt, gates], axis=1), LANES), kc, vc], axis=1)
    return w_a.astype(BF16), w_b.astype(BF16)


def _pad_lanes(v):
    return jnp.pad(v, (0, LANES - v.shape[0]))[None, :]


def kernel(x, ffn1_norm_pre, ffn1_norm_post, ffn1_w_gate, ffn1_w_up, ffn1_w_down, mix_norm_pre, mix_norm_post, w_in, cmp_pe_k, cmp_pe_v, cmp_k_w1, cmp_k_w2, cmp_v_w1, cmp_v_w2, ssm_conv_w, ssm_conv_b, ssm_dt_bias, ssm_a_log, ssm_d, ssm_norm, pool_w, pool_scale, w_out, ffn2_norm_pre, ffn2_norm_post, ffn2_w_gate, ffn2_w_up, ffn2_w_down):
    batch, seq_len, d = x.shape
    depth = w_in.shape[0]
    assert d == D_MODEL and ffn1_w_gate.shape[1:] == (D_MODEL, D_FF) and w_in.shape[1:] == (D_MODEL, sum(IN_WIDTHS))
    assert seq_len % max(ATT_TILE, SLC_KEY_TILE, SSD_STEP_ROWS, POOL_TILE) == 0
    assert (batch * seq_len) % max(FFN_ROW_TILE, ROW_TILE, OUT_ROW_TILE) == 0 and D_FF % FFN_TILE == 0
    cos, sin_signed = _rope_tables(seq_len)
    h = x.reshape(batch * seq_len, d)
    for i in range(depth):
        h = _ffn(h, ffn1_norm_pre[i][None], ffn1_norm_post[i][None],
                 *_ffn_weights_bf16(ffn1_w_gate, ffn1_w_up, ffn1_w_down, i))
        proj_a, proj_b = _in_proj(h, mix_norm_pre[i][None], *_split_w_in(w_in[i]))
        o_nsa = _nsa(proj_a, proj_b, cos, sin_signed,
                     cmp_k_w1[i].astype(BF16), cmp_k_w2[i].astype(BF16),
                     cmp_v_w1[i].astype(BF16), cmp_v_w2[i].astype(BF16),
                     cmp_pe_k[i].reshape(1, -1), cmp_pe_v[i].reshape(1, -1), batch, seq_len)
        o_ssm = _ssd(proj_b, ssm_conv_w[i], ssm_conv_b[i][None], _pad_lanes(ssm_dt_bias[i]),
                     _pad_lanes(ssm_a_log[i]), jnp.repeat(ssm_d[i], SSM_HEAD_DIM)[None],
                     ssm_norm[i][None], batch, seq_len)
        o_pool = _pool(proj_b, pool_w[i].astype(BF16), pool_scale[i][None], seq_len)
        h = _out_proj(h, o_nsa, o_ssm, o_pool, w_out[i].astype(BF16), mix_norm_post[i][None])
        h = _ffn(h, ffn2_norm_pre[i][None], ffn2_norm_post[i][None],
                 *_ffn_weights_bf16(ffn2_w_gate, ffn2_w_up, ffn2_w_down, i))
    return h.reshape(batch, seq_len, d)
```

```python
import functools
import math

import jax
import jax.numpy as jnp
from jax import lax
from jax.experimental import pallas as pl
from jax.experimental.pallas import tpu as pltpu

F32 = jnp.float32
BF16 = jnp.bfloat16

D_MODEL = 2048
HEAD_DIM = 128
NSA_WIDTH = 1024
NSA_HEADS = 8
NSA_GROUPS = 2
NSA_HPG = NSA_HEADS // NSA_GROUPS
KV_WIDTH = NSA_GROUPS * HEAD_DIM
CMP_LEN = 32
CMP_STRIDE = 16
SLC_LEN = 64
SLC_TOPK = 16
WIN_LEN = 512
ROPE_THETA = 10000.0
SSM_WIDTH = 512
SSM_HEAD_DIM = 64
SSM_HEADS = 8
SSM_GROUPS = 2
SSM_STATE = 128
SSM_CONV = 4
SSM_CONV_DIM = SSM_WIDTH + 2 * SSM_GROUPS * SSM_STATE
POOL_WIDTH = 512
POOL_WINDOWS = (2, 4, 8, 16)
POOL_GROUP = POOL_WIDTH // len(POOL_WINDOWS)
D_FF = 5632
FFN_RESID = 0.5
RMS_EPS = 1e-6
NEG = -1e30
LOG2E = math.log2(math.e)
IN_WIDTHS = (NSA_WIDTH, KV_WIDTH, KV_WIDTH, KV_WIDTH, KV_WIDTH, KV_WIDTH, KV_WIDTH,
             3 * NSA_HEADS, SSM_WIDTH, SSM_CONV_DIM, SSM_HEADS, POOL_WIDTH)

LANES = 128
SUBLANES = 8
VMEM_LIMIT_BYTES = 60000 * 1024

A_Q, A_KS, A_VS, A_KW, A_VW = 0, 1024, 1280, 1536, 1792
A_WIDTH = 2048
B_XBC, B_Z, B_POOL, B_MISC, B_KC, B_VC = 0, 1024, 1536, 2048, 2176, 2432
B_WIDTH = 2688
GATE_LANE0 = SSM_HEADS

FFN_ROW_TILE = 1024
FFN_ROW_SPLIT = 2
FFN_TILE = 512
CAST_STEPS = 8
ROW_TILE = 512
OUT_ROW_TILE = 1024
OUT_ROW_SPLIT = 2
PROJ_B_STEPS = 1
ATT_TILE = 256
SLC_KEY_TILE = 512
SSD_CHUNK = 128
SSD_STEP_ROWS = 1024
POOL_TILE = 2048
POOL_HALO = 16
CONV_HALO = 8


def _params(semantics):
    return pltpu.CompilerParams(dimension_semantics=semantics, vmem_limit_bytes=VMEM_LIMIT_BYTES)


def _sigmoid(x):
    return 1.0 / (1.0 + jnp.exp(-x))


def _silu(x):
    return x * _sigmoid(x)


def _rms(x):
    return x * lax.rsqrt(jnp.mean(x * x, axis=-1, keepdims=True) + RMS_EPS)


def _dot(a, b):
    return jnp.dot(a, b, preferred_element_type=F32)


def _dot_nt(a, b):
    return lax.dot_general(a, b, (((1,), (1,)), ((), ())), preferred_element_type=F32)


def _split3(x):
    hi = x.astype(BF16)
    r = x - hi.astype(F32)
    mid = r.astype(BF16)
    lo = (r - mid.astype(F32)).astype(BF16)
    return hi, mid, lo


def _dot_exact_rhs(x, sel):
    hi, mid, lo = _split3(x)
    return _dot(hi, sel) + _dot(mid, sel) + _dot(lo, sel)


def _ffn_kernel(x_ref, gpre_ref, gpost_ref, wg_ref, wu_ref, wd_ref, o_ref, xn_ref):
    j = pl.program_id(1)
    last = pl.num_programs(1) - 1

    def step(first, final):
        part = x_ref.shape[0] // FFN_ROW_SPLIT
        for r in range(FFN_ROW_SPLIT):
            rows = slice(r * part, (r + 1) * part)
            if first:
                xn = (_rms(x_ref[rows, :]) * gpre_ref[...]).astype(BF16)
                xn_ref[rows, :] = xn
            else:
                xn = xn_ref[rows, :]
            g = _dot(xn, wg_ref[...])
            u = _dot(xn, wu_ref[...])
            a = (_silu(g) * u).astype(BF16)
            acc = _dot(a, wd_ref[...])
            if not first:
                acc += o_ref[rows, :]
            if final:
                o_ref[rows, :] = x_ref[rows, :] + FFN_RESID * (_rms(acc) * gpost_ref[...])
            else:
                o_ref[rows, :] = acc

    pl.when(j == 0)(functools.partial(step, True, False))
    pl.when((j > 0) & (j < last))(functools.partial(step, False, False))
    pl.when(j == last)(functools.partial(step, False, True))


def _ffn(x, gpre, gpost, wg, wu, wd):
    n, d = x.shape
    dff = wg.shape[1]
    tm, tf = min(FFN_ROW_TILE, n), FFN_TILE
    return pl.pallas_call(
        _ffn_kernel,
        grid=(n // tm, dff // tf),
        in_specs=[
            pl.BlockSpec((tm, d), lambda i, j: (i, 0)),
            pl.BlockSpec((1, d), lambda i, j: (0, 0)),
            pl.BlockSpec((1, d), lambda i, j: (0, 0)),
            pl.BlockSpec((d, tf), lambda i, j: (0, j)),
            pl.BlockSpec((d, tf), lambda i, j: (0, j)),
            pl.BlockSpec((tf, d), lambda i, j: (j, 0)),
        ],
        out_specs=pl.BlockSpec((tm, d), lambda i, j: (i, 0)),
        out_shape=jax.ShapeDtypeStruct((n, d), F32),
        scratch_shapes=[pltpu.VMEM((tm, d), BF16)],
        compiler_params=_params(("parallel", "arbitrary")),
        name="ffn",
    )(x, gpre, gpost, wg, wu, wd)


def _cast3_kernel(a_ref, b_ref, c_ref, oa_ref, ob_ref, oc_ref):
    oa_ref[...] = a_ref[...].astype(oa_ref.dtype)
    ob_ref[...] = b_ref[...].astype(ob_ref.dtype)
    oc_ref[...] = c_ref[...].astype(oc_ref.dtype)


def _ffn_weights_bf16(wg, wu, wd, layer):
    steps = CAST_STEPS

    def spec_in(w):
        return pl.BlockSpec((None, w.shape[1] // steps, w.shape[2]), lambda r: (layer, r, 0))

    def spec_out(w):
        return pl.BlockSpec((w.shape[1] // steps, w.shape[2]), lambda r: (r, 0))

    ws = (wg, wu, wd)
    return pl.pallas_call(
        _cast3_kernel,
        grid=(steps,),
        in_specs=[spec_in(w) for w in ws],
        out_specs=[spec_out(w) for w in ws],
        out_shape=[jax.ShapeDtypeStruct(w.shape[1:], BF16) for w in ws],
        compiler_params=_params(("parallel",)),
        name="ffn_weights_bf16",
    )(*ws)


def _in_proj_kernel(x_ref, g_ref, wa_ref, wb_ref, oa_ref, ob_ref, xn_ref):
    j = pl.program_id(1)
    slab = ob_ref.shape[1]

    @pl.when(j == 0)
    def _():
        xn = (_rms(x_ref[...]) * g_ref[...]).astype(BF16)
        xn_ref[...] = xn
        oa_ref[...] = _dot(xn, wa_ref[...]).astype(oa_ref.dtype)

    for s in range(PROJ_B_STEPS):
        @pl.when(j == s + 1)
        def _():
            ob_ref[...] = _dot(xn_ref[...], wb_ref[:, s * slab:(s + 1) * slab])


def _in_proj(x, g, w_a, w_b):
    n, d = x.shape
    tm = min(ROW_TILE, n)
    slab = B_WIDTH // PROJ_B_STEPS
    return pl.pallas_call(
        _in_proj_kernel,
        grid=(n // tm, 1 + PROJ_B_STEPS),
        in_specs=[
            pl.BlockSpec((tm, d), lambda i, j: (i, 0)),
            pl.BlockSpec((1, d), lambda i, j: (0, 0)),
            pl.BlockSpec((d, A_WIDTH), lambda i, j: (0, 0), pipeline_mode=pl.Buffered(1)),
            pl.BlockSpec((d, B_WIDTH), lambda i, j: (0, 0), pipeline_mode=pl.Buffered(1)),
        ],
        out_specs=[
            pl.BlockSpec((tm, A_WIDTH), lambda i, j: (i, 0)),
            pl.BlockSpec((tm, slab), lambda i, j: (i, jnp.maximum(j - 1, 0))),
        ],
        out_shape=[jax.ShapeDtypeStruct((n, A_WIDTH), BF16),
                   jax.ShapeDtypeStruct((n, B_WIDTH), F32)],
        scratch_shapes=[pltpu.VMEM((tm, d), BF16)],
        compiler_params=_params(("parallel", "arbitrary")),
        name="in_proj",
    )(x, g, w_a, w_b)


def _out_proj_kernel(h_ref, a_ref, s_ref, p_ref, w_ref, g_ref, o_ref):
    part = h_ref.shape[0] // OUT_ROW_SPLIT
    for r in range(OUT_ROW_SPLIT):
        rows = slice(r * part, (r + 1) * part)
        m = _dot(a_ref[rows, :], w_ref[0:NSA_WIDTH, :])
        m += _dot(s_ref[rows, :], w_ref[NSA_WIDTH:NSA_WIDTH + SSM_WIDTH, :])
        m += _dot(p_ref[rows, :], w_ref[NSA_WIDTH + SSM_WIDTH:, :])
        o_ref[rows, :] = h_ref[rows, :] + _rms(m) * g_ref[...]


def _out_proj(h, o_nsa, o_ssm, o_pool, w, g):
    n, d = h.shape
    tm = min(OUT_ROW_TILE, n)
    return pl.pallas_call(
        _out_proj_kernel,
        grid=(n // tm,),
        in_specs=[
            pl.BlockSpec((tm, d), lambda i: (i, 0)),
            pl.BlockSpec((tm, NSA_WIDTH), lambda i: (i, 0)),
            pl.BlockSpec((tm, SSM_WIDTH), lambda i: (i, 0)),
            pl.BlockSpec((tm, POOL_WIDTH), lambda i: (i, 0)),
            pl.BlockSpec((d, d), lambda i: (0, 0), pipeline_mode=pl.Buffered(1)),
            pl.BlockSpec((1, d), lambda i: (0, 0)),
        ],
        out_specs=pl.BlockSpec((tm, d), lambda i: (i, 0)),
        out_shape=jax.ShapeDtypeStruct((n, d), F32),
        compiler_params=_params(("parallel",)),
        name="out_proj",
    )(h, o_nsa, o_ssm, o_pool, w, g)


def _pool_kernel(halo_ref, x_ref, w_ref, scale_ref, o_ref, *, tiles_per_seq):
    tile = x_ref.shape[0]
    it = pl.program_id(0) % tiles_per_seq
    halo = jnp.where(it == 0, 0.0, halo_ref[...])
    ext = jnp.concatenate([halo, x_ref[...]], axis=0)
    pos = it * tile + lax.broadcasted_iota(jnp.int32, (tile, POOL_GROUP), 0)
    for gi, w in enumerate(POOL_WINDOWS):
        xg = ext[:, gi * POOL_GROUP:(gi + 1) * POOL_GROUP]
        s, k = xg, 1
        while k < w:
            s = s + pltpu.roll(s, k, axis=0)
            k *= 2
        cnt = jnp.minimum(pos + 1, w).astype(F32)
        dlt = s[POOL_HALO:] / cnt - xg[POOL_HALO:]
        y = _dot(dlt.astype(BF16), w_ref[gi])
        sl = slice(gi * POOL_GROUP, (gi + 1) * POOL_GROUP)
        o_ref[:, sl] = (y * scale_ref[:, sl]).astype(o_ref.dtype)


def _pool(proj_b, pool_w, pool_scale, seq_len):
    n = proj_b.shape[0]
    tile = min(POOL_TILE, seq_len)
    tiles_per_seq = seq_len // tile
    ratio = tile // POOL_HALO
    col = B_POOL // POOL_WIDTH
    return pl.pallas_call(
        functools.partial(_pool_kernel, tiles_per_seq=tiles_per_seq),
        grid=(n // tile,),
        in_specs=[
            pl.BlockSpec((POOL_HALO, POOL_WIDTH), lambda i: (jnp.maximum(i * ratio - 1, 0), col)),
            pl.BlockSpec((tile, POOL_WIDTH), lambda i: (i, col)),
            pl.BlockSpec((len(POOL_WINDOWS), POOL_GROUP, POOL_GROUP), lambda i: (0, 0, 0)),
            pl.BlockSpec((1, POOL_WIDTH), lambda i: (0, 0)),
        ],
        out_specs=pl.BlockSpec((tile, POOL_WIDTH), lambda i: (i, 0)),
        out_shape=jax.ShapeDtypeStruct((n, POOL_WIDTH), BF16),
        compiler_params=_params(("parallel",)),
        name="pool",
    )(proj_b, proj_b, pool_w, pool_scale)


def _ssd_kernel(halo_ref, xbc_ref, z_ref, dt_ref, convw_ref, convb_ref, dtb_ref, alog_ref,
                dskip_ref, normw_ref, o_ref, state_ref):
    c = pl.program_id(1)

    @pl.when(c == 0)
    def _():
        state_ref[...] = jnp.zeros_like(state_ref)

    for r in range(xbc_ref.shape[0] // SSD_CHUNK):
        rows = slice(r * SSD_CHUNK, (r + 1) * SSD_CHUNK)
        if r == 0:
            halo = jnp.where(c == 0, 0.0, halo_ref[...])
        else:
            halo = xbc_ref[r * SSD_CHUNK - CONV_HALO:r * SSD_CHUNK, :]
        _ssd_chunk(halo, xbc_ref[rows, :], z_ref[rows, :], dt_ref[rows, :], convw_ref, convb_ref, dtb_ref,
                   alog_ref, dskip_ref, normw_ref, o_ref.at[rows, :], state_ref)


def _ssd_chunk(halo, xbc, z, dt_raw, convw_ref, convb_ref, dtb_ref, alog_ref, dskip_ref, normw_ref,
               o_ref, state_ref):
    L = xbc.shape[0]

    ext = jnp.concatenate([halo, xbc], axis=0)
    conv = convb_ref[...] + convw_ref[SSM_CONV - 1:SSM_CONV, :] * xbc
    for k in range(1, SSM_CONV):
        conv += convw_ref[SSM_CONV - 1 - k:SSM_CONV - k, :] * pltpu.roll(ext, k, axis=0)[CONV_HALO:]
    act = _silu(conv)
    xs = act[:, :SSM_WIDTH]
    bm = act[:, SSM_WIDTH:SSM_WIDTH + SSM_GROUPS * SSM_STATE]
    cm = act[:, SSM_WIDTH + SSM_GROUPS * SSM_STATE:]

    lane = lax.broadcasted_iota(jnp.int32, (1, LANES), 1)
    xdt = dt_raw + dtb_ref[...]
    dt = jnp.maximum(xdt, 0.0) + jnp.log1p(jnp.exp(-jnp.abs(xdt)))
    a_head = jnp.where(lane < SSM_HEADS, -jnp.exp(alog_ref[...]), 0.0)
    cs = a_head * dt
    row = lax.broadcasted_iota(jnp.int32, (L, LANES), 0)
    k = 1
    while k < L:
        cs = cs + jnp.where(row >= k, pltpu.roll(cs, k, axis=0), 0.0)
        k *= 2
    cs_t = cs.T

    er = lax.broadcasted_iota(jnp.int32, (LANES, SSM_WIDTH), 0)
    ec = lax.broadcasted_iota(jnp.int32, (LANES, SSM_WIDTH), 1)
    expand = jnp.where(er == ec // SSM_HEAD_DIM, 1.0, 0.0).astype(BF16)
    dtx = _dot_exact_rhs(dt, expand)
    csx = _dot_exact_rhs(cs, expand)
    cs_last = csx[L - 1:L, :]

    x_dt = xs * dtx
    x_end = (x_dt * jnp.exp(cs_last - csx)).astype(BF16)
    x_dt16 = x_dt.astype(BF16)
    ecs = jnp.exp(csx)
    li = lax.broadcasted_iota(jnp.int32, (L, L), 0)
    si = lax.broadcasted_iota(jnp.int32, (L, L), 1)
    gw = SSM_WIDTH // SSM_GROUPS
    hpg = SSM_HEADS // SSM_GROUPS
    lane_g = lax.broadcasted_iota(jnp.int32, (1, gw), 1)
    ys = []
    for g in range(SSM_GROUPS):
        bg = bm[:, g * SSM_STATE:(g + 1) * SSM_STATE]
        cg = cm[:, g * SSM_STATE:(g + 1) * SSM_STATE].astype(BF16)
        gs = slice(g * gw, (g + 1) * gw)
        cb = _dot_nt(cg, bg.astype(BF16))
        y_g = jnp.zeros((L, gw), F32)
        for hh in range(hpg):
            h = g * hpg + hh
            seg = cs[:, h:h + 1] - cs_t[h:h + 1, :]
            m_h = (cb * jnp.exp(jnp.where(li >= si, seg, NEG))).astype(BF16)
            x_h = jnp.where(lane_g // SSM_HEAD_DIM == hh, x_dt16[:, gs], jnp.zeros((), BF16))
            y_g += _dot(m_h, x_h)
        st = state_ref[g]
        y_g += _dot(cg, st.astype(BF16)) * ecs[:, gs]
        state_ref[g] = jnp.exp(cs_last[:, gs]) * st + _dot(bg.T.astype(BF16), x_end[:, gs])
        ys.append(y_g)
    y = jnp.concatenate(ys, axis=1) + dskip_ref[...] * xs
    y = y * _silu(z)
    outs = []
    for g in range(SSM_GROUPS):
        outs.append(_rms(y[:, g * gw:(g + 1) * gw]))
    o_ref[...] = (jnp.concatenate(outs, axis=1) * normw_ref[...]).astype(o_ref.dtype)


def _ssd(proj_b, conv_w, conv_b, dt_bias, a_log, d_skip, norm_w, batch, seq_len):
    n = proj_b.shape[0]
    L = min(SSD_STEP_ROWS, seq_len)
    nch = seq_len // L
    ratio = L // CONV_HALO
    return pl.pallas_call(
        _ssd_kernel,
        grid=(batch, nch),
        in_specs=[
            pl.BlockSpec((CONV_HALO, SSM_CONV_DIM),
                         lambda b, c: (jnp.maximum((b * nch + c) * ratio - 1, 0), B_XBC // SSM_CONV_DIM)),
            pl.BlockSpec((L, SSM_CONV_DIM), lambda b, c: (b * nch + c, B_XBC // SSM_CONV_DIM)),
            pl.BlockSpec((L, SSM_WIDTH), lambda b, c: (b * nch + c, B_Z // SSM_WIDTH)),
            pl.BlockSpec((L, LANES), lambda b, c: (b * nch + c, B_MISC // LANES)),
            pl.BlockSpec((SSM_CONV, SSM_CONV_DIM), lambda b, c: (0, 0)),
            pl.BlockSpec((1, SSM_CONV_DIM), lambda b, c: (0, 0)),
            pl.BlockSpec((1, LANES), lambda b, c: (0, 0)),
            pl.BlockSpec((1, LANES), lambda b, c: (0, 0)),
            pl.BlockSpec((1, SSM_WIDTH), lambda b, c: (0, 0)),
            pl.BlockSpec((1, SSM_WIDTH), lambda b, c: (0, 0)),
        ],
        out_specs=pl.BlockSpec((L, SSM_WIDTH), lambda b, c: (b * nch + c, 0)),
        out_shape=jax.ShapeDtypeStruct((n, SSM_WIDTH), BF16),
        scratch_shapes=[pltpu.VMEM((SSM_GROUPS, SSM_STATE, SSM_WIDTH // SSM_GROUPS), F32)],
        compiler_params=_params(("parallel", "arbitrary")),
        name="ssd",
    )(proj_b, proj_b, proj_b, proj_b, conv_w, conv_b, dt_bias, a_log, d_skip, norm_w)


def _rope(x, cos, sin_signed):
    return x * cos + pltpu.roll(x, HEAD_DIM // 2, axis=1) * sin_signed


def _compress(src_ref, w1_ref, w2_ref, pe_ref):
    n_cmp = src_ref.shape[0] // CMP_STRIDE
    xs = [src_ref[pl.ds(p, n_cmp, stride=CMP_STRIDE), :] for p in range(CMP_STRIDE)]
    x = jnp.concatenate(xs, axis=1).astype(BF16)
    half = CMP_STRIDE * HEAD_DIM
    first = _dot(x, w1_ref[0:half, :])
    second = _dot(x, w1_ref[half:2 * half, :])
    pe = jnp.broadcast_to(pe_ref[...], (SUBLANES, 2 * half)).astype(BF16)
    pe_term = _dot(pe, w1_ref[...])[0:1, :]
    pre = first + pltpu.roll(second, n_cmp - 1, axis=0) + pe_term
    return _dot(_silu(pre).astype(BF16), w2_ref[...])


def _nsa_kernel(q_ref, gate_ref, kc_src_ref, vc_src_ref, ks_ref, vs_ref, kw_ref, vw_ref,
                cos_ref, sin_ref, w1k_ref, w2k_ref, w1v_ref, w2v_ref, pek_ref, pev_ref,
                o_ref,
                kc_scr, vct_scr, kx_scr, kwp_scr, vst_scr, vwt_scr, qx_scr, score_scr, rank_scr,
                m_scr, l_scr, acc_scr, oct_scr, owt_scr, s_scr, band_scr):
    tq = q_ref.shape[0]
    seq = ks_ref.shape[0]
    tk = min(SLC_KEY_TILE, seq)
    nh = NSA_HPG
    n_cmp, n_slc = seq // CMP_STRIDE, seq // SLC_LEN
    topk = min(SLC_TOPK, n_slc)
    qi = pl.program_id(2)
    q0 = qi * tq
    qk_scale = HEAD_DIM ** -0.5 * LOG2E

    @pl.when(qi == 0)
    def _():
        kc_scr[...] = _compress(kc_src_ref, w1k_ref, w2k_ref, pek_ref).astype(BF16)
        vct_scr[...] = _compress(vc_src_ref, w1v_ref, w2v_ref, pev_ref).T.astype(BF16)
        kwp_scr[0:WIN_LEN, :] = jnp.zeros((WIN_LEN, HEAD_DIM), BF16)
        for j in range(WIN_LEN // tq):
            vwt_scr[j] = jnp.zeros((HEAD_DIM, tq), BF16)
        kc_i = lax.broadcasted_iota(jnp.int32, (tq + WIN_LEN, 1), 0)
        qr_i = lax.broadcasted_iota(jnp.int32, (1, tq), 1)
        band_scr[...] = jnp.where((kc_i > qr_i) & (kc_i <= qr_i + WIN_LEN), 0.0, NEG)

        def prep_rows(i, carry):
            rows = pl.ds(pl.multiple_of(i * tq, tq), tq)
            prows = pl.ds(pl.multiple_of(WIN_LEN + i * tq, tq), tq)
            cos, sin = cos_ref[rows, :], sin_ref[rows, :]
            kx_scr[rows, 0:HEAD_DIM] = _rope(ks_ref[rows, :].astype(F32), cos, sin).astype(BF16)
            blk = (i * tq + lax.broadcasted_iota(jnp.int32, (tq, LANES), 0)) // SLC_LEN
            lane = lax.broadcasted_iota(jnp.int32, (tq, LANES), 1)
            kx_scr[rows, HEAD_DIM:2 * HEAD_DIM] = jnp.where(lane == blk, 1.0, 0.0).astype(BF16)
            kwp_scr[prows, :] = _rope(kw_ref[rows, :].astype(F32), cos, sin).astype(BF16)
            vst_scr[i] = vs_ref[rows, :].astype(F32).T.astype(BF16)
            vwt_scr[i + WIN_LEN // tq] = vw_ref[rows, :].astype(F32).T.astype(BF16)
            return carry

        lax.fori_loop(0, seq // tq, prep_rows, 0)

    qrows = pl.ds(pl.multiple_of(q0, tq), tq)
    qlane = lax.broadcasted_iota(jnp.int32, (1, tq), 1)
    tpos = q0 + qlane

    def q_rows(h):
        return slice(h * tq, (h + 1) * tq)

    cos_q, sin_q = cos_ref[qrows, :], sin_ref[qrows, :]
    for h in range(nh):
        hs = slice(h * HEAD_DIM, (h + 1) * HEAD_DIM)
        qx_scr[q_rows(h), 0:HEAD_DIM] = (_rope(q_ref[:, hs].astype(F32), cos_q, sin_q) * qk_scale).astype(BF16)

    def heads(x):
        return jnp.concatenate([x] * nh, axis=1)

    q_stack = jnp.concatenate([q_ref[:, h * HEAD_DIM:(h + 1) * HEAD_DIM] for h in range(nh)], axis=0)
    nblk = lax.broadcasted_iota(jnp.int32, (n_cmp, 1), 0)
    cmp_bias = jnp.where((nblk * CMP_STRIDE + (CMP_LEN - 1)) <= tpos, 0.0, NEG)
    any_visible = jnp.where(tpos >= CMP_LEN - 1, 1.0, 0.0)
    s = _dot_nt(kc_scr[...], q_stack) * qk_scale + heads(cmp_bias)
    e = jnp.exp2(s - jnp.max(s, axis=0, keepdims=True))
    p = e * (heads(any_visible) / jnp.sum(e, axis=0, keepdims=True))
    p_sum = p[:, 0:tq]
    for h in range(1, nh):
        p_sum += p[:, h * tq:(h + 1) * tq]
    oct_scr[...] = _dot(vct_scr[...], p.astype(BF16))

    nr = lax.broadcasted_iota(jnp.int32, (n_slc, n_cmp), 1) * CMP_STRIDE
    jr = lax.broadcasted_iota(jnp.int32, (n_slc, n_cmp), 0) * SLC_LEN
    overlap_t = jnp.where((nr < jr + SLC_LEN) & (nr + CMP_LEN > jr), 1.0, 0.0).astype(BF16)
    ps_hi = p_sum.astype(BF16)
    ps_lo = (p_sum - ps_hi.astype(F32)).astype(BF16)
    imp_t = _dot(overlap_t, ps_hi) + _dot(overlap_t, ps_lo)

    jblk = lax.broadcasted_iota(jnp.int32, (n_slc, tq), 0)
    qblk = (q0 + lax.broadcasted_iota(jnp.int32, (n_slc, tq), 1)) // SLC_LEN
    valid = jblk <= qblk
    forced = (jblk == 0) | (jblk == qblk) | (jblk == qblk - 1)
    score_scr[...] = jnp.where(valid, jnp.where(forced, 1e9, imp_t), -1e9)
    rank_scr[...] = jnp.zeros_like(rank_scr)
    ng = n_slc // SUBLANES
    g_last = ((q0 + tq - 1) // SLC_LEN) // SUBLANES
    row8 = lax.broadcasted_iota(jnp.int32, (SUBLANES, tq), 0)
    for gp in range(ng):
        @pl.when(gp <= g_last)
        def _():
            cnt = [jnp.zeros((SUBLANES, tq), F32) for _ in range(ng)]
            for jj in range(SUBLANES):
                jp = gp * SUBLANES + jj
                other = jnp.broadcast_to(score_scr[jp:jp + 1, :], (SUBLANES, tq))
                for g in range(ng):
                    sc = score_scr[g * SUBLANES:(g + 1) * SUBLANES, :]
                    if g > gp:
                        ahead = other >= sc
                    elif g < gp:
                        ahead = other > sc
                    else:
                        ahead = (other > sc) | ((other == sc) & (row8 > jj))
                    cnt[g] += jnp.where(ahead, 1.0, 0.0)
            for g in range(ng):
                rank_scr[g * SUBLANES:(g + 1) * SUBLANES, :] += cnt[g]

    wspan = tq + WIN_LEN
    wrows = pl.ds(pl.multiple_of(q0, tq), wspan)
    c = lax.broadcasted_iota(jnp.int32, (wspan, 1), 0)
    wb = band_scr[...] + jnp.where(c + q0 >= WIN_LEN, 0.0, NEG)
    s = _dot_nt(kwp_scr[wrows, :], qx_scr[:, 0:HEAD_DIM]) + heads(wb)
    p = jnp.exp2(s - jnp.max(s, axis=0, keepdims=True))
    pb = p.astype(BF16)
    o = _dot(vwt_scr[qi], pb[0:tq])
    for j in range(1, wspan // tq):
        o += _dot(vwt_scr[qi + j], pb[j * tq:(j + 1) * tq])
    owt_scr[...] = o / jnp.sum(p, axis=0, keepdims=True)

    sel_t = jnp.where(valid & (rank_scr[...] < topk), 1.0, 0.0).astype(BF16)
    sel_pad = jnp.concatenate([sel_t, jnp.zeros((LANES - n_slc, tq), BF16)], axis=0)
    ri = lax.broadcasted_iota(jnp.int32, (tq, tq), 0)
    ci = lax.broadcasted_iota(jnp.int32, (tq, tq), 1)
    eye = jnp.where(ri == ci, 1.0, 0.0).astype(BF16)
    sel = _dot_nt(eye, sel_pad)
    lane = lax.broadcasted_iota(jnp.int32, (tq, LANES), 1)
    sel_bias = jnp.where((lane < n_slc) & (sel < 0.5), NEG, 0.0).astype(BF16)

    for h in range(nh):
        qx_scr[q_rows(h), HEAD_DIM:2 * HEAD_DIM] = sel_bias
    m_scr[...] = jnp.full(m_scr.shape, NEG, F32)
    l_scr[...] = jnp.zeros_like(l_scr)
    acc_scr[...] = jnp.zeros_like(acc_scr)
    chunks = tk // tq

    def scores(kt):
        return _dot_nt(kx_scr[pl.ds(pl.multiple_of(kt * tk, tk), tk), :], qx_scr[...])

    def softmax_update(s, kt):
        m_prev = m_scr[...]
        m_new = jnp.maximum(m_prev, jnp.max(s, axis=0, keepdims=True))
        p = jnp.exp2(s - m_new)
        alpha = jnp.exp2(m_prev - m_new)
        l_scr[...] = alpha * l_scr[...] + jnp.sum(p, axis=0, keepdims=True)
        pb = p.astype(BF16)
        pv = _dot(vst_scr[kt * chunks], pb[0:tq])
        for j in range(1, chunks):
            pv += _dot(vst_scr[kt * chunks + j], pb[j * tq:(j + 1) * tq])
        acc_scr[...] = alpha * acc_scr[...] + pv
        m_scr[...] = m_new

    n_tiles = (q0 + tq + tk - 1) // tk
    s_scr[...] = scores(0)

    def slc_step(kt):
        s = s_scr[...]
        s_next = scores(kt + 1)
        softmax_update(s, kt)
        s_scr[...] = s_next

    def slc_pair(i, carry):
        slc_step(2 * i)
        slc_step(2 * i + 1)
        return carry

    n_steps = n_tiles - 1
    lax.fori_loop(0, n_steps // 2, slc_pair, 0)

    @pl.when(n_steps % 2 == 1)
    def _():
        slc_step(n_steps - 1)

    kpos = (n_tiles - 1) * tk + lax.broadcasted_iota(jnp.int32, (tk, 1), 0)
    softmax_update(s_scr[...] + heads(jnp.where(kpos <= tpos, 0.0, NEG)), n_tiles - 1)

    gates_all = _sigmoid(gate_ref[...]).T
    per_group = 3 * nh
    gates = jnp.where(pl.program_id(1) == 0,
                      gates_all[GATE_LANE0:GATE_LANE0 + per_group],
                      gates_all[GATE_LANE0 + per_group:GATE_LANE0 + 2 * per_group])
    for h in range(nh):
        hs = slice(h * HEAD_DIM, (h + 1) * HEAD_DIM)
        ql = slice(h * tq, (h + 1) * tq)
        o = (gates[3 * h:3 * h + 1] * oct_scr[:, ql]
             + gates[3 * h + 1:3 * h + 2] * (acc_scr[:, ql] / l_scr[:, ql])
             + gates[3 * h + 2:3 * h + 3] * owt_scr[:, ql])
        o_ref[:, hs] = o.T.astype(o_ref.dtype)


def _nsa(proj_a, proj_b, cos, sin_signed, w1k, w2k, w1v, w2v, pe_k, pe_v, batch, seq_len):
    n = proj_a.shape[0]
    tq = min(ATT_TILE, seq_len)
    nq = seq_len // tq
    gw = NSA_HPG * HEAD_DIM
    n_cmp, n_slc = seq_len // CMP_STRIDE, seq_len // SLC_LEN
    assert n_slc <= LANES // 2 and n_slc % SUBLANES == 0 and seq_len % min(SLC_KEY_TILE, seq_len) == 0

    def kv_spec(col0):
        return pl.BlockSpec((seq_len, HEAD_DIM), lambda b, g, i: (b, col0 // HEAD_DIM + g))

    def full(shape):
        return pl.BlockSpec(shape, lambda b, g, i: (0,) * len(shape))

    return pl.pallas_call(
        _nsa_kernel,
        grid=(batch, NSA_GROUPS, nq),
        in_specs=[
            pl.BlockSpec((tq, gw), lambda b, g, i: (b * nq + i, A_Q // gw + g)),
            pl.BlockSpec((tq, LANES), lambda b, g, i: (b * nq + i, B_MISC // LANES)),
            kv_spec(B_KC), kv_spec(B_VC),
            kv_spec(A_KS), kv_spec(A_VS), kv_spec(A_KW), kv_spec(A_VW),
            full((seq_len, HEAD_DIM)), full((seq_len, HEAD_DIM)),
            full((CMP_LEN * HEAD_DIM, HEAD_DIM)), full((HEAD_DIM, HEAD_DIM)),
            full((CMP_LEN * HEAD_DIM, HEAD_DIM)), full((HEAD_DIM, HEAD_DIM)),
            full((1, CMP_LEN * HEAD_DIM)), full((1, CMP_LEN * HEAD_DIM)),
        ],
        out_specs=pl.BlockSpec((tq, gw), lambda b, g, i: (b * nq + i, g)),
        out_shape=jax.ShapeDtypeStruct((n, NSA_WIDTH), BF16),
        scratch_shapes=[
            pltpu.VMEM((n_cmp, HEAD_DIM), BF16),
            pltpu.VMEM((HEAD_DIM, n_cmp), BF16),
            pltpu.VMEM((seq_len, 2 * HEAD_DIM), BF16),
            pltpu.VMEM((seq_len + WIN_LEN, HEAD_DIM), BF16),
            pltpu.VMEM((seq_len // tq, HEAD_DIM, tq), BF16),
            pltpu.VMEM(((seq_len + WIN_LEN) // tq, HEAD_DIM, tq), BF16),
            pltpu.VMEM((NSA_HPG * tq, 2 * HEAD_DIM), BF16),
            pltpu.VMEM((n_slc, tq), F32),
            pltpu.VMEM((n_slc, tq), F32),
            pltpu.VMEM((1, NSA_HPG * tq), F32),
            pltpu.VMEM((1, NSA_HPG * tq), F32),
            pltpu.VMEM((HEAD_DIM, NSA_HPG * tq), F32),
            pltpu.VMEM((HEAD_DIM, NSA_HPG * tq), F32),
            pltpu.VMEM((HEAD_DIM, NSA_HPG * tq), F32),
            pltpu.VMEM((min(SLC_KEY_TILE, seq_len), NSA_HPG * tq), F32),
            pltpu.VMEM((tq + WIN_LEN, tq), F32),
        ],
        compiler_params=_params(("parallel", "parallel", "arbitrary")),
        name="nsa",
    )(proj_a, proj_b, proj_b, proj_b, proj_a, proj_a, proj_a, proj_a,
      cos, sin_signed, w1k, w2k, w1v, w2v, pe_k, pe_v)


def _rope_tables(seq_len):
    inv = ROPE_THETA ** (-jnp.arange(0, HEAD_DIM, 2, dtype=F32) / HEAD_DIM)
    ang = jnp.arange(seq_len, dtype=F32)[:, None] * inv[None, :]
    ang = jnp.concatenate([ang, ang], -1)
    sign = jnp.concatenate([-jnp.ones((HEAD_DIM // 2,), F32), jnp.ones((HEAD_DIM // 2,), F32)])
    return jnp.cos(ang), jnp.sin(ang) * sign


def _split_w_in(w_in):
    offs = [0]
    for wd in IN_WIDTHS:
        offs.append(offs[-1] + wd)
    (q, kc, vc, ks, vs, kw, vw, gates, z, xbc, dt, pv) = [
        w_in[:, offs[i]:offs[i + 1]] for i in range(len(IN_WIDTHS))]
    d = w_in.shape[0]

    def pad_to(w, width):
        return jnp.pad(w, ((0, 0), (0, width - w.shape[1])))

    w_a = jnp.concatenate([q, ks, vs, kw, vw], axis=1)
    w_b = jnp.concatenate([xbc, z, pv, pad_to(jnp.concatenate([dt, gates], axis=1), LANES), kc, vc], axis=1)
    return w_a.astype(BF16), w_b.astype(BF16)


def _pad_lanes(v):
    return jnp.pad(v, (0, LANES - v.shape[0]))[None, :]


def kernel(x, ffn1_norm_pre, ffn1_norm_post, ffn1_w_gate, ffn1_w_up, ffn1_w_down, mix_norm_pre, mix_norm_post, w_in, cmp_pe_k, cmp_pe_v, cmp_k_w1, cmp_k_w2, cmp_v_w1, cmp_v_w2, ssm_conv_w, ssm_conv_b, ssm_dt_bias, ssm_a_log, ssm_d, ssm_norm, pool_w, pool_scale, w_out, ffn2_norm_pre, ffn2_norm_post, ffn2_w_gate, ffn2_w_up, ffn2_w_down):
    batch, seq_len, d = x.shape
    depth = w_in.shape[0]
    assert d == D_MODEL and ffn1_w_gate.shape[1:] == (D_MODEL, D_FF) and w_in.shape[1:] == (D_MODEL, sum(IN_WIDTHS))
    assert seq_len % max(ATT_TILE, SLC_KEY_TILE, SSD_STEP_ROWS, POOL_TILE) == 0
    assert (batch * seq_len) % max(FFN_ROW_TILE, ROW_TILE, OUT_ROW_TILE) == 0 and D_FF % FFN_TILE == 0
    cos, sin_signed = _rope_tables(seq_len)
    h = x.reshape(batch * seq_len, d)
    for i in range(depth):
        h = _ffn(h, ffn1_norm_pre[i][None], ffn1_norm_post[i][None],
                 *_ffn_weights_bf16(ffn1_w_gate, ffn1_w_up, ffn1_w_down, i))
        proj_a, proj_b = _in_proj(h, mix_norm_pre[i][None], *_split_w_in(w_in[i]))
        o_nsa = _nsa(proj_a, proj_b, cos, sin_signed,
                     cmp_k_w1[i].astype(BF16), cmp_k_w2[i].astype(BF16),
                     cmp_v_w1[i].astype(BF16), cmp_v_w2[i].astype(BF16),
                     cmp_pe_k[i].reshape(1, -1), cmp_pe_v[i].reshape(1, -1), batch, seq_len)
        o_ssm = _ssd(proj_b, ssm_conv_w[i], ssm_conv_b[i][None], _pad_lanes(ssm_dt_bias[i]),
                     _pad_lanes(ssm_a_log[i]), jnp.repeat(ssm_d[i], SSM_HEAD_DIM)[None],
                     ssm_norm[i][None], batch, seq_len)
        o_pool = _pool(proj_b, pool_w[i].astype(BF16), pool_scale[i][None], seq_len)
        h = _out_proj(h, o_nsa, o_ssm, o_pool, w_out[i].astype(BF16), mix_norm_post[i][None])
        h = _ffn(h, ffn2_norm_pre[i][None], ffn2_norm_post[i][None],
                 *_ffn_weights_bf16(ffn2_w_gate, ffn2_w_up, ffn2_w_down, i))
    return h.reshape(batch, seq_len, d)
```

```python
import functools
import math

import jax
import jax.numpy as jnp
from jax import lax
from jax.experimental import pallas as pl
from jax.experimental.pallas import tpu as pltpu

F32 = jnp.float32
BF16 = jnp.bfloat16

D_MODEL = 2048
HEAD_DIM = 128
NSA_WIDTH = 1024
NSA_HEADS = 8
NSA_GROUPS = 2
NSA_HPG = NSA_HEADS // NSA_GROUPS
KV_WIDTH = NSA_GROUPS * HEAD_DIM
CMP_LEN = 32
CMP_STRIDE = 16
SLC_LEN = 64
SLC_TOPK = 16
WIN_LEN = 512
ROPE_THETA = 10000.0
SSM_WIDTH = 512
SSM_HEAD_DIM = 64
SSM_HEADS = 8
SSM_GROUPS = 2
SSM_STATE = 128
SSM_CONV = 4
SSM_CONV_DIM = SSM_WIDTH + 2 * SSM_GROUPS * SSM_STATE
POOL_WIDTH = 512
POOL_WINDOWS = (2, 4, 8, 16)
POOL_GROUP = POOL_WIDTH // len(POOL_WINDOWS)
D_FF = 5632
FFN_RESID = 0.5
RMS_EPS = 1e-6
NEG = -1e30
LOG2E = math.log2(math.e)
IN_WIDTHS = (NSA_WIDTH, KV_WIDTH, KV_WIDTH, KV_WIDTH, KV_WIDTH, KV_WIDTH, KV_WIDTH,
             3 * NSA_HEADS, SSM_WIDTH, SSM_CONV_DIM, SSM_HEADS, POOL_WIDTH)

LANES = 128
SUBLANES = 8
VMEM_LIMIT_BYTES = 60000 * 1024

A_Q, A_KS, A_VS, A_KW, A_VW = 0, 1024, 1280, 1536, 1792
A_WIDTH = 2048
B_XBC, B_Z, B_POOL, B_MISC, B_KC, B_VC = 0, 1024, 1536, 2048, 2176, 2432
B_WIDTH = 2688
GATE_LANE0 = SSM_HEADS

FFN_ROW_TILE = 1024
FFN_ROW_SPLIT = 2
FFN_TILE = 512
CAST_STEPS = 8
ROW_TILE = 512
OUT_ROW_TILE = 1024
OUT_ROW_SPLIT = 2
ATT_TILE = 256
SLC_KEY_TILE = 512
SSD_CHUNK = 128
SSD_STEP_ROWS = 1024
POOL_TILE = 2048
POOL_HALO = 16
CONV_HALO = 8


def _params(semantics):
    return pltpu.CompilerParams(dimension_semantics=semantics, vmem_limit_bytes=VMEM_LIMIT_BYTES)


def _sigmoid(x):
    return 1.0 / (1.0 + jnp.exp(-x))


def _silu(x):
    return x * _sigmoid(x)


def _rms(x):
    return x * lax.rsqrt(jnp.mean(x * x, axis=-1, keepdims=True) + RMS_EPS)


def _dot(a, b):
    return jnp.dot(a, b, preferred_element_type=F32)


def _dot_nt(a, b):
    return lax.dot_general(a, b, (((1,), (1,)), ((), ())), preferred_element_type=F32)


def _split3(x):
    hi = x.astype(BF16)
    r = x - hi.astype(F32)
    mid = r.astype(BF16)
    lo = (r - mid.astype(F32)).astype(BF16)
    return hi, mid, lo


def _dot_exact_rhs(x, sel):
    hi, mid, lo = _split3(x)
    return _dot(hi, sel) + _dot(mid, sel) + _dot(lo, sel)


def _ffn_kernel(x_ref, gpre_ref, gpost_ref, wg_ref, wu_ref, wd_ref, o_ref, xn_ref):
    j = pl.program_id(1)
    last = pl.num_programs(1) - 1

    def step(first, final):
        part = x_ref.shape[0] // FFN_ROW_SPLIT
        for r in range(FFN_ROW_SPLIT):
            rows = slice(r * part, (r + 1) * part)
            if first:
                xn = (_rms(x_ref[rows, :]) * gpre_ref[...]).astype(BF16)
                xn_ref[rows, :] = xn
            else:
                xn = xn_ref[rows, :]
            g = _dot(xn, wg_ref[...])
            u = _dot(xn, wu_ref[...])
            a = (_silu(g) * u).astype(BF16)
            acc = _dot(a, wd_ref[...])
            if not first:
                acc += o_ref[rows, :]
            if final:
                o_ref[rows, :] = x_ref[rows, :] + FFN_RESID * (_rms(acc) * gpost_ref[...])
            else:
                o_ref[rows, :] = acc

    pl.when(j == 0)(functools.partial(step, True, False))
    pl.when((j > 0) & (j < last))(functools.partial(step, False, False))
    pl.when(j == last)(functools.partial(step, False, True))


def _ffn(x, gpre, gpost, wg, wu, wd):
    n, d = x.shape
    dff = wg.shape[1]
    tm, tf = min(FFN_ROW_TILE, n), FFN_TILE
    return pl.pallas_call(
        _ffn_kernel,
        grid=(n // tm, dff // tf),
        in_specs=[
            pl.BlockSpec((tm, d), lambda i, j: (i, 0)),
            pl.BlockSpec((1, d), lambda i, j: (0, 0)),
            pl.BlockSpec((1, d), lambda i, j: (0, 0)),
            pl.BlockSpec((d, tf), lambda i, j: (0, j)),
            pl.BlockSpec((d, tf), lambda i, j: (0, j)),
            pl.BlockSpec((tf, d), lambda i, j: (j, 0)),
        ],
        out_specs=pl.BlockSpec((tm, d), lambda i, j: (i, 0)),
        out_shape=jax.ShapeDtypeStruct((n, d), F32),
        scratch_shapes=[pltpu.VMEM((tm, d), BF16)],
        compiler_params=_params(("parallel", "arbitrary")),
        name="ffn",
    )(x, gpre, gpost, wg, wu, wd)


def _cast3_kernel(a_ref, b_ref, c_ref, oa_ref, ob_ref, oc_ref):
    oa_ref[...] = a_ref[...].astype(oa_ref.dtype)
    ob_ref[...] = b_ref[...].astype(ob_ref.dtype)
    oc_ref[...] = c_ref[...].astype(oc_ref.dtype)


def _ffn_weights_bf16(wg, wu, wd, layer):
    steps = CAST_STEPS

    def spec_in(w):
        return pl.BlockSpec((None, w.shape[1] // steps, w.shape[2]), lambda r: (layer, r, 0))

    def spec_out(w):
        return pl.BlockSpec((w.shape[1] // steps, w.shape[2]), lambda r: (r, 0))

    ws = (wg, wu, wd)
    return pl.pallas_call(
        _cast3_kernel,
        grid=(steps,),
        in_specs=[spec_in(w) for w in ws],
        out_specs=[spec_out(w) for w in ws],
        out_shape=[jax.ShapeDtypeStruct(w.shape[1:], BF16) for w in ws],
        compiler_params=_params(("parallel",)),
        name="ffn_weights_bf16",
    )(*ws)


def _in_proj_kernel(x_ref, g_ref, wa_ref, wb_ref, oa_ref, ob_ref):
    xn = (_rms(x_ref[...]) * g_ref[...]).astype(BF16)
    oa_ref[...] = _dot(xn, wa_ref[...]).astype(oa_ref.dtype)
    ob_ref[...] = _dot(xn, wb_ref[...])


def _in_proj(x, g, w_a, w_b):
    n, d = x.shape
    tm = min(ROW_TILE, n)
    return pl.pallas_call(
        _in_proj_kernel,
        grid=(n // tm,),
        in_specs=[
            pl.BlockSpec((tm, d), lambda i: (i, 0)),
            pl.BlockSpec((1, d), lambda i: (0, 0)),
            pl.BlockSpec((d, A_WIDTH), lambda i: (0, 0), pipeline_mode=pl.Buffered(1)),
            pl.BlockSpec((d, B_WIDTH), lambda i: (0, 0), pipeline_mode=pl.Buffered(1)),
        ],
        out_specs=[
            pl.BlockSpec((tm, A_WIDTH), lambda i: (i, 0)),
            pl.BlockSpec((tm, B_WIDTH), lambda i: (i, 0)),
        ],
        out_shape=[jax.ShapeDtypeStruct((n, A_WIDTH), BF16),
                   jax.ShapeDtypeStruct((n, B_WIDTH), F32)],
        compiler_params=_params(("parallel",)),
        name="in_proj",
    )(x, g, w_a, w_b)


def _out_proj_kernel(h_ref, a_ref, s_ref, p_ref, w_ref, g_ref, o_ref):
    part = h_ref.shape[0] // OUT_ROW_SPLIT
    for r in range(OUT_ROW_SPLIT):
        rows = slice(r * part, (r + 1) * part)
        m = _dot(a_ref[rows, :], w_ref[0:NSA_WIDTH, :])
        m += _dot(s_ref[rows, :], w_ref[NSA_WIDTH:NSA_WIDTH + SSM_WIDTH, :])
        m += _dot(p_ref[rows, :], w_ref[NSA_WIDTH + SSM_WIDTH:, :])
        o_ref[rows, :] = h_ref[rows, :] + _rms(m) * g_ref[...]


def _out_proj(h, o_nsa, o_ssm, o_pool, w, g):
    n, d = h.shape
    tm = min(OUT_ROW_TILE, n)
    return pl.pallas_call(
        _out_proj_kernel,
        grid=(n // tm,),
        in_specs=[
            pl.BlockSpec((tm, d), lambda i: (i, 0)),
            pl.BlockSpec((tm, NSA_WIDTH), lambda i: (i, 0)),
            pl.BlockSpec((tm, SSM_WIDTH), lambda i: (i, 0)),
            pl.BlockSpec((tm, POOL_WIDTH), lambda i: (i, 0)),
            pl.BlockSpec((d, d), lambda i: (0, 0), pipeline_mode=pl.Buffered(1)),
            pl.BlockSpec((1, d), lambda i: (0, 0)),
        ],
        out_specs=pl.BlockSpec((tm, d), lambda i: (i, 0)),
        out_shape=jax.ShapeDtypeStruct((n, d), F32),
        compiler_params=_params(("parallel",)),
        name="out_proj",
    )(h, o_nsa, o_ssm, o_pool, w, g)


def _pool_kernel(halo_ref, x_ref, w_ref, scale_ref, o_ref, *, tiles_per_seq):
    tile = x_ref.shape[0]
    it = pl.program_id(0) % tiles_per_seq
    halo = jnp.where(it == 0, 0.0, halo_ref[...])
    ext = jnp.concatenate([halo, x_ref[...]], axis=0)
    pos = it * tile + lax.broadcasted_iota(jnp.int32, (tile, POOL_GROUP), 0)
    for gi, w in enumerate(POOL_WINDOWS):
        xg = ext[:, gi * POOL_GROUP:(gi + 1) * POOL_GROUP]
        s, k = xg, 1
        while k < w:
            s = s + pltpu.roll(s, k, axis=0)
            k *= 2
        cnt = jnp.minimum(pos + 1, w).astype(F32)
        dlt = s[POOL_HALO:] / cnt - xg[POOL_HALO:]
        y = _dot(dlt.astype(BF16), w_ref[gi])
        sl = slice(gi * POOL_GROUP, (gi + 1) * POOL_GROUP)
        o_ref[:, sl] = (y * scale_ref[:, sl]).astype(o_ref.dtype)


def _pool(proj_b, pool_w, pool_scale, seq_len):
    n = proj_b.shape[0]
    tile = min(POOL_TILE, seq_len)
    tiles_per_seq = seq_len // tile
    ratio = tile // POOL_HALO
    col = B_POOL // POOL_WIDTH
    return pl.pallas_call(
        functools.partial(_pool_kernel, tiles_per_seq=tiles_per_seq),
        grid=(n // tile,),
        in_specs=[
            pl.BlockSpec((POOL_HALO, POOL_WIDTH), lambda i: (jnp.maximum(i * ratio - 1, 0), col)),
            pl.BlockSpec((tile, POOL_WIDTH), lambda i: (i, col)),
            pl.BlockSpec((len(POOL_WINDOWS), POOL_GROUP, POOL_GROUP), lambda i: (0, 0, 0)),
            pl.BlockSpec((1, POOL_WIDTH), lambda i: (0, 0)),
        ],
        out_specs=pl.BlockSpec((tile, POOL_WIDTH), lambda i: (i, 0)),
        out_shape=jax.ShapeDtypeStruct((n, POOL_WIDTH), BF16),
        compiler_params=_params(("parallel",)),
        name="pool",
    )(proj_b, proj_b, pool_w, pool_scale)


def _ssd_kernel(halo_ref, xbc_ref, z_ref, dt_ref, convw_ref, convb_ref, dtb_ref, alog_ref,
                dskip_ref, normw_ref, o_ref, state_ref):
    c = pl.program_id(1)

    @pl.when(c == 0)
    def _():
        state_ref[...] = jnp.zeros_like(state_ref)

    for r in range(xbc_ref.shape[0] // SSD_CHUNK):
        rows = slice(r * SSD_CHUNK, (r + 1) * SSD_CHUNK)
        if r == 0:
            halo = jnp.where(c == 0, 0.0, halo_ref[...])
        else:
            halo = xbc_ref[r * SSD_CHUNK - CONV_HALO:r * SSD_CHUNK, :]
        _ssd_chunk(halo, xbc_ref[rows, :], z_ref[rows, :], dt_ref[rows, :], convw_ref, convb_ref, dtb_ref,
                   alog_ref, dskip_ref, normw_ref, o_ref.at[rows, :], state_ref)


def _ssd_chunk(halo, xbc, z, dt_raw, convw_ref, convb_ref, dtb_ref, alog_ref, dskip_ref, normw_ref,
               o_ref, state_ref):
    L = xbc.shape[0]

    ext = jnp.concatenate([halo, xbc], axis=0)
    conv = convb_ref[...] + convw_ref[SSM_CONV - 1:SSM_CONV, :] * xbc
    for k in range(1, SSM_CONV):
        conv += convw_ref[SSM_CONV - 1 - k:SSM_CONV - k, :] * pltpu.roll(ext, k, axis=0)[CONV_HALO:]
    act = _silu(conv)
    xs = act[:, :SSM_WIDTH]
    bm = act[:, SSM_WIDTH:SSM_WIDTH + SSM_GROUPS * SSM_STATE]
    cm = act[:, SSM_WIDTH + SSM_GROUPS * SSM_STATE:]

    lane = lax.broadcasted_iota(jnp.int32, (1, LANES), 1)
    xdt = dt_raw + dtb_ref[...]
    dt = jnp.maximum(xdt, 0.0) + jnp.log1p(jnp.exp(-jnp.abs(xdt)))
    a_head = jnp.where(lane < SSM_HEADS, -jnp.exp(alog_ref[...]), 0.0)
    cs = a_head * dt
    row = lax.broadcasted_iota(jnp.int32, (L, LANES), 0)
    k = 1
    while k < L:
        cs = cs + jnp.where(row >= k, pltpu.roll(cs, k, axis=0), 0.0)
        k *= 2
    cs_t = cs.T

    er = lax.broadcasted_iota(jnp.int32, (LANES, SSM_WIDTH), 0)
    ec = lax.broadcasted_iota(jnp.int32, (LANES, SSM_WIDTH), 1)
    expand = jnp.where(er == ec // SSM_HEAD_DIM, 1.0, 0.0).astype(BF16)
    dtx = _dot_exact_rhs(dt, expand)
    csx = _dot_exact_rhs(cs, expand)
    cs_last = csx[L - 1:L, :]

    x_dt = xs * dtx
    x_end = (x_dt * jnp.exp(cs_last - csx)).astype(BF16)
    x_dt16 = x_dt.astype(BF16)
    ecs = jnp.exp(csx)
    li = lax.broadcasted_iota(jnp.int32, (L, L), 0)
    si = lax.broadcasted_iota(jnp.int32, (L, L), 1)
    gw = SSM_WIDTH // SSM_GROUPS
    hpg = SSM_HEADS // SSM_GROUPS
    lane_g = lax.broadcasted_iota(jnp.int32, (1, gw), 1)
    ys = []
    for g in range(SSM_GROUPS):
        bg = bm[:, g * SSM_STATE:(g + 1) * SSM_STATE]
        cg = cm[:, g * SSM_STATE:(g + 1) * SSM_STATE].astype(BF16)
        gs = slice(g * gw, (g + 1) * gw)
        cb = _dot_nt(cg, bg.astype(BF16))
        y_g = jnp.zeros((L, gw), F32)
        for hh in range(hpg):
            h = g * hpg + hh
            seg = cs[:, h:h + 1] - cs_t[h:h + 1, :]
            m_h = (cb * jnp.exp(jnp.where(li >= si, seg, NEG))).astype(BF16)
            x_h = jnp.where(lane_g // SSM_HEAD_DIM == hh, x_dt16[:, gs], jnp.zeros((), BF16))
            y_g += _dot(m_h, x_h)
        st = state_ref[g]
        y_g += _dot(cg, st.astype(BF16)) * ecs[:, gs]
        state_ref[g] = jnp.exp(cs_last[:, gs]) * st + _dot(bg.T.astype(BF16), x_end[:, gs])
        ys.append(y_g)
    y = jnp.concatenate(ys, axis=1) + dskip_ref[...] * xs
    y = y * _silu(z)
    outs = []
    for g in range(SSM_GROUPS):
        outs.append(_rms(y[:, g * gw:(g + 1) * gw]))
    o_ref[...] = (jnp.concatenate(outs, axis=1) * normw_ref[...]).astype(o_ref.dtype)


def _ssd(proj_b, conv_w, conv_b, dt_bias, a_log, d_skip, norm_w, batch, seq_len):
    n = proj_b.shape[0]
    L = min(SSD_STEP_ROWS, seq_len)
    nch = seq_len // L
    ratio = L // CONV_HALO
    return pl.pallas_call(
        _ssd_kernel,
        grid=(batch, nch),
        in_specs=[
            pl.BlockSpec((CONV_HALO, SSM_CONV_DIM),
                         lambda b, c: (jnp.maximum((b * nch + c) * ratio - 1, 0), B_XBC // SSM_CONV_DIM)),
            pl.BlockSpec((L, SSM_CONV_DIM), lambda b, c: (b * nch + c, B_XBC // SSM_CONV_DIM)),
            pl.BlockSpec((L, SSM_WIDTH), lambda b, c: (b * nch + c, B_Z // SSM_WIDTH)),
            pl.BlockSpec((L, LANES), lambda b, c: (b * nch + c, B_MISC // LANES)),
            pl.BlockSpec((SSM_CONV, SSM_CONV_DIM), lambda b, c: (0, 0)),
            pl.BlockSpec((1, SSM_CONV_DIM), lambda b, c: (0, 0)),
            pl.BlockSpec((1, LANES), lambda b, c: (0, 0)),
            pl.BlockSpec((1, LANES), lambda b, c: (0, 0)),
            pl.BlockSpec((1, SSM_WIDTH), lambda b, c: (0, 0)),
            pl.BlockSpec((1, SSM_WIDTH), lambda b, c: (0, 0)),
        ],
        out_specs=pl.BlockSpec((L, SSM_WIDTH), lambda b, c: (b * nch + c, 0)),
        out_shape=jax.ShapeDtypeStruct((n, SSM_WIDTH), BF16),
        scratch_shapes=[pltpu.VMEM((SSM_GROUPS, SSM_STATE, SSM_WIDTH // SSM_GROUPS), F32)],
        compiler_params=_params(("parallel", "arbitrary")),
        name="ssd",
    )(proj_b, proj_b, proj_b, proj_b, conv_w, conv_b, dt_bias, a_log, d_skip, norm_w)


def _rope(x, cos, sin_signed):
    return x * cos + pltpu.roll(x, HEAD_DIM // 2, axis=1) * sin_signed


def _compress(src_ref, w1_ref, w2_ref, pe_ref):
    n_cmp = src_ref.shape[0] // CMP_STRIDE
    xs = [src_ref[pl.ds(p, n_cmp, stride=CMP_STRIDE), :] for p in range(CMP_STRIDE)]
    x = jnp.concatenate(xs, axis=1).astype(BF16)
    half = CMP_STRIDE * HEAD_DIM
    first = _dot(x, w1_ref[0:half, :])
    second = _dot(x, w1_ref[half:2 * half, :])
    pe = jnp.broadcast_to(pe_ref[...], (SUBLANES, 2 * half)).astype(BF16)
    pe_term = _dot(pe, w1_ref[...])[0:1, :]
    pre = first + pltpu.roll(second, n_cmp - 1, axis=0) + pe_term
    return _dot(_silu(pre).astype(BF16), w2_ref[...])


def _nsa_kernel(q_ref, gate_ref, kc_src_ref, vc_src_ref, ks_ref, vs_ref, kw_ref, vw_ref,
                cos_ref, sin_ref, w1k_ref, w2k_ref, w1v_ref, w2v_ref, pek_ref, pev_ref,
                o_ref,
                kc_scr, vct_scr, kx_scr, kwp_scr, vst_scr, vwt_scr, qx_scr, score_scr, rank_scr,
                m_scr, l_scr, acc_scr, oct_scr, owt_scr, s_scr, band_scr):
    tq = q_ref.shape[0]
    seq = ks_ref.shape[0]
    tk = min(SLC_KEY_TILE, seq)
    nh = NSA_HPG
    n_cmp, n_slc = seq // CMP_STRIDE, seq // SLC_LEN
    topk = min(SLC_TOPK, n_slc)
    qi = pl.program_id(2)
    q0 = qi * tq
    qk_scale = HEAD_DIM ** -0.5 * LOG2E

    @pl.when(qi == 0)
    def _():
        kc_scr[...] = _compress(kc_src_ref, w1k_ref, w2k_ref, pek_ref).astype(BF16)
        vct_scr[...] = _compress(vc_src_ref, w1v_ref, w2v_ref, pev_ref).T.astype(BF16)
        kwp_scr[0:WIN_LEN, :] = jnp.zeros((WIN_LEN, HEAD_DIM), BF16)
        for j in range(WIN_LEN // tq):
            vwt_scr[j] = jnp.zeros((HEAD_DIM, tq), BF16)
        kc_i = lax.broadcasted_iota(jnp.int32, (tq + WIN_LEN, 1), 0)
        qr_i = lax.broadcasted_iota(jnp.int32, (1, tq), 1)
        band_scr[...] = jnp.where((kc_i > qr_i) & (kc_i <= qr_i + WIN_LEN), 0.0, NEG)

        def prep_rows(i, carry):
            rows = pl.ds(pl.multiple_of(i * tq, tq), tq)
            prows = pl.ds(pl.multiple_of(WIN_LEN + i * tq, tq), tq)
            cos, sin = cos_ref[rows, :], sin_ref[rows, :]
            kx_scr[rows, 0:HEAD_DIM] = _rope(ks_ref[rows, :].astype(F32), cos, sin).astype(BF16)
            blk = (i * tq + lax.broadcasted_iota(jnp.int32, (tq, LANES), 0)) // SLC_LEN
            lane = lax.broadcasted_iota(jnp.int32, (tq, LANES), 1)
            kx_scr[rows, HEAD_DIM:2 * HEAD_DIM] = jnp.where(lane == blk, 1.0, 0.0).astype(BF16)
            kwp_scr[prows, :] = _rope(kw_ref[rows, :].astype(F32), cos, sin).astype(BF16)
            vst_scr[i] = vs_ref[rows, :].astype(F32).T.astype(BF16)
            vwt_scr[i + WIN_LEN // tq] = vw_ref[rows, :].astype(F32).T.astype(BF16)
            return carry

        lax.fori_loop(0, seq // tq, prep_rows, 0)

    qrows = pl.ds(pl.multiple_of(q0, tq), tq)
    qlane = lax.broadcasted_iota(jnp.int32, (1, tq), 1)
    tpos = q0 + qlane

    def q_rows(h):
        return slice(h * tq, (h + 1) * tq)

    cos_q, sin_q = cos_ref[qrows, :], sin_ref[qrows, :]
    for h in range(nh):
        hs = slice(h * HEAD_DIM, (h + 1) * HEAD_DIM)
        qx_scr[q_rows(h), 0:HEAD_DIM] = (_rope(q_ref[:, hs].astype(F32), cos_q, sin_q) * qk_scale).astype(BF16)

    def heads(x):
        return jnp.concatenate([x] * nh, axis=1)

    q_stack = jnp.concatenate([q_ref[:, h * HEAD_DIM:(h + 1) * HEAD_DIM] for h in range(nh)], axis=0)
    nblk = lax.broadcasted_iota(jnp.int32, (n_cmp, 1), 0)
    cmp_bias = jnp.where((nblk * CMP_STRIDE + (CMP_LEN - 1)) <= tpos, 0.0, NEG)
    any_visible = jnp.where(tpos >= CMP_LEN - 1, 1.0, 0.0)
    s = _dot_nt(kc_scr[...], q_stack) * qk_scale + heads(cmp_bias)
    e = jnp.exp2(s - jnp.max(s, axis=0, keepdims=True))
    p = e * (heads(any_visible) / jnp.sum(e, axis=0, keepdims=True))
    p_sum = p[:, 0:tq]
    for h in range(1, nh):
        p_sum += p[:, h * tq:(h + 1) * tq]
    oct_scr[...] = _dot(vct_scr[...], p.astype(BF16))

    nr = lax.broadcasted_iota(jnp.int32, (n_slc, n_cmp), 1) * CMP_STRIDE
    jr = lax.broadcasted_iota(jnp.int32, (n_slc, n_cmp), 0) * SLC_LEN
    overlap_t = jnp.where((nr < jr + SLC_LEN) & (nr + CMP_LEN > jr), 1.0, 0.0).astype(BF16)
    ps_hi = p_sum.astype(BF16)
    ps_lo = (p_sum - ps_hi.astype(F32)).astype(BF16)
    imp_t = _dot(overlap_t, ps_hi) + _dot(overlap_t, ps_lo)

    jblk = lax.broadcasted_iota(jnp.int32, (n_slc, tq), 0)
    qblk = (q0 + lax.broadcasted_iota(jnp.int32, (n_slc, tq), 1)) // SLC_LEN
    valid = jblk <= qblk
    forced = (jblk == 0) | (jblk == qblk) | (jblk == qblk - 1)
    score_scr[...] = jnp.where(valid, jnp.where(forced, 1e9, imp_t), -1e9)
    rank_scr[...] = jnp.zeros_like(rank_scr)
    ng = n_slc // SUBLANES
    g_last = ((q0 + tq - 1) // SLC_LEN) // SUBLANES
    row8 = lax.broadcasted_iota(jnp.int32, (SUBLANES, tq), 0)
    for gp in range(ng):
        @pl.when(gp <= g_last)
        def _():
            cnt = [jnp.zeros((SUBLANES, tq), F32) for _ in range(ng)]
            for jj in range(SUBLANES):
                jp = gp * SUBLANES + jj
                other = jnp.broadcast_to(score_scr[jp:jp + 1, :], (SUBLANES, tq))
                for g in range(ng):
                    sc = score_scr[g * SUBLANES:(g + 1) * SUBLANES, :]
                    if g > gp:
                        ahead = other >= sc
                    elif g < gp:
                        ahead = other > sc
                    else:
                        ahead = (other > sc) | ((other == sc) & (row8 > jj))
                    cnt[g] += jnp.where(ahead, 1.0, 0.0)
            for g in range(ng):
                rank_scr[g * SUBLANES:(g + 1) * SUBLANES, :] += cnt[g]

    wspan = tq + WIN_LEN
    wrows = pl.ds(pl.multiple_of(q0, tq), wspan)
    c = lax.broadcasted_iota(jnp.int32, (wspan, 1), 0)
    wb = band_scr[...] + jnp.where(c + q0 >= WIN_LEN, 0.0, NEG)
    s = _dot_nt(kwp_scr[wrows, :], qx_scr[:, 0:HEAD_DIM]) + heads(wb)
    p = jnp.exp2(s - jnp.max(s, axis=0, keepdims=True))
    pb = p.astype(BF16)
    o = _dot(vwt_scr[qi], pb[0:tq])
    for j in range(1, wspan // tq):
        o += _dot(vwt_scr[qi + j], pb[j * tq:(j + 1) * tq])
    owt_scr[...] = o / jnp.sum(p, axis=0, keepdims=True)

    sel_t = jnp.where(valid & (rank_scr[...] < topk), 1.0, 0.0).astype(BF16)
    sel_pad = jnp.concatenate([sel_t, jnp.zeros((LANES - n_slc, tq), BF16)], axis=0)
    ri = lax.broadcasted_iota(jnp.int32, (tq, tq), 0)
    ci = lax.broadcasted_iota(jnp.int32, (tq, tq), 1)
    eye = jnp.where(ri == ci, 1.0, 0.0).astype(BF16)
    sel = _dot_nt(eye, sel_pad)
    lane = lax.broadcasted_iota(jnp.int32, (tq, LANES), 1)
    sel_bias = jnp.where((lane < n_slc) & (sel < 0.5), NEG, 0.0).astype(BF16)

    for h in range(nh):
        qx_scr[q_rows(h), HEAD_DIM:2 * HEAD_DIM] = sel_bias
    m_scr[...] = jnp.full(m_scr.shape, NEG, F32)
    l_scr[...] = jnp.zeros_like(l_scr)
    acc_scr[...] = jnp.zeros_like(acc_scr)
    chunks = tk // tq

    def scores(kt):
        return _dot_nt(kx_scr[pl.ds(pl.multiple_of(kt * tk, tk), tk), :], qx_scr[...])

    def softmax_update(s, kt):
        m_prev = m_scr[...]
        m_new = jnp.maximum(m_prev, jnp.max(s, axis=0, keepdims=True))
        p = jnp.exp2(s - m_new)
        alpha = jnp.exp2(m_prev - m_new)
        l_scr[...] = alpha * l_scr[...] + jnp.sum(p, axis=0, keepdims=True)
        pb = p.astype(BF16)
        pv = _dot(vst_scr[kt * chunks], pb[0:tq])
        for j in range(1, chunks):
            pv += _dot(vst_scr[kt * chunks + j], pb[j * tq:(j + 1) * tq])
        acc_scr[...] = alpha * acc_scr[...] + pv
        m_scr[...] = m_new

    n_tiles = (q0 + tq + tk - 1) // tk
    s_scr[...] = scores(0)

    def slc_step(kt):
        s = s_scr[...]
        s_next = scores(kt + 1)
        softmax_update(s, kt)
        s_scr[...] = s_next

    def slc_pair(i, carry):
        slc_step(2 * i)
        slc_step(2 * i + 1)
        return carry

    n_steps = n_tiles - 1
    lax.fori_loop(0, n_steps // 2, slc_pair, 0)

    @pl.when(n_steps % 2 == 1)
    def _():
        slc_step(n_steps - 1)

    kpos = (n_tiles - 1) * tk + lax.broadcasted_iota(jnp.int32, (tk, 1), 0)
    softmax_update(s_scr[...] + heads(jnp.where(kpos <= tpos, 0.0, NEG)), n_tiles - 1)

    gates_all = _sigmoid(gate_ref[...]).T
    per_group = 3 * nh
    gates = jnp.where(pl.program_id(1) == 0,
                      gates_all[GATE_LANE0:GATE_LANE0 + per_group],
                      gates_all[GATE_LANE0 + per_group:GATE_LANE0 + 2 * per_group])
    for h in range(nh):
        hs = slice(h * HEAD_DIM, (h + 1) * HEAD_DIM)
        ql = slice(h * tq, (h + 1) * tq)
        o = (gates[3 * h:3 * h + 1] * oct_scr[:, ql]
             + gates[3 * h + 1:3 * h + 2] * (acc_scr[:, ql] / l_scr[:, ql])
             + gates[3 * h + 2:3 * h + 3] * owt_scr[:, ql])
        o_ref[:, hs] = o.T.astype(o_ref.dtype)


def _nsa(proj_a, proj_b, cos, sin_signed, w1k, w2k, w1v, w2v, pe_k, pe_v, batch, seq_len):
    n = proj_a.shape[0]
    tq = min(ATT_TILE, seq_len)
    nq = seq_len // tq
    gw = NSA_HPG * HEAD_DIM
    n_cmp, n_slc = seq_len // CMP_STRIDE, seq_len // SLC_LEN
    assert n_slc <= LANES // 2 and n_slc % SUBLANES == 0 and seq_len % min(SLC_KEY_TILE, seq_len) == 0

    def kv_spec(col0):
        return pl.BlockSpec((seq_len, HEAD_DIM), lambda b, g, i: (b, col0 // HEAD_DIM + g))

    def full(shape):
        return pl.BlockSpec(shape, lambda b, g, i: (0,) * len(shape))

    return pl.pallas_call(
        _nsa_kernel,
        grid=(batch, NSA_GROUPS, nq),
        in_specs=[
            pl.BlockSpec((tq, gw), lambda b, g, i: (b * nq + i, A_Q // gw + g)),
            pl.BlockSpec((tq, LANES), lambda b, g, i: (b * nq + i, B_MISC // LANES)),
            kv_spec(B_KC), kv_spec(B_VC),
            kv_spec(A_KS), kv_spec(A_VS), kv_spec(A_KW), kv_spec(A_VW),
            full((seq_len, HEAD_DIM)), full((seq_len, HEAD_DIM)),
            full((CMP_LEN * HEAD_DIM, HEAD_DIM)), full((HEAD_DIM, HEAD_DIM)),
            full((CMP_LEN * HEAD_DIM, HEAD_DIM)), full((HEAD_DIM, HEAD_DIM)),
            full((1, CMP_LEN * HEAD_DIM)), full((1, CMP_LEN * HEAD_DIM)),
        ],
        out_specs=pl.BlockSpec((tq, gw), lambda b, g, i: (b * nq + i, g)),
        out_shape=jax.ShapeDtypeStruct((n, NSA_WIDTH), BF16),
        scratch_shapes=[
            pltpu.VMEM((n_cmp, HEAD_DIM), BF16),
            pltpu.VMEM((HEAD_DIM, n_cmp), BF16),
            pltpu.VMEM((seq_len, 2 * HEAD_DIM), BF16),
            pltpu.VMEM((seq_len + WIN_LEN, HEAD_DIM), BF16),
            pltpu.VMEM((seq_len // tq, HEAD_DIM, tq), BF16),
            pltpu.VMEM(((seq_len + WIN_LEN) // tq, HEAD_DIM, tq), BF16),
            pltpu.VMEM((NSA_HPG * tq, 2 * HEAD_DIM), BF16),
            pltpu.VMEM((n_slc, tq), F32),
            pltpu.VMEM((n_slc, tq), F32),
            pltpu.VMEM((1, NSA_HPG * tq), F32),
            pltpu.VMEM((1, NSA_HPG * tq), F32),
            pltpu.VMEM((HEAD_DIM, NSA_HPG * tq), F32),
            pltpu.VMEM((HEAD_DIM, NSA_HPG * tq), F32),
            pltpu.VMEM((HEAD_DIM, NSA_HPG * tq), F32),
            pltpu.VMEM((min(SLC_KEY_TILE, seq_len), NSA_HPG * tq), F32),
            pltpu.VMEM((tq + WIN_LEN, tq), F32),
        ],
        compiler_params=_params(("parallel", "parallel", "arbitrary")),
        name="nsa",
    )(proj_a, proj_b, proj_b, proj_b, proj_a, proj_a, proj_a, proj_a,
      cos, sin_signed, w1k, w2k, w1v, w2v, pe_k, pe_v)


def _rope_tables(seq_len):
    inv = ROPE_THETA ** (-jnp.arange(0, HEAD_DIM, 2, dtype=F32) / HEAD_DIM)
    ang = jnp.arange(seq_len, dtype=F32)[:, None] * inv[None, :]
    ang = jnp.concatenate([ang, ang], -1)
    sign = jnp.concatenate([-jnp.ones((HEAD_DIM // 2,), F32), jnp.ones((HEAD_DIM // 2,), F32)])
    return jnp.cos(ang), jnp.sin(ang) * sign


def _split_w_in(w_in):
    offs = [0]
    for wd in IN_WIDTHS:
        offs.append(offs[-1] + wd)
    (q, kc, vc, ks, vs, kw, vw, gates, z, xbc, dt, pv) = [
        w_in[:, offs[i]:offs[i + 1]] for i in range(len(IN_WIDTHS))]
    d = w_in.shape[0]

    def pad_to(w, width):
        return jnp.pad(w, ((0, 0), (0, width - w.shape[1])))

    w_a = jnp.concatenate([q, ks, vs, kw, vw], axis=1)
    w_b = jnp.concatenate([xbc, z, pv, pad_to(jnp.concatenate([dt, gates], axis=1), LANES), kc, vc], axis=1)
    return w_a.astype(BF16), w_b.astype(BF16)


def _pad_lanes(v):
    return jnp.pad(v, (0, LANES - v.shape[0]))[None, :]


def kernel(x, ffn1_norm_pre, ffn1_norm_post, ffn1_w_gate, ffn1_w_up, ffn1_w_down, mix_norm_pre, mix_norm_post, w_in, cmp_pe_k, cmp_pe_v, cmp_k_w1, cmp_k_w2, cmp_v_w1, cmp_v_w2, ssm_conv_w, ssm_conv_b, ssm_dt_bias, ssm_a_log, ssm_d, ssm_norm, pool_w, pool_scale, w_out, ffn2_norm_pre, ffn2_norm_post, ffn2_w_gate, ffn2_w_up, ffn2_w_down):
    batch, seq_len, d = x.shape
    depth = w_in.shape[0]
    assert d == D_MODEL and ffn1_w_gate.shape[1:] == (D_MODEL, D_FF) and w_in.shape[1:] == (D_MODEL, sum(IN_WIDTHS))
    assert seq_len % max(ATT_TILE, SLC_KEY_TILE, SSD_STEP_ROWS, POOL_TILE) == 0
    assert (batch * seq_len) % max(FFN_ROW_TILE, ROW_TILE, OUT_ROW_TILE) == 0 and D_FF % FFN_TILE == 0
    cos, sin_signed = _rope_tables(seq_len)
    h = x.reshape(batch * seq_len, d)
    for i in range(depth):
        h = _ffn(h, ffn1_norm_pre[i][None], ffn1_norm_post[i][None],
                 *_ffn_weights_bf16(ffn1_w_gate, ffn1_w_up, ffn1_w_down, i))
        proj_a, proj_b = _in_proj(h, mix_norm_pre[i][None], *_split_w_in(w_in[i]))
        o_nsa = _nsa(proj_a, proj_b, cos, sin_signed,
                     cmp_k_w1[i].astype(BF16), cmp_k_w2[i].astype(BF16),
                     cmp_v_w1[i].astype(BF16), cmp_v_w2[i].astype(BF16),
                     cmp_pe_k[i].reshape(1, -1), cmp_pe_v[i].reshape(1, -1), batch, seq_len)
        o_ssm = _ssd(proj_b, ssm_conv_w[i], ssm_conv_b[i][None], _pad_lanes(ssm_dt_bias[i]),
                     _pad_lanes(ssm_a_log[i]), jnp.repeat(ssm_d[i], SSM_HEAD_DIM)[None],
                     ssm_norm[i][None], batch, seq_len)
        o_pool = _pool(proj_b, pool_w[i].astype(BF16), pool_scale[i][None], seq_len)
        h = _out_proj(h, o_nsa, o_ssm, o_pool, w_out[i].astype(BF16), mix_norm_post[i][None])
        h = _ffn(h, ffn2_norm_pre[i][None], ffn2_norm_post[i][None],
                 *_ffn_weights_bf16(ffn2_w_gate, ffn2_w_up, ffn2_w_down, i))
    return h.reshape(batch, seq_len, d)
```

```python
import functools
import math

import jax
import jax.numpy as jnp
from jax import lax
from jax.experimental import pallas as pl
from jax.experimental.pallas import tpu as pltpu

F32 = jnp.float32
BF16 = jnp.bfloat16

D_MODEL = 2048
HEAD_DIM = 128
NSA_WIDTH = 1024
NSA_HEADS = 8
NSA_GROUPS = 2
NSA_HPG = NSA_HEADS // NSA_GROUPS
KV_WIDTH = NSA_GROUPS * HEAD_DIM
CMP_LEN = 32
CMP_STRIDE = 16
SLC_LEN = 64
SLC_TOPK = 16
WIN_LEN = 512
ROPE_THETA = 10000.0
SSM_WIDTH = 512
SSM_HEAD_DIM = 64
SSM_HEADS = 8
SSM_GROUPS = 2
SSM_STATE = 128
SSM_CONV = 4
SSM_CONV_DIM = SSM_WIDTH + 2 * SSM_GROUPS * SSM_STATE
POOL_WIDTH = 512
POOL_WINDOWS = (2, 4, 8, 16)
POOL_GROUP = POOL_WIDTH // len(POOL_WINDOWS)
D_FF = 5632
FFN_RESID = 0.5
RMS_EPS = 1e-6
NEG = -1e30
LOG2E = math.log2(math.e)
IN_WIDTHS = (NSA_WIDTH, KV_WIDTH, KV_WIDTH, KV_WIDTH, KV_WIDTH, KV_WIDTH, KV_WIDTH,
             3 * NSA_HEADS, SSM_WIDTH, SSM_CONV_DIM, SSM_HEADS, POOL_WIDTH)

LANES = 128
SUBLANES = 8
VMEM_LIMIT_BYTES = 60000 * 1024

A_Q, A_KS, A_VS, A_KW, A_VW = 0, 1024, 1280, 1536, 1792
A_WIDTH = 2048
B_XBC, B_Z, B_POOL, B_MISC, B_KC, B_VC = 0, 1024, 1536, 2048, 2176, 2432
B_WIDTH = 2688
GATE_LANE0 = SSM_HEADS

FFN_ROW_TILE = 1024
FFN_ROW_SPLIT = 2
FFN_TILE = 512
CAST_STEPS = 8
ROW_TILE = 512
OUT_ROW_TILE = 1024
OUT_ROW_SPLIT = 2
ATT_TILE = 256
SLC_KEY_TILE = 512
SSD_CHUNK = 128
SSD_STEP_ROWS = 1024
POOL_TILE = 2048
POOL_HALO = 16
CONV_HALO = 8


def _params(semantics):
    return pltpu.CompilerParams(dimension_semantics=semantics, vmem_limit_bytes=VMEM_LIMIT_BYTES)


def _sigmoid(x):
    return 1.0 / (1.0 + jnp.exp(-x))


def _silu(x):
    return x * _sigmoid(x)


def _rms(x):
    return x * lax.rsqrt(jnp.mean(x * x, axis=-1, keepdims=True) + RMS_EPS)


def _dot(a, b):
    return jnp.dot(a, b, preferred_element_type=F32)


def _dot_nt(a, b):
    return lax.dot_general(a, b, (((1,), (1,)), ((), ())), preferred_element_type=F32)


def _split3(x):
    hi = x.astype(BF16)
    r = x - hi.astype(F32)
    mid = r.astype(BF16)
    lo = (r - mid.astype(F32)).astype(BF16)
    return hi, mid, lo


def _dot_exact_rhs(x, sel):
    hi, mid, lo = _split3(x)
    return _dot(hi, sel) + _dot(mid, sel) + _dot(lo, sel)


def _ffn_kernel(x_ref, gpre_ref, gpost_ref, wg_ref, wu_ref, wd_ref, o_ref, xn_ref):
    j = pl.program_id(1)
    last = pl.num_programs(1) - 1

    def step(first, final):
        part = x_ref.shape[0] // FFN_ROW_SPLIT
        for r in range(FFN_ROW_SPLIT):
            rows = slice(r * part, (r + 1) * part)
            if first:
                xn = (_rms(x_ref[rows, :]) * gpre_ref[...]).astype(BF16)
                xn_ref[rows, :] = xn
            else:
                xn = xn_ref[rows, :]
            g = _dot(xn, wg_ref[...])
            u = _dot(xn, wu_ref[...])
            a = (_silu(g) * u).astype(BF16)
            acc = _dot(a, wd_ref[...])
            if not first:
                acc += o_ref[rows, :]
            if final:
                o_ref[rows, :] = x_ref[rows, :] + FFN_RESID * (_rms(acc) * gpost_ref[...])
            else:
                o_ref[rows, :] = acc

    pl.when(j == 0)(functools.partial(step, True, False))
    pl.when((j > 0) & (j < last))(functools.partial(step, False, False))
    pl.when(j == last)(functools.partial(step, False, True))


def _ffn(x, gpre, gpost, wg, wu, wd):
    n, d = x.shape
    dff = wg.shape[1]
    tm, tf = min(FFN_ROW_TILE, n), FFN_TILE
    return pl.pallas_call(
        _ffn_kernel,
        grid=(n // tm, dff // tf),
        in_specs=[
            pl.BlockSpec((tm, d), lambda i, j: (i, 0)),
            pl.BlockSpec((1, d), lambda i, j: (0, 0)),
            pl.BlockSpec((1, d), lambda i, j: (0, 0)),
            pl.BlockSpec((d, tf), lambda i, j: (0, j)),
            pl.BlockSpec((d, tf), lambda i, j: (0, j)),
            pl.BlockSpec((tf, d), lambda i, j: (j, 0)),
        ],
        out_specs=pl.BlockSpec((tm, d), lambda i, j: (i, 0)),
        out_shape=jax.ShapeDtypeStruct((n, d), F32),
        scratch_shapes=[pltpu.VMEM((tm, d), BF16)],
        compiler_params=_params(("parallel", "arbitrary")),
        name="ffn",
    )(x, gpre, gpost, wg, wu, wd)


def _cast3_kernel(a_ref, b_ref, c_ref, oa_ref, ob_ref, oc_ref):
    oa_ref[...] = a_ref[...].astype(oa_ref.dtype)
    ob_ref[...] = b_ref[...].astype(ob_ref.dtype)
    oc_ref[...] = c_ref[...].astype(oc_ref.dtype)


def _ffn_weights_bf16(wg, wu, wd, layer):
    steps = CAST_STEPS

    def spec_in(w):
        return pl.BlockSpec((None, w.shape[1] // steps, w.shape[2]), lambda r: (layer, r, 0))

    def spec_out(w):
        return pl.BlockSpec((w.shape[1] // steps, w.shape[2]), lambda r: (r, 0))

    ws = (wg, wu, wd)
    return pl.pallas_call(
        _cast3_kernel,
        grid=(steps,),
        in_specs=[spec_in(w) for w in ws],
        out_specs=[spec_out(w) for w in ws],
        out_shape=[jax.ShapeDtypeStruct(w.shape[1:], BF16) for w in ws],
        compiler_params=_params(("parallel",)),
        name="ffn_weights_bf16",
    )(*ws)


def _in_proj_kernel(x_ref, g_ref, wa_ref, wb_ref, oa_ref, ob_ref):
    xn = (_rms(x_ref[...]) * g_ref[...]).astype(BF16)
    oa_ref[...] = _dot(xn, wa_ref[...]).astype(oa_ref.dtype)
    ob_ref[...] = _dot(xn, wb_ref[...])


def _in_proj(x, g, w_a, w_b):
    n, d = x.shape
    tm = min(ROW_TILE, n)
    return pl.pallas_call(
        _in_proj_kernel,
        grid=(n // tm,),
        in_specs=[
            pl.BlockSpec((tm, d), lambda i: (i, 0)),
            pl.BlockSpec((1, d), lambda i: (0, 0)),
            pl.BlockSpec((d, A_WIDTH), lambda i: (0, 0), pipeline_mode=pl.Buffered(1)),
            pl.BlockSpec((d, B_WIDTH), lambda i: (0, 0), pipeline_mode=pl.Buffered(1)),
        ],
        out_specs=[
            pl.BlockSpec((tm, A_WIDTH), lambda i: (i, 0)),
            pl.BlockSpec((tm, B_WIDTH), lambda i: (i, 0)),
        ],
        out_shape=[jax.ShapeDtypeStruct((n, A_WIDTH), BF16),
                   jax.ShapeDtypeStruct((n, B_WIDTH), F32)],
        compiler_params=_params(("parallel",)),
        name="in_proj",
    )(x, g, w_a, w_b)


def _out_proj_kernel(h_ref, a_ref, s_ref, p_ref, w_ref, g_ref, o_ref):
    part = h_ref.shape[0] // OUT_ROW_SPLIT
    for r in range(OUT_ROW_SPLIT):
        rows = slice(r * part, (r + 1) * part)
        m = _dot(a_ref[rows, :], w_ref[0:NSA_WIDTH, :])
        m += _dot(s_ref[rows, :], w_ref[NSA_WIDTH:NSA_WIDTH + SSM_WIDTH, :])
        m += _dot(p_ref[rows, :], w_ref[NSA_WIDTH + SSM_WIDTH:, :])
        o_ref[rows, :] = h_ref[rows, :] + _rms(m) * g_ref[...]


def _out_proj(h, o_nsa, o_ssm, o_pool, w, g):
    n, d = h.shape
    tm = min(OUT_ROW_TILE, n)
    return pl.pallas_call(
        _out_proj_kernel,
        grid=(n // tm,),
        in_specs=[
            pl.BlockSpec((tm, d), lambda i: (i, 0)),
            pl.BlockSpec((tm, NSA_WIDTH), lambda i: (i, 0)),
            pl.BlockSpec((tm, SSM_WIDTH), lambda i: (i, 0)),
            pl.BlockSpec((tm, POOL_WIDTH), lambda i: (i, 0)),
            pl.BlockSpec((d, d), lambda i: (0, 0), pipeline_mode=pl.Buffered(1)),
            pl.BlockSpec((1, d), lambda i: (0, 0)),
        ],
        out_specs=pl.BlockSpec((tm, d), lambda i: (i, 0)),
        out_shape=jax.ShapeDtypeStruct((n, d), F32),
        compiler_params=_params(("parallel",)),
        name="out_proj",
    )(h, o_nsa, o_ssm, o_pool, w, g)


def _pool_kernel(halo_ref, x_ref, w_ref, scale_ref, o_ref, *, tiles_per_seq):
    tile = x_ref.shape[0]
    it = pl.program_id(0) % tiles_per_seq
    halo = jnp.where(it == 0, 0.0, halo_ref[...])
    ext = jnp.concatenate([halo, x_ref[...]], axis=0)
    pos = it * tile + lax.broadcasted_iota(jnp.int32, (tile, POOL_GROUP), 0)
    for gi, w in enumerate(POOL_WINDOWS):
        xg = ext[:, gi * POOL_GROUP:(gi + 1) * POOL_GROUP]
        s, k = xg, 1
        while k < w:
            s = s + pltpu.roll(s, k, axis=0)
            k *= 2
        cnt = jnp.minimum(pos + 1, w).astype(F32)
        dlt = s[POOL_HALO:] / cnt - xg[POOL_HALO:]
        y = _dot(dlt.astype(BF16), w_ref[gi])
        sl = slice(gi * POOL_GROUP, (gi + 1) * POOL_GROUP)
        o_ref[:, sl] = (y * scale_ref[:, sl]).astype(o_ref.dtype)


def _pool(proj_b, pool_w, pool_scale, seq_len):
    n = proj_b.shape[0]
    tile = min(POOL_TILE, seq_len)
    tiles_per_seq = seq_len // tile
    ratio = tile // POOL_HALO
    col = B_POOL // POOL_WIDTH
    return pl.pallas_call(
        functools.partial(_pool_kernel, tiles_per_seq=tiles_per_seq),
        grid=(n // tile,),
        in_specs=[
            pl.BlockSpec((POOL_HALO, POOL_WIDTH), lambda i: (jnp.maximum(i * ratio - 1, 0), col)),
            pl.BlockSpec((tile, POOL_WIDTH), lambda i: (i, col)),
            pl.BlockSpec((len(POOL_WINDOWS), POOL_GROUP, POOL_GROUP), lambda i: (0, 0, 0)),
            pl.BlockSpec((1, POOL_WIDTH), lambda i: (0, 0)),
        ],
        out_specs=pl.BlockSpec((tile, POOL_WIDTH), lambda i: (i, 0)),
        out_shape=jax.ShapeDtypeStruct((n, POOL_WIDTH), BF16),
        compiler_params=_params(("parallel",)),
        name="pool",
    )(proj_b, proj_b, pool_w, pool_scale)


def _ssd_kernel(halo_ref, xbc_ref, z_ref, dt_ref, convw_ref, convb_ref, dtb_ref, alog_ref,
                dskip_ref, normw_ref, o_ref, state_ref):
    c = pl.program_id(1)

    @pl.when(c == 0)
    def _():
        state_ref[...] = jnp.zeros_like(state_ref)

    for r in range(xbc_ref.shape[0] // SSD_CHUNK):
        rows = slice(r * SSD_CHUNK, (r + 1) * SSD_CHUNK)
        if r == 0:
            halo = jnp.where(c == 0, 0.0, halo_ref[...])
        else:
            halo = xbc_ref[r * SSD_CHUNK - CONV_HALO:r * SSD_CHUNK, :]
        _ssd_chunk(halo, xbc_ref[rows, :], z_ref[rows, :], dt_ref[rows, :], convw_ref, convb_ref, dtb_ref,
                   alog_ref, dskip_ref, normw_ref, o_ref.at[rows, :], state_ref)


def _ssd_chunk(halo, xbc, z, dt_raw, convw_ref, convb_ref, dtb_ref, alog_ref, dskip_ref, normw_ref,
               o_ref, state_ref):
    L = xbc.shape[0]

    ext = jnp.concatenate([halo, xbc], axis=0)
    conv = convb_ref[...] + convw_ref[SSM_CONV - 1:SSM_CONV, :] * xbc
    for k in range(1, SSM_CONV):
        conv += convw_ref[SSM_CONV - 1 - k:SSM_CONV - k, :] * pltpu.roll(ext, k, axis=0)[CONV_HALO:]
    act = _silu(conv)
    xs = act[:, :SSM_WIDTH]
    bm = act[:, SSM_WIDTH:SSM_WIDTH + SSM_GROUPS * SSM_STATE]
    cm = act[:, SSM_WIDTH + SSM_GROUPS * SSM_STATE:]

    lane = lax.broadcasted_iota(jnp.int32, (1, LANES), 1)
    xdt = dt_raw + dtb_ref[...]
    dt = jnp.maximum(xdt, 0.0) + jnp.log1p(jnp.exp(-jnp.abs(xdt)))
    a_head = jnp.where(lane < SSM_HEADS, -jnp.exp(alog_ref[...]), 0.0)
    cs = a_head * dt
    row = lax.broadcasted_iota(jnp.int32, (L, LANES), 0)
    k = 1
    while k < L:
        cs = cs + jnp.where(row >= k, pltpu.roll(cs, k, axis=0), 0.0)
        k *= 2
    cs_t = cs.T

    er = lax.broadcasted_iota(jnp.int32, (LANES, SSM_WIDTH), 0)
    ec = lax.broadcasted_iota(jnp.int32, (LANES, SSM_WIDTH), 1)
    expand = jnp.where(er == ec // SSM_HEAD_DIM, 1.0, 0.0).astype(BF16)
    dtx = _dot_exact_rhs(dt, expand)
    csx = _dot_exact_rhs(cs, expand)
    cs_last = csx[L - 1:L, :]

    x_dt = xs * dtx
    x_end = (x_dt * jnp.exp(cs_last - csx)).astype(BF16)
    x_dt16 = x_dt.astype(BF16)
    ecs = jnp.exp(csx)
    li = lax.broadcasted_iota(jnp.int32, (L, L), 0)
    si = lax.broadcasted_iota(jnp.int32, (L, L), 1)
    gw = SSM_WIDTH // SSM_GROUPS
    hpg = SSM_HEADS // SSM_GROUPS
    lane_g = lax.broadcasted_iota(jnp.int32, (1, gw), 1)
    ys = []
    for g in range(SSM_GROUPS):
        bg = bm[:, g * SSM_STATE:(g + 1) * SSM_STATE]
        cg = cm[:, g * SSM_STATE:(g + 1) * SSM_STATE].astype(BF16)
        gs = slice(g * gw, (g + 1) * gw)
        cb = _dot_nt(cg, bg.astype(BF16))
        y_g = jnp.zeros((L, gw), F32)
        for hh in range(hpg):
            h = g * hpg + hh
            seg = cs[:, h:h + 1] - cs_t[h:h + 1, :]
            m_h = (cb * jnp.exp(jnp.where(li >= si, seg, NEG))).astype(BF16)
            x_h = jnp.where(lane_g // SSM_HEAD_DIM == hh, x_dt16[:, gs], jnp.zeros((), BF16))
            y_g += _dot(m_h, x_h)
        st = state_ref[g]
        y_g += _dot(cg, st.astype(BF16)) * ecs[:, gs]
        state_ref[g] = jnp.exp(cs_last[:, gs]) * st + _dot(bg.T.astype(BF16), x_end[:, gs])
        ys.append(y_g)
    y = jnp.concatenate(ys, axis=1) + dskip_ref[...] * xs
    y = y * _silu(z)
    outs = []
    for g in range(SSM_GROUPS):
        outs.append(_rms(y[:, g * gw:(g + 1) * gw]))
    o_ref[...] = (jnp.concatenate(outs, axis=1) * normw_ref[...]).astype(o_ref.dtype)


def _ssd(proj_b, conv_w, conv_b, dt_bias, a_log, d_skip, norm_w, batch, seq_len):
    n = proj_b.shape[0]
    L = min(SSD_STEP_ROWS, seq_len)
    nch = seq_len // L
    ratio = L // CONV_HALO
    return pl.pallas_call(
        _ssd_kernel,
        grid=(batch, nch),
        in_specs=[
            pl.BlockSpec((CONV_HALO, SSM_CONV_DIM),
                         lambda b, c: (jnp.maximum((b * nch + c) * ratio - 1, 0), B_XBC // SSM_CONV_DIM)),
            pl.BlockSpec((L, SSM_CONV_DIM), lambda b, c: (b * nch + c, B_XBC // SSM_CONV_DIM)),
            pl.BlockSpec((L, SSM_WIDTH), lambda b, c: (b * nch + c, B_Z // SSM_WIDTH)),
            pl.BlockSpec((L, LANES), lambda b, c: (b * nch + c, B_MISC // LANES)),
            pl.BlockSpec((SSM_CONV, SSM_CONV_DIM), lambda b, c: (0, 0)),
            pl.BlockSpec((1, SSM_CONV_DIM), lambda b, c: (0, 0)),
            pl.BlockSpec((1, LANES), lambda b, c: (0, 0)),
            pl.BlockSpec((1, LANES), lambda b, c: (0, 0)),
            pl.BlockSpec((1, SSM_WIDTH), lambda b, c: (0, 0)),
            pl.BlockSpec((1, SSM_WIDTH), lambda b, c: (0, 0)),
        ],
        out_specs=pl.BlockSpec((L, SSM_WIDTH), lambda b, c: (b * nch + c, 0)),
        out_shape=jax.ShapeDtypeStruct((n, SSM_WIDTH), BF16),
        scratch_shapes=[pltpu.VMEM((SSM_GROUPS, SSM_STATE, SSM_WIDTH // SSM_GROUPS), F32)],
        compiler_params=_params(("parallel", "arbitrary")),
        name="ssd",
    )(proj_b, proj_b, proj_b, proj_b, conv_w, conv_b, dt_bias, a_log, d_skip, norm_w)


def _rope(x, cos, sin_signed):
    return x * cos + pltpu.roll(x, HEAD_DIM // 2, axis=1) * sin_signed


def _compress(src_ref, w1_ref, w2_ref, pe_ref):
    n_cmp = src_ref.shape[0] // CMP_STRIDE
    xs = [src_ref[pl.ds(p, n_cmp, stride=CMP_STRIDE), :] for p in range(CMP_STRIDE)]
    x = jnp.concatenate(xs, axis=1).astype(BF16)
    half = CMP_STRIDE * HEAD_DIM
    first = _dot(x, w1_ref[0:half, :])
    second = _dot(x, w1_ref[half:2 * half, :])
    pe = jnp.broadcast_to(pe_ref[...], (SUBLANES, 2 * half)).astype(BF16)
    pe_term = _dot(pe, w1_ref[...])[0:1, :]
    pre = first + pltpu.roll(second, n_cmp - 1, axis=0) + pe_term
    return _dot(_silu(pre).astype(BF16), w2_ref[...])


def _nsa_kernel(q_ref, gate_ref, kc_src_ref, vc_src_ref, ks_ref, vs_ref, kw_ref, vw_ref,
                cos_ref, sin_ref, w1k_ref, w2k_ref, w1v_ref, w2v_ref, pek_ref, pev_ref,
                o_ref,
                kc_scr, vct_scr, kx_scr, kwp_scr, vst_scr, vwt_scr, qx_scr, score_scr, rank_scr,
                m_scr, l_scr, acc_scr, oct_scr, owt_scr, s_scr, band_scr):
    tq = q_ref.shape[0]
    seq = ks_ref.shape[0]
    tk = min(SLC_KEY_TILE, seq)
    nh = NSA_HPG
    n_cmp, n_slc = seq // CMP_STRIDE, seq // SLC_LEN
    topk = min(SLC_TOPK, n_slc)
    qi = pl.program_id(2)
    q0 = qi * tq
    qk_scale = HEAD_DIM ** -0.5 * LOG2E

    @pl.when(qi == 0)
    def _():
        kc_scr[...] = _compress(kc_src_ref, w1k_ref, w2k_ref, pek_ref).astype(BF16)
        vct_scr[...] = _compress(vc_src_ref, w1v_ref, w2v_ref, pev_ref).T.astype(BF16)
        kwp_scr[0:WIN_LEN, :] = jnp.zeros((WIN_LEN, HEAD_DIM), BF16)
        for j in range(WIN_LEN // tq):
            vwt_scr[j] = jnp.zeros((HEAD_DIM, tq), BF16)
        kc_i = lax.broadcasted_iota(jnp.int32, (tq + WIN_LEN, 1), 0)
        qr_i = lax.broadcasted_iota(jnp.int32, (1, tq), 1)
        band_scr[...] = jnp.where((kc_i > qr_i) & (kc_i <= qr_i + WIN_LEN), 0.0, NEG)

        def prep_rows(i, carry):
            rows = pl.ds(pl.multiple_of(i * tq, tq), tq)
            prows = pl.ds(pl.multiple_of(WIN_LEN + i * tq, tq), tq)
            cos, sin = cos_ref[rows, :], sin_ref[rows, :]
            kx_scr[rows, 0:HEAD_DIM] = _rope(ks_ref[rows, :].astype(F32), cos, sin).astype(BF16)
            blk = (i * tq + lax.broadcasted_iota(jnp.int32, (tq, LANES), 0)) // SLC_LEN
            lane = lax.broadcasted_iota(jnp.int32, (tq, LANES), 1)
            kx_scr[rows, HEAD_DIM:2 * HEAD_DIM] = jnp.where(lane == blk, 1.0, 0.0).astype(BF16)
            kwp_scr[prows, :] = _rope(kw_ref[rows, :].astype(F32), cos, sin).astype(BF16)
            eye_d = jnp.where(lax.broadcasted_iota(jnp.int32, (HEAD_DIM, HEAD_DIM), 0)
                              == lax.broadcasted_iota(jnp.int32, (HEAD_DIM, HEAD_DIM), 1), 1.0, 0.0).astype(BF16)
            vst_scr[i] = _dot_nt(eye_d, vs_ref[rows, :]).astype(BF16)
            vwt_scr[i + WIN_LEN // tq] = _dot_nt(eye_d, vw_ref[rows, :]).astype(BF16)
            return carry

        lax.fori_loop(0, seq // tq, prep_rows, 0)

    qrows = pl.ds(pl.multiple_of(q0, tq), tq)
    qlane = lax.broadcasted_iota(jnp.int32, (1, tq), 1)
    tpos = q0 + qlane

    def q_rows(h):
        return slice(h * tq, (h + 1) * tq)

    cos_q, sin_q = cos_ref[qrows, :], sin_ref[qrows, :]
    for h in range(nh):
        hs = slice(h * HEAD_DIM, (h + 1) * HEAD_DIM)
        qx_scr[q_rows(h), 0:HEAD_DIM] = (_rope(q_ref[:, hs].astype(F32), cos_q, sin_q) * qk_scale).astype(BF16)

    def heads(x):
        return jnp.concatenate([x] * nh, axis=1)

    q_stack = jnp.concatenate([q_ref[:, h * HEAD_DIM:(h + 1) * HEAD_DIM] for h in range(nh)], axis=0)
    nblk = lax.broadcasted_iota(jnp.int32, (n_cmp, 1), 0)
    cmp_bias = jnp.where((nblk * CMP_STRIDE + (CMP_LEN - 1)) <= tpos, 0.0, NEG)
    any_visible = jnp.where(tpos >= CMP_LEN - 1, 1.0, 0.0)
    s = _dot_nt(kc_scr[...], q_stack) * qk_scale + heads(cmp_bias)
    e = jnp.exp2(s - jnp.max(s, axis=0, keepdims=True))
    p = e * (heads(any_visible) / jnp.sum(e, axis=0, keepdims=True))
    p_sum = p[:, 0:tq]
    for h in range(1, nh):
        p_sum += p[:, h * tq:(h + 1) * tq]
    oct_scr[...] = _dot(vct_scr[...], p.astype(BF16))

    nr = lax.broadcasted_iota(jnp.int32, (n_slc, n_cmp), 1) * CMP_STRIDE
    jr = lax.broadcasted_iota(jnp.int32, (n_slc, n_cmp), 0) * SLC_LEN
    overlap_t = jnp.where((nr < jr + SLC_LEN) & (nr + CMP_LEN > jr), 1.0, 0.0).astype(BF16)
    ps_hi = p_sum.astype(BF16)
    ps_lo = (p_sum - ps_hi.astype(F32)).astype(BF16)
    imp_t = _dot(overlap_t, ps_hi) + _dot(overlap_t, ps_lo)

    jblk = lax.broadcasted_iota(jnp.int32, (n_slc, tq), 0)
    qblk = (q0 + lax.broadcasted_iota(jnp.int32, (n_slc, tq), 1)) // SLC_LEN
    valid = jblk <= qblk
    forced = (jblk == 0) | (jblk == qblk) | (jblk == qblk - 1)
    score_scr[...] = jnp.where(valid, jnp.where(forced, 1e9, imp_t), -1e9)
    rank_scr[...] = jnp.zeros_like(rank_scr)
    ng = n_slc // SUBLANES
    g_last = ((q0 + tq - 1) // SLC_LEN) // SUBLANES
    row8 = lax.broadcasted_iota(jnp.int32, (SUBLANES, tq), 0)
    for gp in range(ng):
        @pl.when(gp <= g_last)
        def _():
            cnt = [jnp.zeros((SUBLANES, tq), F32) for _ in range(ng)]
            for jj in range(SUBLANES):
                jp = gp * SUBLANES + jj
                other = jnp.broadcast_to(score_scr[jp:jp + 1, :], (SUBLANES, tq))
                for g in range(ng):
                    sc = score_scr[g * SUBLANES:(g + 1) * SUBLANES, :]
                    if g > gp:
                        ahead = other >= sc
                    elif g < gp:
                        ahead = other > sc
                    else:
                        ahead = (other > sc) | ((other == sc) & (row8 > jj))
                    cnt[g] += jnp.where(ahead, 1.0, 0.0)
            for g in range(ng):
                rank_scr[g * SUBLANES:(g + 1) * SUBLANES, :] += cnt[g]

    wspan = tq + WIN_LEN
    wrows = pl.ds(pl.multiple_of(q0, tq), wspan)
    c = lax.broadcasted_iota(jnp.int32, (wspan, 1), 0)
    wb = band_scr[...] + jnp.where(c + q0 >= WIN_LEN, 0.0, NEG)
    s = _dot_nt(kwp_scr[wrows, :], qx_scr[:, 0:HEAD_DIM]) + heads(wb)
    p = jnp.exp2(s - jnp.max(s, axis=0, keepdims=True))
    pb = p.astype(BF16)
    o = _dot(vwt_scr[qi], pb[0:tq])
    for j in range(1, wspan // tq):
        o += _dot(vwt_scr[qi + j], pb[j * tq:(j + 1) * tq])
    owt_scr[...] = o / jnp.sum(p, axis=0, keepdims=True)

    sel_t = jnp.where(valid & (rank_scr[...] < topk), 1.0, 0.0).astype(BF16)
    sel_pad = jnp.concatenate([sel_t, jnp.zeros((LANES - n_slc, tq), BF16)], axis=0)
    ri = lax.broadcasted_iota(jnp.int32, (tq, tq), 0)
    ci = lax.broadcasted_iota(jnp.int32, (tq, tq), 1)
    eye = jnp.where(ri == ci, 1.0, 0.0).astype(BF16)
    sel = _dot_nt(eye, sel_pad)
    lane = lax.broadcasted_iota(jnp.int32, (tq, LANES), 1)
    sel_bias = jnp.where((lane < n_slc) & (sel < 0.5), NEG, 0.0).astype(BF16)

    for h in range(nh):
        qx_scr[q_rows(h), HEAD_DIM:2 * HEAD_DIM] = sel_bias
    m_scr[...] = jnp.full(m_scr.shape, NEG, F32)
    l_scr[...] = jnp.zeros_like(l_scr)
    acc_scr[...] = jnp.zeros_like(acc_scr)
    chunks = tk // tq

    def scores(kt):
        return _dot_nt(kx_scr[pl.ds(pl.multiple_of(kt * tk, tk), tk), :], qx_scr[...])

    def softmax_update(s, kt):
        m_prev = m_scr[...]
        m_new = jnp.maximum(m_prev, jnp.max(s, axis=0, keepdims=True))
        p = jnp.exp2(s - m_new)
        alpha = jnp.exp2(m_prev - m_new)
        l_scr[...] = alpha * l_scr[...] + jnp.sum(p, axis=0, keepdims=True)
        pb = p.astype(BF16)
        pv = _dot(vst_scr[kt * chunks], pb[0:tq])
        for j in range(1, chunks):
            pv += _dot(vst_scr[kt * chunks + j], pb[j * tq:(j + 1) * tq])
        acc_scr[...] = alpha * acc_scr[...] + pv
        m_scr[...] = m_new

    n_tiles = (q0 + tq + tk - 1) // tk
    s_scr[...] = scores(0)

    def slc_step(kt):
        s = s_scr[...]
        s_next = scores(kt + 1)
        softmax_update(s, kt)
        s_scr[...] = s_next

    def slc_pair(i, carry):
        slc_step(2 * i)
        slc_step(2 * i + 1)
        return carry

    n_steps = n_tiles - 1
    lax.fori_loop(0, n_steps // 2, slc_pair, 0)

    @pl.when(n_steps % 2 == 1)
    def _():
        slc_step(n_steps - 1)

    kpos = (n_tiles - 1) * tk + lax.broadcasted_iota(jnp.int32, (tk, 1), 0)
    softmax_update(s_scr[...] + heads(jnp.where(kpos <= tpos, 0.0, NEG)), n_tiles - 1)

    gates_all = _sigmoid(gate_ref[...]).T
    per_group = 3 * nh
    gates = jnp.where(pl.program_id(1) == 0,
                      gates_all[GATE_LANE0:GATE_LANE0 + per_group],
                      gates_all[GATE_LANE0 + per_group:GATE_LANE0 + 2 * per_group])
    for h in range(nh):
        hs = slice(h * HEAD_DIM, (h + 1) * HEAD_DIM)
        ql = slice(h * tq, (h + 1) * tq)
        o = (gates[3 * h:3 * h + 1] * oct_scr[:, ql]
             + gates[3 * h + 1:3 * h + 2] * (acc_scr[:, ql] / l_scr[:, ql])
             + gates[3 * h + 2:3 * h + 3] * owt_scr[:, ql])
        o_ref[:, hs] = o.T.astype(o_ref.dtype)


def _nsa(proj_a, proj_b, cos, sin_signed, w1k, w2k, w1v, w2v, pe_k, pe_v, batch, seq_len):
    n = proj_a.shape[0]
    tq = min(ATT_TILE, seq_len)
    nq = seq_len // tq
    gw = NSA_HPG * HEAD_DIM
    n_cmp, n_slc = seq_len // CMP_STRIDE, seq_len // SLC_LEN
    assert n_slc <= LANES // 2 and n_slc % SUBLANES == 0 and seq_len % min(SLC_KEY_TILE, seq_len) == 0

    def kv_spec(col0):
        return pl.BlockSpec((seq_len, HEAD_DIM), lambda b, g, i: (b, col0 // HEAD_DIM + g))

    def full(shape):
        return pl.BlockSpec(shape, lambda b, g, i: (0,) * len(shape))

    return pl.pallas_call(
        _nsa_kernel,
        grid=(batch, NSA_GROUPS, nq),
        in_specs=[
            pl.BlockSpec((tq, gw), lambda b, g, i: (b * nq + i, A_Q // gw + g)),
            pl.BlockSpec((tq, LANES), lambda b, g, i: (b * nq + i, B_MISC // LANES)),
            kv_spec(B_KC), kv_spec(B_VC),
            kv_spec(A_KS), kv_spec(A_VS), kv_spec(A_KW), kv_spec(A_VW),
            full((seq_len, HEAD_DIM)), full((seq_len, HEAD_DIM)),
            full((CMP_LEN * HEAD_DIM, HEAD_DIM)), full((HEAD_DIM, HEAD_DIM)),
            full((CMP_LEN * HEAD_DIM, HEAD_DIM)), full((HEAD_DIM, HEAD_DIM)),
            full((1, CMP_LEN * HEAD_DIM)), full((1, CMP_LEN * HEAD_DIM)),
        ],
        out_specs=pl.BlockSpec((tq, gw), lambda b, g, i: (b * nq + i, g)),
        out_shape=jax.ShapeDtypeStruct((n, NSA_WIDTH), BF16),
        scratch_shapes=[
            pltpu.VMEM((n_cmp, HEAD_DIM), BF16),
            pltpu.VMEM((HEAD_DIM, n_cmp), BF16),
            pltpu.VMEM((seq_len, 2 * HEAD_DIM), BF16),
            pltpu.VMEM((seq_len + WIN_LEN, HEAD_DIM), BF16),
            pltpu.VMEM((seq_len // tq, HEAD_DIM, tq), BF16),
            pltpu.VMEM(((seq_len + WIN_LEN) // tq, HEAD_DIM, tq), BF16),
            pltpu.VMEM((NSA_HPG * tq, 2 * HEAD_DIM), BF16),
            pltpu.VMEM((n_slc, tq), F32),
            pltpu.VMEM((n_slc, tq), F32),
            pltpu.VMEM((1, NSA_HPG * tq), F32),
            pltpu.VMEM((1, NSA_HPG * tq), F32),
            pltpu.VMEM((HEAD_DIM, NSA_HPG * tq), F32),
            pltpu.VMEM((HEAD_DIM, NSA_HPG * tq), F32),
            pltpu.VMEM((HEAD_DIM, NSA_HPG * tq), F32),
            pltpu.VMEM((min(SLC_KEY_TILE, seq_len), NSA_HPG * tq), F32),
            pltpu.VMEM((tq + WIN_LEN, tq), F32),
        ],
        compiler_params=_params(("parallel", "parallel", "arbitrary")),
        name="nsa",
    )(proj_a, proj_b, proj_b, proj_b, proj_a, proj_a, proj_a, proj_a,
      cos, sin_signed, w1k, w2k, w1v, w2v, pe_k, pe_v)


def _rope_tables(seq_len):
    inv = ROPE_THETA ** (-jnp.arange(0, HEAD_DIM, 2, dtype=F32) / HEAD_DIM)
    ang = jnp.arange(seq_len, dtype=F32)[:, None] * inv[None, :]
    ang = jnp.concatenate([ang, ang], -1)
    sign = jnp.concatenate([-jnp.ones((HEAD_DIM // 2,), F32), jnp.ones((HEAD_DIM // 2,), F32)])
    return jnp.cos(ang), jnp.sin(ang) * sign


def _split_w_in(w_in):
    offs = [0]
    for wd in IN_WIDTHS:
        offs.append(offs[-1] + wd)
    (q, kc, vc, ks, vs, kw, vw, gates, z, xbc, dt, pv) = [
        w_in[:, offs[i]:offs[i + 1]] for i in range(len(IN_WIDTHS))]
    d = w_in.shape[0]

    def pad_to(w, width):
        return jnp.pad(w, ((0, 0), (0, width - w.shape[1])))

    w_a = jnp.concatenate([q, ks, vs, kw, vw], axis=1)
    w_b = jnp.concatenate([xbc, z, pv, pad_to(jnp.concatenate([dt, gates], axis=1), LANES), kc, vc], axis=1)
    return w_a.astype(BF16), w_b.astype(BF16)


def _pad_lanes(v):
    return jnp.pad(v, (0, LANES - v.shape[0]))[None, :]


def kernel(x, ffn1_norm_pre, ffn1_norm_post, ffn1_w_gate, ffn1_w_up, ffn1_w_down, mix_norm_pre, mix_norm_post, w_in, cmp_pe_k, cmp_pe_v, cmp_k_w1, cmp_k_w2, cmp_v_w1, cmp_v_w2, ssm_conv_w, ssm_conv_b, ssm_dt_bias, ssm_a_log, ssm_d, ssm_norm, pool_w, pool_scale, w_out, ffn2_norm_pre, ffn2_norm_post, ffn2_w_gate, ffn2_w_up, ffn2_w_down):
    batch, seq_len, d = x.shape
    depth = w_in.shape[0]
    assert d == D_MODEL and ffn1_w_gate.shape[1:] == (D_MODEL, D_FF) and w_in.shape[1:] == (D_MODEL, sum(IN_WIDTHS))
    assert seq_len % max(ATT_TILE, SLC_KEY_TILE, SSD_STEP_ROWS, POOL_TILE) == 0
    assert (batch * seq_len) % max(FFN_ROW_TILE, ROW_TILE, OUT_ROW_TILE) == 0 and D_FF % FFN_TILE == 0
    cos, sin_signed = _rope_tables(seq_len)
    h = x.reshape(batch * seq_len, d)
    for i in range(depth):
        h = _ffn(h, ffn1_norm_pre[i][None], ffn1_norm_post[i][None],
                 *_ffn_weights_bf16(ffn1_w_gate, ffn1_w_up, ffn1_w_down, i))
        proj_a, proj_b = _in_proj(h, mix_norm_pre[i][None], *_split_w_in(w_in[i]))
        o_nsa = _nsa(proj_a, proj_b, cos, sin_signed,
                     cmp_k_w1[i].astype(BF16), cmp_k_w2[i].astype(BF16),
                     cmp_v_w1[i].astype(BF16), cmp_v_w2[i].astype(BF16),
                     cmp_pe_k[i].reshape(1, -1), cmp_pe_v[i].reshape(1, -1), batch, seq_len)
        o_ssm = _ssd(proj_b, ssm_conv_w[i], ssm_conv_b[i][None], _pad_lanes(ssm_dt_bias[i]),
                     _pad_lanes(ssm_a_log[i]), jnp.repeat(ssm_d[i], SSM_HEAD_DIM)[None],
                     ssm_norm[i][None], batch, seq_len)
        o_pool = _pool(proj_b, pool_w[i].astype(BF16), pool_scale[i][None], seq_len)
        h = _out_proj(h, o_nsa, o_ssm, o_pool, w_out[i].astype(BF16), mix_norm_post[i][None])
        h = _ffn(h, ffn2_norm_pre[i][None], ffn2_norm_post[i][None],
                 *_ffn_weights_bf16(ffn2_w_gate, ffn2_w_up, ffn2_w_down, i))
    return h.reshape(batch, seq_len, d)
```
